```python
import math
import jax, jax.numpy as jnp
from jax import lax
import numpy as np

D_MODEL = 1024
BATCH = 8
SEQ = 2048
DEPTH = 2
DEC_BATCH = 128
DEC_SEQ = 8
PAST_LEN = 16384
PAGE_SIZE = 128

N_MIXERS = 2
N_POOL = (DEPTH + 1) // 2
N_HGRN = DEPTH // 2
N_DENSE = (DEPTH + 1) // 2
N_MOE = DEPTH // 2
POOL_WINDOWS = (2, 4, 8, 16)
POOL_GROUPS = len(POOL_WINDOWS)
POOL_GW = D_MODEL // POOL_GROUPS
POOL_BUF = max(POOL_WINDOWS) - 1
HG_EXPAND = 128
HG_HEADS = D_MODEL // HG_EXPAND
HG_DK = HG_EXPAND
HG_DV = D_MODEL // HG_HEADS
HG_K = HG_HEADS * HG_DK
HG_V = HG_HEADS * HG_DV
HG_CHUNK = 32
RMS_EPS = 1e-6
MEM_LEN = 256
XA_HEADS = 4
XA_HD = D_MODEL // XA_HEADS
XA_SCALE = XA_HD ** -0.5
D_FF = 7 * D_MODEL // 2
N_EXPERTS = 8
TOP_K = 2
D_FF_EXPERT = D_FF
ALPHA = (2 * DEPTH) ** 0.25
BETA = (8 * DEPTH) ** -0.25
LN_EPS = 1e-5

kernel_name = 'pool_hgrn2_memxattn_moe_decoder_step'


def layer_norm(x, g, b):
    xf = x.astype(jnp.float32)
    mu = jnp.mean(xf, axis=-1, keepdims=True)
    var = jnp.mean(jnp.square(xf - mu), axis=-1, keepdims=True)
    return ((xf - mu) * lax.rsqrt(var + LN_EPS) * g + b).astype(x.dtype)


def pool_mixer(x, prev, n_prev_valid, w_grp, scale):
    B, T, _ = x.shape
    xx = jnp.concatenate([prev.astype(x.dtype), x], axis=1)
    csum = jnp.cumsum(xx.astype(jnp.float32), axis=1)
    csum = jnp.pad(csum, ((0, 0), (1, 0), (0, 0)))
    end = csum[:, POOL_BUF + 1:, :]
    t = jnp.arange(T)
    means = []
    for g, w in enumerate(POOL_WINDOWS):
        sl = slice(g * POOL_GW, (g + 1) * POOL_GW)
        start = csum[:, POOL_BUF + 1 - w: POOL_BUF + 1 - w + T, sl]
        cnt = jnp.minimum(t + 1 + n_prev_valid, w).astype(jnp.float32)
        means.append((end[..., sl] - start) / cnt[None, :, None])
    pooled = jnp.stack(means, axis=2)
    tok = x.reshape(B, T, POOL_GROUPS, POOL_GW).astype(jnp.float32)
    mixed = jnp.einsum('btgc,gcd->btgd', (pooled - tok).astype(x.dtype), w_grp)
    mixed = mixed.reshape(B, T, D_MODEL) * scale
    return mixed.astype(x.dtype), xx[:, -POOL_BUF:, :]


def hgrn_lower_bound(lb_param, i):
    c = jnp.cumsum(jax.nn.softmax(lb_param.astype(jnp.float32), axis=0), axis=0)
    return c[i] - c[0]


def hgrn2_mixer(x, s0, w_in, lb, norm_g, w_o):
    B, T, _ = x.shape
    f32 = jnp.float32
    proj = jnp.einsum('btd,de->bte', x, w_in)
    q = proj[..., :HG_K].astype(f32)
    f_pre = proj[..., HG_K:2 * HG_K].astype(f32)
    inp = proj[..., 2 * HG_K:2 * HG_K + HG_V].astype(f32)
    gate = proj[..., 2 * HG_K + HG_V:].astype(f32)
    f = lb + (1.0 - lb) * jax.nn.sigmoid(f_pre)
    log_f = jnp.log(f)
    k = 1.0 - f
    C = math.gcd(T, HG_CHUNK)
    nC = T // C

    def to_chunks(a, d):
        return a.reshape(B, nC, C, HG_HEADS, d).transpose(1, 0, 3, 2, 4)

    qc, kc, lc = to_chunks(q, HG_DK), to_chunks(k, HG_DK), to_chunks(log_f, HG_DK)
    vc = to_chunks(inp, HG_DV)
    causal = jnp.tril(jnp.ones((C, C), dtype=bool))

    def chunk_step(S, blk):
        qb, kb, vb, lfb = blk
        bcum = jnp.cumsum(lfb, axis=2)
        q_dec = qb * jnp.exp(bcum)
        k_inv = kb * jnp.exp(-bcum)
        A = jnp.einsum('bhtk,bhsk->bhts', q_dec, k_inv)
        A = jnp.where(causal, A, 0.0)
        o = jnp.einsum('bhts,bhsv->bhtv', A, vb) + jnp.einsum('bhtk,bhkv->bhtv', q_dec, S)
        b_last = bcum[:, :, -1, :]
        k_dec = kb * jnp.exp(b_last[:, :, None, :] - bcum)
        S_new = jnp.exp(b_last)[..., None] * S + jnp.einsum('bhsk,bhsv->bhkv', k_dec, vb)
        return S_new, o

    s_fin, oc = lax.scan(chunk_step, s0.astype(f32), (qc, kc, vc, lc))
    o = oc.transpose(1, 0, 3, 2, 4).reshape(B, T, HG_HEADS, HG_DV)
    o = o * lax.rsqrt(jnp.mean(jnp.square(o), axis=-1, keepdims=True) + RMS_EPS) * norm_g
    o = o.reshape(B, T, HG_V) * jax.nn.silu(gate)
    out = jnp.einsum('bte,ed->btd', o.astype(x.dtype), w_o)
    return out.astype(x.dtype), s_fin.astype(s0.dtype)


def mem_kv(mem, w_kv):
    B, M, _ = mem.shape
    kv = jnp.einsum('bmd,de->bme', mem, w_kv)
    k = kv[..., :XA_HEADS * XA_HD].reshape(B, M, XA_HEADS, XA_HD)
    v = kv[..., XA_HEADS * XA_HD:].reshape(B, M, XA_HEADS, XA_HD)
    return k, v


def mem_attend(x, k, v, w_q, w_o):
    B, T, _ = x.shape
    q = jnp.einsum('btd,de->bte', x, w_q).reshape(B, T, XA_HEADS, XA_HD)
    s = jnp.einsum('bthd,bmhd->bhtm', q, k).astype(jnp.float32) * XA_SCALE
    p = jax.nn.softmax(s, axis=-1).astype(v.dtype)
    o = jnp.einsum('bhtm,bmhd->bthd', p, v).reshape(B, T, XA_HEADS * XA_HD)
    return jnp.einsum('bte,ed->btd', o, w_o).astype(x.dtype)


def swiglu(x, w_gu, w_d):
    gu = jnp.einsum('btd,df->btf', x, w_gu)
    n = w_d.shape[0]
    h = jax.nn.silu(gu[..., :n]) * gu[..., n:]
    return jnp.einsum('btf,fd->btd', h, w_d)


def moe_swiglu(x, w_router, w_gu, w_d):
    logits = jnp.einsum('btd,de->bte', x, w_router).astype(jnp.float32)
    probs = jax.nn.softmax(logits, axis=-1)
    top_p, top_i = lax.top_k(probs, TOP_K)
    top_p = top_p / jnp.sum(top_p, axis=-1, keepdims=True)
    gates = jnp.sum(jax.nn.one_hot(top_i, N_EXPERTS, dtype=jnp.float32) * top_p[..., None], axis=-2)
    out = jnp.zeros(x.shape, jnp.float32)
    for e in range(N_EXPERTS):
        out = out + gates[..., e:e + 1] * swiglu(x, w_gu[e], w_d[e]).astype(jnp.float32)
    return out.astype(x.dtype)


def trunk(x, mem_k, mem_v, pool_prev, n_prev_valid, hg_state, w):
    pool_new, hg_new = [], []
    for i in range(DEPTH):
        j = i // N_MIXERS
        if i % N_MIXERS == 0:
            h, buf = pool_mixer(x, pool_prev[j], n_prev_valid, w['pool_w'][j], w['pool_scale'][j])
            pool_new.append(buf)
        else:
            lb = hgrn_lower_bound(w['hg_lb'], i)
            h, s = hgrn2_mixer(x, hg_state[j], w['hg_w_in'][j], lb, w['hg_norm_g'][j], w['hg_w_o'][j])
            hg_new.append(s)
        x = layer_norm(ALPHA * x + h, w['ln_g'][i, 0], w['ln_b'][i, 0])
        a = mem_attend(x, mem_k[i], mem_v[i], w['xa_w_q'][i], w['xa_w_o'][i])
        x = layer_norm(ALPHA * x + a, w['ln_g'][i, 1], w['ln_b'][i, 1])
        if i % 2 == 0:
            f = swiglu(x, w['ffn_w_gu'][j], w['ffn_w_d'][j]).astype(x.dtype)
        else:
            f = moe_swiglu(x, w['moe_w_router'][j], w['moe_w_gu'][j], w['moe_w_d'][j])
        x = layer_norm(ALPHA * x + f, w['ln_g'][i, 2], w['ln_b'][i, 2])
    return x, jnp.stack(pool_new), jnp.stack(hg_new)


def setup_inputs(seed: int = 0) -> dict:
    key = jax.random.key(seed)
    ks = jax.random.split(key, 24)
    D = D_MODEL

    def nrm(k, shape, scale):
        return jax.random.normal(k, shape, jnp.float32) * scale

    return {
        'x_prompt': nrm(ks[0], (BATCH, SEQ, D), 1.0),
        'x_sample': nrm(ks[1], (DEC_BATCH, DEC_SEQ, D), 1.0),
        'mem_prompt': nrm(ks[2], (BATCH, MEM_LEN, D), 1.0),
        'state_pool': nrm(ks[3], (N_POOL, DEC_BATCH, POOL_BUF, D), 1.0),
        'state_hgrn': nrm(ks[4], (N_HGRN, DEC_BATCH, HG_HEADS, HG_DK, HG_DV), 0.5),
        'cache_mem_k': nrm(ks[5], (DEPTH, DEC_BATCH, MEM_LEN, XA_HEADS, XA_HD), 1.0),
        'cache_mem_v': nrm(ks[6], (DEPTH, DEC_BATCH, MEM_LEN, XA_HEADS, XA_HD), 1.0),
        'pool_w': nrm(ks[7], (N_POOL, POOL_GROUPS, POOL_GW, POOL_GW), POOL_GW ** -0.5 * BETA),
        'pool_scale': 1.0 + nrm(ks[8], (N_POOL, D), 0.02),
        'hg_w_in': nrm(ks[9], (N_HGRN, D, 2 * HG_K + 2 * HG_V), D ** -0.5),
        'hg_lb': nrm(ks[10], (DEPTH, HG_K), 0.1),
        'hg_norm_g': 1.0 + nrm(ks[11], (N_HGRN, HG_DV), 0.02),
        'hg_w_o': nrm(ks[12], (N_HGRN, HG_V, D), HG_V ** -0.5 * BETA),
        'xa_w_q': nrm(ks[13], (DEPTH, D, XA_HEADS * XA_HD), D ** -0.5),
        'xa_w_kv': nrm(ks[14], (DEPTH, D, 2 * XA_HEADS * XA_HD), D ** -0.5),
        'xa_w_o': nrm(ks[15], (DEPTH, XA_HEADS * XA_HD, D), (XA_HEADS * XA_HD) ** -0.5 * BETA),
        'ffn_w_gu': nrm(ks[16], (N_DENSE, D, 2 * D_FF), D ** -0.5),
        'ffn_w_d': nrm(ks[17], (N_DENSE, D_FF, D), D_FF ** -0.5 * BETA),
        'moe_w_router': nrm(ks[18], (N_MOE, D, N_EXPERTS), D ** -0.5),
        'moe_w_gu': nrm(ks[19], (N_MOE, N_EXPERTS, D, 2 * D_FF_EXPERT), D ** -0.5),
        'moe_w_d': nrm(ks[20], (N_MOE, N_EXPERTS, D_FF_EXPERT, D), D_FF_EXPERT ** -0.5 * BETA),
        'ln_g': 1.0 + nrm(ks[21], (DEPTH, 3, D), 0.02),
        'ln_b': nrm(ks[22], (DEPTH, 3, D), 0.02),
    }


def reference(x_prompt, x_sample, mem_prompt, state_pool, state_hgrn, cache_mem_k, cache_mem_v,
              pool_w, pool_scale, hg_w_in, hg_lb, hg_norm_g, hg_w_o, xa_w_q, xa_w_kv, xa_w_o,
              ffn_w_gu, ffn_w_d, moe_w_router, moe_w_gu, moe_w_d, ln_g, ln_b):
    w = {'pool_w': pool_w, 'pool_scale': pool_scale, 'hg_w_in': hg_w_in, 'hg_lb': hg_lb,
         'hg_norm_g': hg_norm_g, 'hg_w_o': hg_w_o, 'xa_w_q': xa_w_q, 'xa_w_o': xa_w_o,
         'ffn_w_gu': ffn_w_gu, 'ffn_w_d': ffn_w_d, 'moe_w_router': moe_w_router,
         'moe_w_gu': moe_w_gu, 'moe_w_d': moe_w_d, 'ln_g': ln_g, 'ln_b': ln_b}
    kv = [mem_kv(mem_prompt, xa_w_kv[i]) for i in range(DEPTH)]
    mem_k_prompt = jnp.stack([kv_i[0] for kv_i in kv])
    mem_v_prompt = jnp.stack([kv_i[1] for kv_i in kv])
    pool0 = jnp.zeros((N_POOL, x_prompt.shape[0], POOL_BUF, D_MODEL), x_prompt.dtype)
    hg0 = jnp.zeros((N_HGRN, x_prompt.shape[0], HG_HEADS, HG_DK, HG_DV), x_prompt.dtype)
    y_prompt, pool_prompt, hgrn_prompt = trunk(x_prompt, mem_k_prompt, mem_v_prompt, pool0, 0, hg0, w)
    y_sample, pool_sample, hgrn_sample = trunk(x_sample, cache_mem_k, cache_mem_v, state_pool, POOL_BUF, state_hgrn, w)
    return (y_prompt, y_sample, pool_prompt, hgrn_prompt, mem_k_prompt, mem_v_prompt, pool_sample, hgrn_sample)
```

```python
import functools
import math

import jax
import jax.numpy as jnp
from jax import lax
from jax.experimental import pallas as pl
from jax.experimental.pallas import tpu as pltpu

F32 = jnp.float32
BF16 = jnp.bfloat16

POOL_WINDOWS = (2, 4, 8, 16)
POOL_HALO = 16
HG_HEADS = 8
HG_CHUNK = 32
RMS_EPS = 1e-6
XA_HEADS = 4
N_EXPERTS = 8
TOP_K = 2
LN_EPS = 1e-5
LANES = 128
VMEM_LIMIT = 56 * 1024 * 1024

_NT = (((1,), (1,)), ((), ()))
_TN = (((0,), (0,)), ((), ()))


def _cparams(*sem):
    return pltpu.CompilerParams(dimension_semantics=sem, vmem_limit_bytes=VMEM_LIMIT)


def _layer_norm(z, g, b):
    mu = jnp.mean(z, axis=-1, keepdims=True)
    zc = z - mu
    var = jnp.mean(zc * zc, axis=-1, keepdims=True)
    return zc * lax.rsqrt(var + LN_EPS) * g + b


def _dot(a, b):
    return jnp.dot(a, b, preferred_element_type=F32)


def _kv_kernel(mem_ref, w_ref, k_ref, v_ref):
    kv = _dot(mem_ref[...].astype(BF16), w_ref[0])
    n = k_ref.shape[-1]
    k_ref[0] = kv[:, :n]
    v_ref[0] = kv[:, n:]


def _mem_kv(mem2d, w_kv):
    rows, d = mem2d.shape
    depth, _, e2 = w_kv.shape
    e = e2 // 2
    tm = min(rows, 512)
    out = jax.ShapeDtypeStruct((depth, rows, e), F32)
    return pl.pallas_call(
        _kv_kernel,
        grid=(depth, rows // tm),
        in_specs=[pl.BlockSpec((tm, d), lambda l, i: (i, 0)),
                  pl.BlockSpec((1, d, e2), lambda l, i: (l, 0, 0))],
        out_specs=[pl.BlockSpec((1, tm, e), lambda l, i: (l, i, 0)),
                   pl.BlockSpec((1, tm, e), lambda l, i: (l, i, 0))],
        out_shape=[out, out],
        compiler_params=_cparams("parallel", "parallel"),
        name="mem_kv",
    )(mem2d, w_kv)


def _window_sums(a, w):
    s, k = a, 1
    while k < w:
        s = s + pltpu.roll(s, k, 0)
        k *= 2
    return s


def _pool_mix(pooled_minus_tok, w_ref, scale):
    parts = [_dot(pooled_minus_tok[g].astype(BF16), w_ref[g]) for g in range(len(POOL_WINDOWS))]
    return jnp.concatenate(parts, axis=1) * scale


def _pool_prompt_kernel(x_ref, halo_ref, w_ref, scale_ref, g_ref, b_ref, o_ref, buf_ref, *, alpha):
    t = pl.program_id(1)
    tt = x_ref.shape[1]
    gw = w_ref.shape[-1]
    x = x_ref[0]
    buf_ref[0:POOL_HALO, :] = jnp.where(t > 0, halo_ref[0], 0.0)
    buf_ref[POOL_HALO:, :] = x
    pos = t * tt + lax.broadcasted_iota(jnp.int32, (tt, 1), 0)
    diffs = []
    for g, w in enumerate(POOL_WINDOWS):
        sl = slice(g * gw, (g + 1) * gw)
        s = _window_sums(buf_ref[:, sl], w)[POOL_HALO:, :]
        cnt = jnp.minimum(pos + 1, w).astype(F32)
        diffs.append(s / cnt - x[:, sl])
    mixed = _pool_mix(diffs, w_ref, scale_ref[...])
    o_ref[0] = _layer_norm(alpha * x + mixed, g_ref[...], b_ref[...])


def _pool_prompt(x, pool_w, scale, ln_g, ln_b, alpha):
    b, t, d = x.shape
    tt = min(t, 512)
    hb = tt // POOL_HALO
    vec = pl.BlockSpec((1, d), lambda i, j: (0, 0))
    return pl.pallas_call(
        functools.partial(_pool_prompt_kernel, alpha=alpha),
        grid=(b, t // tt),
        in_specs=[pl.BlockSpec((1, tt, d), lambda i, j: (i, j, 0)),
                  pl.BlockSpec((1, POOL_HALO, d), lambda i, j: (i, jnp.maximum(j * hb - 1, 0), 0)),
                  pl.BlockSpec(pool_w.shape, lambda i, j: (0, 0, 0)),
                  vec, vec, vec],
        out_specs=pl.BlockSpec((1, tt, d), lambda i, j: (i, j, 0)),
        out_shape=jax.ShapeDtypeStruct(x.shape, F32),
        scratch_shapes=[pltpu.VMEM((tt + POOL_HALO, d), F32)],
        compiler_params=_cparams("parallel", "parallel"),
        name="pool_prompt",
    )(x, x, pool_w, scale, ln_g, ln_b)


def _pool_sample_kernel(x_ref, prev_ref, w_ref, scale_ref, g_ref, b_ref, o_ref, buf_ref, *, alpha, n_prev):
    nb, t, d = x_ref.shape
    gw = w_ref.shape[-1]
    rows = POOL_HALO + t
    buf_ref[:, 0:POOL_HALO, :] = prev_ref[...]
    buf_ref[:, POOL_HALO:, :] = x_ref[...]
    x = x_ref[...].reshape(nb * t, d)
    pos = lax.broadcasted_iota(jnp.int32, (nb, t, 1), 1).reshape(nb * t, 1)
    diffs = []
    for g, w in enumerate(POOL_WINDOWS):
        sl = slice(g * gw, (g + 1) * gw)
        s = _window_sums(buf_ref[:, :, sl].reshape(nb * rows, gw), w)
        s = s.reshape(nb, rows, gw)[:, POOL_HALO:, :].reshape(nb * t, gw)
        cnt = jnp.minimum(pos + 1 + n_prev, w).astype(F32)
        diffs.append(s / cnt - x[:, sl])
    mixed = _pool_mix(diffs, w_ref, scale_ref[...])
    o_ref[...] = _layer_norm(alpha * x + mixed, g_ref[...], b_ref[...]).reshape(nb, t, d)


def _pool_sample(x, prev_padded, n_prev, pool_w, scale, ln_g, ln_b, alpha):
    b, t, d = x.shape
    nb = min(b, 16)
    vec = pl.BlockSpec((1, d), lambda i: (0, 0))
    return pl.pallas_call(
        functools.partial(_pool_sample_kernel, alpha=alpha, n_prev=n_prev),
        grid=(b // nb,),
        in_specs=[pl.BlockSpec((nb, t, d), lambda i: (i, 0, 0)),
                  pl.BlockSpec((nb, POOL_HALO, d), lambda i: (i, 0, 0)),
                  pl.BlockSpec(pool_w.shape, lambda i: (0, 0, 0)),
                  vec, vec, vec],
        out_specs=pl.BlockSpec((nb, t, d), lambda i: (i, 0, 0)),
        out_shape=jax.ShapeDtypeStruct(x.shape, F32),
        scratch_shapes=[pltpu.VMEM((nb, POOL_HALO + t, d), F32)],
        compiler_params=_cparams("parallel"),
        name="pool_sample",
    )(x, prev_padded, pool_w, scale, ln_g, ln_b)


def _attn_kernel(x_ref, k_ref, v_ref, wq_ref, wo_ref, g_ref, b_ref, o_ref, oh_ref, *, alpha):
    nb, tt, d = x_ref.shape
    hd = d // XA_HEADS
    scale = hd ** -0.5
    x = x_ref[...].reshape(nb * tt, d)
    q = _dot(x.astype(BF16), wq_ref[...])
    for h in range(XA_HEADS):
        sl = slice(h * hd, (h + 1) * hd)
        qh = q[:, sl].astype(BF16).reshape(nb, tt, hd)
        kh = k_ref[:, :, sl].astype(BF16)
        vh = v_ref[:, :, sl].astype(BF16)
        s = jnp.einsum("btd,bmd->btm", qh, kh, preferred_element_type=F32) * scale
        p = jnp.exp(s - jnp.max(s, axis=-1, keepdims=True))
        p = (p / jnp.sum(p, axis=-1, keepdims=True)).astype(BF16)
        oh = jnp.einsum("btm,bmd->btd", p, vh, preferred_element_type=F32)
        oh_ref[:, sl] = oh.reshape(nb * tt, hd).astype(BF16)
    a = _dot(oh_ref[...], wo_ref[...])
    o_ref[...] = _layer_norm(alpha * x + a, g_ref[...], b_ref[...]).reshape(nb, tt, d)


def _mem_attend(x, k4, v4, layer, wq, wo, ln_g, ln_b, alpha, nb, tt):
    b, t, d = x.shape
    m = k4.shape[2]
    vec = pl.BlockSpec((1, d), lambda i, j: (0, 0))
    mat = pl.BlockSpec((d, d), lambda i, j: (0, 0))
    kv = pl.BlockSpec((None, nb, m, d), lambda i, j: (layer, i, 0, 0))
    return pl.pallas_call(
        functools.partial(_attn_kernel, alpha=alpha),
        grid=(b // nb, t // tt),
        in_specs=[pl.BlockSpec((nb, tt, d), lambda i, j: (i, j, 0)), kv, kv, mat, mat, vec, vec],
        out_specs=pl.BlockSpec((nb, tt, d), lambda i, j: (i, j, 0)),
        out_shape=jax.ShapeDtypeStruct(x.shape, F32),
        scratch_shapes=[pltpu.VMEM((nb * tt, d), BF16)],
        compiler_params=_cparams("parallel", "parallel"),
        name=f"mem_attend_l{layer}",
    )(x, k4, v4, wq, wo, ln_g, ln_b)


def _swiglu_step(x_ref, wg_ref, wu_ref, wd_ref, xb_ref, acc_ref):
    f = pl.program_id(1)

    @pl.when(f == 0)
    def _():
        xb_ref[...] = x_ref[...].astype(BF16)
        acc_ref[...] = jnp.zeros_like(acc_ref)

    xb = xb_ref[...]
    gate = _dot(xb, wg_ref[0])
    up = _dot(xb, wu_ref[0])
    h = (gate * jax.nn.sigmoid(gate) * up).astype(BF16)
    acc_ref[...] += _dot(h, wd_ref[0])


def _ffn_ln_kernel(x_ref, wg_ref, wu_ref, wd_ref, g_ref, b_ref, o_ref, xb_ref, acc_ref, *, alpha):
    _swiglu_step(x_ref, wg_ref, wu_ref, wd_ref, xb_ref, acc_ref)

    @pl.when(pl.program_id(1) == pl.num_programs(1) - 1)
    def _():
        o_ref[...] = _layer_norm(alpha * x_ref[...] + acc_ref[...], g_ref[...], b_ref[...])


def _ffn_ln(x2d, w_gu, w_d, ln_g, ln_b, alpha, tm, tf):
    n, d = x2d.shape
    tm = min(tm, n)
    f = w_d.shape[1]
    nf = f // tf
    vec = pl.BlockSpec((1, d), lambda i, j: (0, 0))
    return pl.pallas_call(
        functools.partial(_ffn_ln_kernel, alpha=alpha),
        grid=(n // tm, nf),
        in_specs=[pl.BlockSpec((tm, d), lambda i, j: (i, 0)),
                  pl.BlockSpec((1, d, tf), lambda i, j: (0, 0, j)),
                  pl.BlockSpec((1, d, tf), lambda i, j: (0, 0, nf + j)),
                  pl.BlockSpec((1, tf, d), lambda i, j: (0, j, 0)),
                  vec, vec],
        out_specs=pl.BlockSpec((tm, d), lambda i, j: (i, 0)),
        out_shape=jax.ShapeDtypeStruct((n, d), F32),
        scratch_shapes=[pltpu.VMEM((tm, d), BF16), pltpu.VMEM((tm, d), F32)],
        compiler_params=_cparams("parallel", "arbitrary"),
        name="ffn_dense",
    )(x2d, w_gu, w_gu, w_d, ln_g, ln_b)


def _expert_kernel(eid_ref, used_ref, x_ref, wg_ref, wu_ref, wd_ref, o_ref, xb_ref, acc_ref):
    i = pl.program_id(0)
    last = pl.program_id(1) == pl.num_programs(1) - 1

    @pl.when(used_ref[i] > 0)
    def _():
        _swiglu_step(x_ref, wg_ref, wu_ref, wd_ref, xb_ref, acc_ref)

        @pl.when(last)
        def _():
            o_ref[...] = acc_ref[...]

    @pl.when(jnp.logical_and(used_ref[i] == 0, last))
    def _():
        o_ref[...] = jnp.zeros_like(o_ref)


def _expert_ffn(xs, tile_eid, tile_used, w_gu, w_d, tm, tf):
    r, d = xs.shape
    f = w_d.shape[1]
    nf = f // tf
    grid_spec = pltpu.PrefetchScalarGridSpec(
        num_scalar_prefetch=2,
        grid=(r // tm, nf),
        in_specs=[pl.BlockSpec((tm, d), lambda i, j, e, u: (i, 0)),
                  pl.BlockSpec((1, d, tf), lambda i, j, e, u: (e[i], 0, j)),
                  pl.BlockSpec((1, d, tf), lambda i, j, e, u: (e[i], 0, nf + j)),
                  pl.BlockSpec((1, tf, d), lambda i, j, e, u: (e[i], j, 0))],
        out_specs=pl.BlockSpec((tm, d), lambda i, j, e, u: (i, 0)),
        scratch_shapes=[pltpu.VMEM((tm, d), BF16), pltpu.VMEM((tm, d), F32)],
    )
    return pl.pallas_call(
        _expert_kernel,
        grid_spec=grid_spec,
        out_shape=jax.ShapeDtypeStruct((r, d), F32),
        compiler_params=_cparams("parallel", "arbitrary"),
        name="ffn_experts",
    )(tile_eid, tile_used, xs, w_gu, w_gu, w_d)


def _hgrn_kernel(*refs, alpha, layer, chunk, has_state):
    if has_state:
        (x_ref, s0_ref, w_in_ref, lbp_ref, ng_ref, w_o_ref, g_ref, b_ref,
         o_ref, sfin_ref, st_ref, proj_ref, og_ref) = refs
    else:
        (x_ref, w_in_ref, lbp_ref, ng_ref, w_o_ref, g_ref, b_ref,
         o_ref, sfin_ref, st_ref, proj_ref, og_ref) = refs
        s0_ref = None
    t = pl.program_id(1)
    nb, tt, d = x_ref.shape
    heads = HG_HEADS
    dv = ng_ref.shape[-1]
    vdim = heads * dv
    kdim = (w_in_ref.shape[1] - 2 * vdim) // 2
    dk = kdim // heads
    m = nb * tt
    c = chunk
    chunks_per_seq = tt // c

    @pl.when(t == 0)
    def _():
        if has_state:
            for bi in range(nb):
                for h in range(heads):
                    st_ref[bi, h] = s0_ref[bi, h].T
        else:
            st_ref[...] = jnp.zeros_like(st_ref)

    x = x_ref[...].reshape(m, d)
    proj_ref[...] = _dot(x.astype(BF16), w_in_ref[...])

    lbp = lbp_ref[...]
    e = jnp.exp(lbp - jnp.max(lbp, axis=0, keepdims=True))
    sm = e / jnp.sum(e, axis=0, keepdims=True)
    lb = jnp.sum(sm[1:layer + 1], axis=0, keepdims=True)

    ri = lax.broadcasted_iota(jnp.int32, (c, c), 0)
    ci = lax.broadcasted_iota(jnp.int32, (c, c), 1)
    causal = ri >= ci
    tri = causal.astype(F32)
    mid = max(c // 2 - 1, 0)
    ng = ng_ref[...]

    def chunk_body(j, carry):
        r0 = pl.multiple_of(j * c, c)
        bi = j // chunks_per_seq
        rows = pl.ds(r0, c)
        q = proj_ref[rows, 0:kdim]
        f_pre = proj_ref[rows, kdim:2 * kdim]
        v = proj_ref[rows, 2 * kdim:2 * kdim + vdim]
        gate = proj_ref[rows, 2 * kdim + vdim:]
        f = lb + (1.0 - lb) * jax.nn.sigmoid(f_pre)
        k = 1.0 - f
        bcum = jnp.dot(tri, jnp.log(f), preferred_element_type=F32, precision=lax.Precision.HIGHEST)
        bmid = bcum[mid:mid + 1]
        blast = bcum[c - 1:c]
        q_dec = q * jnp.exp(bcum)
        q_rel = (q_dec * jnp.exp(-bmid)).astype(BF16)
        k_rel32 = k * jnp.exp(bmid - bcum)
        k_rel = k_rel32.astype(BF16)
        k_end = (k_rel32 * jnp.exp(blast - bmid)).astype(BF16)
        q_dec = q_dec.astype(BF16)
        vb = v.astype(BF16)
        decay = jnp.exp(blast)
        out_gate = gate * jax.nn.sigmoid(gate)
        for h in range(heads):
            sk = slice(h * dk, (h + 1) * dk)
            sv = slice(h * dv, (h + 1) * dv)
            a = lax.dot_general(q_rel[:, sk], k_rel[:, sk], _NT, preferred_element_type=F32)
            a = jnp.where(causal, a, 0.0).astype(BF16)
            st = st_ref[bi, h]
            o = _dot(a, vb[:, sv]) + lax.dot_general(q_dec[:, sk], st.astype(BF16), _NT,
                                                      preferred_element_type=F32)
            st_ref[bi, h] = st * decay[:, sk] + lax.dot_general(vb[:, sv], k_end[:, sk], _TN,
                                                                preferred_element_type=F32)
            o = o * lax.rsqrt(jnp.mean(o * o, axis=-1, keepdims=True) + RMS_EPS) * ng
            og_ref[rows, sv] = (o * out_gate[:, sv]).astype(og_ref.dtype)
        return carry

    lax.fori_loop(0, m // c, chunk_body, 0)

    out = _dot(og_ref[...].astype(BF16), w_o_ref[...])
    o_ref[...] = _layer_norm(alpha * x + out, g_ref[...], b_ref[...]).reshape(nb, tt, d)

    @pl.when(t == pl.num_programs(1) - 1)
    def _():
        for bi in range(nb):
            for h in range(heads):
                sfin_ref[bi, h] = st_ref[bi, h].T


def _hgrn(x, s0, w_in, lb_param, norm_g, w_o, ln_g, ln_b, alpha, layer, chunk, nb, tt):
    b, t, d = x.shape
    heads = HG_HEADS
    dv = norm_g.shape[-1]
    dk = (w_in.shape[1] - 2 * heads * dv) // (2 * heads)
    m = nb * tt
    has_state = s0 is not None
    const2 = lambda i, j: (0, 0)
    state_spec = pl.BlockSpec((nb, heads, dk, dv), lambda i, j: (i, 0, 0, 0))
    in_specs = [pl.BlockSpec((nb, tt, d), lambda i, j: (i, j, 0))]
    args = [x]
    if has_state:
        in_specs.append(state_spec)
        args.append(s0)
    in_specs += [pl.BlockSpec(w_in.shape, const2), pl.BlockSpec(lb_param.shape, const2),
                 pl.BlockSpec(norm_g.shape, const2), pl.BlockSpec(w_o.shape, const2),
                 pl.BlockSpec((1, d), const2), pl.BlockSpec((1, d), const2)]
    args += [w_in, lb_param, norm_g, w_o, ln_g, ln_b]
    return pl.pallas_call(
        functools.partial(_hgrn_kernel, alpha=alpha, layer=layer, chunk=chunk, has_state=has_state),
        grid=(b // nb, t // tt),
        in_specs=in_specs,
        out_specs=[pl.BlockSpec((nb, tt, d), lambda i, j: (i, j, 0)), state_spec],
        out_shape=[jax.ShapeDtypeStruct(x.shape, F32),
                   jax.ShapeDtypeStruct((b, heads, dk, dv), F32)],
        scratch_shapes=[pltpu.VMEM((nb, heads, dv, dk), F32),
                        pltpu.VMEM((m, w_in.shape[1]), F32),
                        pltpu.VMEM((m, heads * dv), BF16 if chunk % 16 == 0 else F32)],
        compiler_params=_cparams("parallel", "arbitrary"),
        name="hgrn_state" if has_state else "hgrn_fresh",
    )(*args)


_I_E0, _I_E1, _I_G0, _I_G1, _I_R0, _I_R1 = range(6)


def _router_kernel(x_ref, w_ref, cnt0_ref, info_ref, cnt_ref, carry_ref):
    i = pl.program_id(0)
    tm = x_ref.shape[0]

    @pl.when(i == 0)
    def _():
        carry_ref[...] = cnt0_ref[...]

    logits = jnp.dot(x_ref[...], w_ref[...], preferred_element_type=F32, precision=lax.Precision.HIGHEST)
    lane = lax.broadcasted_iota(jnp.int32, (tm, LANES), 1)
    valid = lane < N_EXPERTS
    logits = jnp.where(valid, logits, -jnp.inf)
    p = jnp.exp(logits - jnp.max(logits, axis=-1, keepdims=True))
    p = p / jnp.sum(p, axis=-1, keepdims=True)
    p0 = jnp.where(valid, p, -1.0)
    v0 = jnp.max(p0, axis=-1, keepdims=True)
    e0 = jnp.min(jnp.where(p0 == v0, lane, LANES), axis=-1, keepdims=True)
    p1 = jnp.where(lane == e0, -1.0, p0)
    v1 = jnp.max(p1, axis=-1, keepdims=True)
    e1 = jnp.min(jnp.where(p1 == v1, lane, LANES), axis=-1, keepdims=True)
    den = v0 + v1
    sel = jnp.logical_or(lane == e0, lane == e1)
    ri = lax.broadcasted_iota(jnp.int32, (tm, tm), 0)
    ci = lax.broadcasted_iota(jnp.int32, (tm, tm), 1)
    before = (ri > ci).astype(BF16)
    rank = _dot(before, sel.astype(BF16)) + carry_ref[...]
    r0 = jnp.sum(jnp.where(lane == e0, rank, 0.0), axis=-1, keepdims=True)
    r1 = jnp.sum(jnp.where(lane == e1, rank, 0.0), axis=-1, keepdims=True)
    carry_ref[...] += jnp.sum(sel.astype(F32), axis=0, keepdims=True)
    info = jnp.zeros((tm, LANES), F32)
    for idx, val in ((_I_E0, e0.astype(F32)), (_I_E1, e1.astype(F32)), (_I_G0, v0 / den),
                     (_I_G1, v1 / den), (_I_R0, r0), (_I_R1, r1)):
        info = jnp.where(lane == idx, val, info)
    info_ref[...] = info
    cnt_ref[...] = carry_ref[...]


def _router(x2d, w_router_padded, counts_in, tm):
    n, d = x2d.shape
    return pl.pallas_call(
        _router_kernel,
        grid=(n // tm,),
        in_specs=[pl.BlockSpec((tm, d), lambda i: (i, 0)),
                  pl.BlockSpec((d, LANES), lambda i: (0, 0)),
                  pl.BlockSpec((1, LANES), lambda i: (0, 0))],
        out_specs=[pl.BlockSpec((tm, LANES), lambda i: (i, 0)),
                   pl.BlockSpec((1, LANES), lambda i: (0, 0))],
        out_shape=[jax.ShapeDtypeStruct((n, LANES), F32), jax.ShapeDtypeStruct((1, LANES), F32)],
        scratch_shapes=[pltpu.VMEM((1, LANES), F32)],
        compiler_params=_cparams("arbitrary"),
        name="moe_router",
    )(x2d, w_router_padded, counts_in)


def _row_copy(src, src_row, dst, dst_row, sem):
    return pltpu.make_async_copy(src.at[pl.ds(src_row, 1)], dst.at[pl.ds(dst_row, 1)], sem)


def _dispatch_kernel(pos_ref, x_hbm, buf_hbm, out_hbm, sem):
    del buf_hbm
    i = pl.program_id(0)
    tb = pos_ref.shape[0] // TOP_K
    base = i * tb

    def issue(j, carry):
        for kk in range(TOP_K):
            _row_copy(x_hbm, base + j, out_hbm, pos_ref[TOP_K * j + kk], sem).start()
        return carry

    lax.fori_loop(0, tb, issue, 0)
    for _ in range(TOP_K):
        pltpu.make_async_copy(x_hbm.at[pl.ds(0, tb)], out_hbm.at[pl.ds(0, tb)], sem).wait()


def _dispatch(x2d, pos_flat, buf, tb):
    n, d = x2d.shape
    return pl.pallas_call(
        _dispatch_kernel,
        grid=(n // tb,),
        in_specs=[pl.BlockSpec((TOP_K * tb,), lambda i: (i,), memory_space=pltpu.SMEM),
                  pl.BlockSpec(memory_space=pl.ANY),
                  pl.BlockSpec(memory_space=pl.ANY)],
        out_specs=pl.BlockSpec(memory_space=pl.ANY),
        out_shape=jax.ShapeDtypeStruct(buf.shape, buf.dtype),
        scratch_shapes=[pltpu.SemaphoreType.DMA(())],
        input_output_aliases={2: 0},
        compiler_params=_cparams("arbitrary"),
        name="moe_dispatch",
    )(pos_flat, x2d, buf)


def _combine_kernel(pos_ref, posn_ref, x_ref, info_ref, y_hbm, g_ref, b_ref, o_ref, ybuf, sem, *, alpha):
    i = pl.program_id(0)
    n = pl.num_programs(0)
    tm = x_ref.shape[0]

    def issue(p_ref, slot):
        def body(j, carry):
            for kk in range(TOP_K):
                _row_copy(y_hbm, p_ref[TOP_K * j + kk], ybuf.at[slot, kk], j, sem.at[slot]).start()
            return carry
        lax.fori_loop(0, tm, body, 0)

    @pl.when(i == 0)
    def _():
        issue(pos_ref, 0)

    @pl.when(i + 1 < n)
    def _():
        issue(posn_ref, (i + 1) % 2)

    slot = i % 2
    for kk in range(TOP_K):
        pltpu.make_async_copy(y_hbm.at[pl.ds(0, tm)], ybuf.at[slot, kk], sem.at[slot]).wait()
    info = info_ref[...]
    moe = info[:, _I_G0:_I_G0 + 1] * ybuf[slot, 0] + info[:, _I_G1:_I_G1 + 1] * ybuf[slot, 1]
    o_ref[...] = _layer_norm(alpha * x_ref[...] + moe, g_ref[...], b_ref[...])


def _combine_ln(x2d, info, pos_flat, y, ln_g, ln_b, alpha, tm):
    n, d = x2d.shape
    nt = n // tm
    vec = pl.BlockSpec((1, d), lambda i: (0, 0))
    return pl.pallas_call(
        functools.partial(_combine_kernel, alpha=alpha),
        grid=(nt,),
        in_specs=[pl.BlockSpec((TOP_K * tm,), lambda i: (i,), memory_space=pltpu.SMEM),
                  pl.BlockSpec((TOP_K * tm,), lambda i: (jnp.minimum(i + 1, nt - 1),), memory_space=pltpu.SMEM),
                  pl.BlockSpec((tm, d), lambda i: (i, 0)),
                  pl.BlockSpec((tm, LANES), lambda i: (i, 0)),
                  pl.BlockSpec(memory_space=pl.ANY),
                  vec, vec],
        out_specs=pl.BlockSpec((tm, d), lambda i: (i, 0)),
        out_shape=jax.ShapeDtypeStruct((n, d), F32),
        scratch_shapes=[pltpu.VMEM((2, TOP_K, tm, d), F32), pltpu.SemaphoreType.DMA((2,))],
        compiler_params=_cparams("arbitrary"),
        name="moe_combine",
    )(pos_flat, pos_flat, x2d, info, y, ln_g, ln_b)


def _moe_ln(xs2d, w_router, w_gu, w_d, ln_g, ln_b, alpha, tm_expert, tf):
    d = xs2d[0].shape[1]
    n_total = sum(x.shape[0] for x in xs2d)
    w_r = jnp.pad(w_router, ((0, 0), (0, LANES - N_EXPERTS)))
    counts = jnp.zeros((1, LANES), F32)
    infos = []
    for x in xs2d:
        info, counts = _router(x, w_r, counts, min(x.shape[0], 1024))
        infos.append(info)
    cnt = counts[0, :N_EXPERTS].astype(jnp.int32)
    padded = (cnt + tm_expert - 1) // tm_expert * tm_expert
    ends = jnp.cumsum(padded)
    offs = ends - padded
    n_rows = -(-(TOP_K * n_total) // tm_expert) * tm_expert + N_EXPERTS * tm_expert
    tile_start = jnp.arange(n_rows // tm_expert, dtype=jnp.int32) * tm_expert
    tile_eid = jnp.minimum(jnp.sum(tile_start[:, None] >= ends[None, :], axis=1), N_EXPERTS - 1).astype(jnp.int32)
    tile_used = (tile_start < offs[tile_eid] + cnt[tile_eid]).astype(jnp.int32)
    buf = jnp.zeros((n_rows, d), F32)
    poss = []
    for x, info in zip(xs2d, infos):
        eid = info[:, _I_E0:_I_E1 + 1].astype(jnp.int32)
        rank = info[:, _I_R0:_I_R1 + 1].astype(jnp.int32)
        pos = (offs[eid] + rank).reshape(-1)
        poss.append(pos)
        buf = _dispatch(x, pos, buf, min(x.shape[0], 1024))
    y = _expert_ffn(buf, tile_eid, tile_used, w_gu, w_d, tm_expert, tf)
    return [_combine_ln(x, info, pos, y, ln_g, ln_b, alpha, min(x.shape[0], 256))
            for x, info, pos in zip(xs2d, infos, poss)]


def kernel(x_prompt, x_sample, mem_prompt, state_pool, state_hgrn, cache_mem_k, cache_mem_v, pool_w, pool_scale, hg_w_in, hg_lb, hg_norm_g, hg_w_o, xa_w_q, xa_w_kv, xa_w_o, ffn_w_gu, ffn_w_d, moe_w_router, moe_w_gu, moe_w_d, ln_g, ln_b):
    b, t, d = x_prompt.shape
    sb, st, _ = x_sample.shape
    depth = ln_g.shape[0]
    mem_len = mem_prompt.shape[1]
    alpha = (2 * depth) ** 0.25
    n_prev = state_pool.shape[2]

    pool_w_b = pool_w.astype(BF16)
    hg_w_in_b = hg_w_in.astype(BF16)
    hg_w_o_b = hg_w_o.astype(BF16)
    xa_w_q_b = xa_w_q.astype(BF16)
    xa_w_kv_b = xa_w_kv.astype(BF16)
    xa_w_o_b = xa_w_o.astype(BF16)
    ffn_w_gu_b = ffn_w_gu.astype(BF16)
    ffn_w_d_b = ffn_w_d.astype(BF16)
    moe_w_gu_b = moe_w_gu.astype(BF16)
    moe_w_d_b = moe_w_d.astype(BF16)

    def vec(a):
        return a.reshape(1, -1)

    mk, mv = _mem_kv(mem_prompt.reshape(b * mem_len, d), xa_w_kv_b)
    mk4 = mk.reshape(depth, b, mem_len, d)
    mv4 = mv.reshape(depth, b, mem_len, d)
    ck4 = cache_mem_k.reshape(depth, sb, mem_len, d)
    cv4 = cache_mem_v.reshape(depth, sb, mem_len, d)

    tt_attn = min(t, 512)
    nb_attn = min(sb, 8)

    def attend(xp, xs, layer):
        args = (xa_w_q_b[layer], xa_w_o_b[layer], vec(ln_g[layer, 1]), vec(ln_b[layer, 1]), alpha)
        return (_mem_attend(xp, mk4, mv4, layer, *args, 1, tt_attn),
                _mem_attend(xs, ck4, cv4, layer, *args, nb_attn, st))

    pool_args = (pool_w_b[0], vec(pool_scale[0]), vec(ln_g[0, 0]), vec(ln_b[0, 0]), alpha)
    xp = _pool_prompt(x_prompt, *pool_args)
    prev = jnp.pad(state_pool[0], ((0, 0), (POOL_HALO - n_prev, 0), (0, 0)))
    xs = _pool_sample(x_sample, prev, n_prev, *pool_args)
    xp, xs = attend(xp, xs, 0)
    ffn_args = (ffn_w_gu_b, ffn_w_d_b, vec(ln_g[0, 2]), vec(ln_b[0, 2]), alpha)
    xp = _ffn_ln(xp.reshape(b * t, d), *ffn_args, 512, 512).reshape(b, t, d)
    xs = _ffn_ln(xs.reshape(sb * st, d), *ffn_args, 512, 512).reshape(sb, st, d)

    hg_args = (hg_w_in_b[0], hg_lb, vec(hg_norm_g[0]), hg_w_o_b[0], vec(ln_g[1, 0]), vec(ln_b[1, 0]), alpha, 1)
    xp, hg_p = _hgrn(xp, None, *hg_args, HG_CHUNK, 1, min(t, 256))
    xs, hg_s = _hgrn(xs, state_hgrn[0], *hg_args, math.gcd(st, HG_CHUNK), min(sb, 8), st)
    xp, xs = attend(xp, xs, 1)
    yp, ys = _moe_ln([xp.reshape(b * t, d), xs.reshape(sb * st, d)], moe_w_router[0], moe_w_gu_b[0],
                     moe_w_d_b[0], vec(ln_g[1, 2]), vec(ln_b[1, 2]), alpha, 512, 512)

    heads_hd = (depth, b, mem_len, XA_HEADS, d // XA_HEADS)
    pool_p = x_prompt[:, t - n_prev:, :][None]
    pool_s = jnp.concatenate([state_pool[0], x_sample], axis=1)[:, st:, :][None]
    return (yp.reshape(b, t, d), ys.reshape(sb, st, d), pool_p, hg_p[None],
            mk.reshape(heads_hd), mv.reshape(heads_hd), pool_s, hg_s[None])
```

```python
import functools
import math

import jax
import jax.numpy as jnp
from jax import lax
from jax.experimental import pallas as pl
from jax.experimental.pallas import tpu as pltpu

F32 = jnp.float32
BF16 = jnp.bfloat16

POOL_WINDOWS = (2, 4, 8, 16)
POOL_HALO = 16
HG_HEADS = 8
HG_CHUNK = 32
RMS_EPS = 1e-6
XA_HEADS = 4
N_EXPERTS = 8
TOP_K = 2
LN_EPS = 1e-5
LANES = 128
VMEM_LIMIT = 56 * 1024 * 1024

_NT = (((1,), (1,)), ((), ()))
_TN = (((0,), (0,)), ((), ()))


def _cparams(*sem):
    return pltpu.CompilerParams(dimension_semantics=sem, vmem_limit_bytes=VMEM_LIMIT)


def _layer_norm(z, g, b):
    mu = jnp.mean(z, axis=-1, keepdims=True)
    zc = z - mu
    var = jnp.mean(zc * zc, axis=-1, keepdims=True)
    return zc * lax.rsqrt(var + LN_EPS) * g + b


def _dot(a, b):
    return jnp.dot(a, b, preferred_element_type=F32)


def _kv_kernel(mem_ref, w_ref, k_ref, v_ref):
    kv = _dot(mem_ref[...].astype(BF16), w_ref[0])
    n = k_ref.shape[-1]
    k_ref[0] = kv[:, :n]
    v_ref[0] = kv[:, n:]


def _mem_kv(mem2d, w_kv):
    rows, d = mem2d.shape
    depth, _, e2 = w_kv.shape
    e = e2 // 2
    tm = min(rows, 512)
    out = jax.ShapeDtypeStruct((depth, rows, e), F32)
    return pl.pallas_call(
        _kv_kernel,
        grid=(depth, rows // tm),
        in_specs=[pl.BlockSpec((tm, d), lambda l, i: (i, 0)),
                  pl.BlockSpec((1, d, e2), lambda l, i: (l, 0, 0))],
        out_specs=[pl.BlockSpec((1, tm, e), lambda l, i: (l, i, 0)),
                   pl.BlockSpec((1, tm, e), lambda l, i: (l, i, 0))],
        out_shape=[out, out],
        compiler_params=_cparams("parallel", "parallel"),
        name="mem_kv",
    )(mem2d, w_kv)


def _window_sums(a, w):
    s, k = a, 1
    while k < w:
        s = s + pltpu.roll(s, k, 0)
        k *= 2
    return s


def _pool_mix(pooled_minus_tok, w_ref, scale):
    parts = [_dot(pooled_minus_tok[g].astype(BF16), w_ref[g]) for g in range(len(POOL_WINDOWS))]
    return jnp.concatenate(parts, axis=1) * scale


def _pool_prompt_kernel(x_ref, halo_ref, w_ref, scale_ref, g_ref, b_ref, o_ref, buf_ref, *, alpha):
    t = pl.program_id(1)
    tt = x_ref.shape[1]
    gw = w_ref.shape[-1]
    x = x_ref[0]
    buf_ref[0:POOL_HALO, :] = jnp.where(t > 0, halo_ref[0], 0.0)
    buf_ref[POOL_HALO:, :] = x
    pos = t * tt + lax.broadcasted_iota(jnp.int32, (tt, 1), 0)
    diffs = []
    for g, w in enumerate(POOL_WINDOWS):
        sl = slice(g * gw, (g + 1) * gw)
        s = _window_sums(buf_ref[:, sl], w)[POOL_HALO:, :]
        cnt = jnp.minimum(pos + 1, w).astype(F32)
        diffs.append(s / cnt - x[:, sl])
    mixed = _pool_mix(diffs, w_ref, scale_ref[...])
    o_ref[0] = _layer_norm(alpha * x + mixed, g_ref[...], b_ref[...])


def _pool_prompt(x, pool_w, scale, ln_g, ln_b, alpha):
    b, t, d = x.shape
    tt = min(t, 512)
    hb = tt // POOL_HALO
    vec = pl.BlockSpec((1, d), lambda i, j: (0, 0))
    return pl.pallas_call(
        functools.partial(_pool_prompt_kernel, alpha=alpha),
        grid=(b, t // tt),
        in_specs=[pl.BlockSpec((1, tt, d), lambda i, j: (i, j, 0)),
                  pl.BlockSpec((1, POOL_HALO, d), lambda i, j: (i, jnp.maximum(j * hb - 1, 0), 0)),
                  pl.BlockSpec(pool_w.shape, lambda i, j: (0, 0, 0)),
                  vec, vec, vec],
        out_specs=pl.BlockSpec((1, tt, d), lambda i, j: (i, j, 0)),
        out_shape=jax.ShapeDtypeStruct(x.shape, F32),
        scratch_shapes=[pltpu.VMEM((tt + POOL_HALO, d), F32)],
        compiler_params=_cparams("parallel", "parallel"),
        name="pool_prompt",
    )(x, x, pool_w, scale, ln_g, ln_b)


def _pool_sample_kernel(x_ref, prev_ref, w_ref, scale_ref, g_ref, b_ref, o_ref, buf_ref, *, alpha, n_prev):
    nb, t, d = x_ref.shape
    gw = w_ref.shape[-1]
    rows = POOL_HALO + t
    buf_ref[:, 0:POOL_HALO, :] = prev_ref[...]
    buf_ref[:, POOL_HALO:, :] = x_ref[...]
    x = x_ref[...].reshape(nb * t, d)
    pos = lax.broadcasted_iota(jnp.int32, (nb, t, 1), 1).reshape(nb * t, 1)
    diffs = []
    for g, w in enumerate(POOL_WINDOWS):
        sl = slice(g * gw, (g + 1) * gw)
        s = _window_sums(buf_ref[:, :, sl].reshape(nb * rows, gw), w)
        s = s.reshape(nb, rows, gw)[:, POOL_HALO:, :].reshape(nb * t, gw)
        cnt = jnp.minimum(pos + 1 + n_prev, w).astype(F32)
        diffs.append(s / cnt - x[:, sl])
    mixed = _pool_mix(diffs, w_ref, scale_ref[...])
    o_ref[...] = _layer_norm(alpha * x + mixed, g_ref[...], b_ref[...]).reshape(nb, t, d)


def _pool_sample(x, prev_padded, n_prev, pool_w, scale, ln_g, ln_b, alpha):
    b, t, d = x.shape
    nb = min(b, 16)
    vec = pl.BlockSpec((1, d), lambda i: (0, 0))
    return pl.pallas_call(
        functools.partial(_pool_sample_kernel, alpha=alpha, n_prev=n_prev),
        grid=(b // nb,),
        in_specs=[pl.BlockSpec((nb, t, d), lambda i: (i, 0, 0)),
                  pl.BlockSpec((nb, POOL_HALO, d), lambda i: (i, 0, 0)),
                  pl.BlockSpec(pool_w.shape, lambda i: (0, 0, 0)),
                  vec, vec, vec],
        out_specs=pl.BlockSpec((nb, t, d), lambda i: (i, 0, 0)),
        out_shape=jax.ShapeDtypeStruct(x.shape, F32),
        scratch_shapes=[pltpu.VMEM((nb, POOL_HALO + t, d), F32)],
        compiler_params=_cparams("parallel"),
        name="pool_sample",
    )(x, prev_padded, pool_w, scale, ln_g, ln_b)


def _attn_kernel(x_ref, k_ref, v_ref, wq_ref, wo_ref, g_ref, b_ref, o_ref, oh_ref, *, alpha):
    nb, tt, d = x_ref.shape
    hd = d // XA_HEADS
    scale = hd ** -0.5
    x = x_ref[...].reshape(nb * tt, d)
    q = _dot(x.astype(BF16), wq_ref[...])
    for h in range(XA_HEADS):
        sl = slice(h * hd, (h + 1) * hd)
        qh = q[:, sl].astype(BF16).reshape(nb, tt, hd)
        kh = k_ref[:, :, sl].astype(BF16)
        vh = v_ref[:, :, sl].astype(BF16)
        s = jnp.einsum("btd,bmd->btm", qh, kh, preferred_element_type=F32) * scale
        p = jnp.exp(s - jnp.max(s, axis=-1, keepdims=True))
        p = (p / jnp.sum(p, axis=-1, keepdims=True)).astype(BF16)
        oh = jnp.einsum("btm,bmd->btd", p, vh, preferred_element_type=F32)
        oh_ref[:, sl] = oh.reshape(nb * tt, hd).astype(BF16)
    a = _dot(oh_ref[...], wo_ref[...])
    o_ref[...] = _layer_norm(alpha * x + a, g_ref[...], b_ref[...]).reshape(nb, tt, d)


def _mem_attend(x, k4, v4, layer, wq, wo, ln_g, ln_b, alpha, nb, tt):
    b, t, d = x.shape
    m = k4.shape[2]
    vec = pl.BlockSpec((1, d), lambda i, j: (0, 0))
    mat = pl.BlockSpec((d, d), lambda i, j: (0, 0))
    kv = pl.BlockSpec((None, nb, m, d), lambda i, j: (layer, i, 0, 0))
    return pl.pallas_call(
        functools.partial(_attn_kernel, alpha=alpha),
        grid=(b // nb, t // tt),
        in_specs=[pl.BlockSpec((nb, tt, d), lambda i, j: (i, j, 0)), kv, kv, mat, mat, vec, vec],
        out_specs=pl.BlockSpec((nb, tt, d), lambda i, j: (i, j, 0)),
        out_shape=jax.ShapeDtypeStruct(x.shape, F32),
        scratch_shapes=[pltpu.VMEM((nb * tt, d), BF16)],
        compiler_params=_cparams("parallel", "parallel"),
        name=f"mem_attend_l{layer}",
    )(x, k4, v4, wq, wo, ln_g, ln_b)


def _attn_cache_kernel(x_ref, k_ref, v_ref, wq_ref, wo_ref, g_ref, b_ref, o_ref, q_scr, oh_scr, *, alpha):
    nb, t, d = x_ref.shape
    heads = XA_HEADS
    hd = d // heads
    halves = hd // LANES
    assert halves == 2
    rows = k_ref.shape[1]
    qr = heads * t
    scale = hd ** -0.5
    x = x_ref[...].reshape(nb * t, d)
    q = _dot(x.astype(BF16), wq_ref[...])
    for c in range(halves):
        for h in range(heads):
            col = (h * halves + c) * LANES
            q_scr[c, :, h * t:(h + 1) * t, :] = q[:, col:col + LANES].reshape(nb, t, LANES)

    col_id = lax.broadcasted_iota(jnp.int32, (qr, rows), 1)
    row_head = lax.broadcasted_iota(jnp.int32, (qr, rows), 0) // t
    col_half = (col_id // heads) % halves
    col_head = col_id % heads
    keep = jnp.logical_and(col_half == 1, col_head == row_head)

    def per_batch(b, carry):
        kb = k_ref[b].astype(BF16)
        vb = v_ref[b].astype(BF16)
        g0 = lax.dot_general(q_scr[0, b].astype(BF16), kb, _NT, preferred_element_type=F32)
        g1 = lax.dot_general(q_scr[1, b].astype(BF16), kb, _NT, preferred_element_type=F32)
        part = jnp.where(col_half == 0, g0, g1)
        s = (part + pltpu.roll(part, heads, 1)) * scale
        s = jnp.where(keep, s, -jnp.inf)
        p = jnp.exp(s - jnp.max(s, axis=-1, keepdims=True))
        p = p / jnp.sum(p, axis=-1, keepdims=True)
        p_half = (pltpu.roll(p, rows - heads, 1).astype(BF16), p.astype(BF16))
        for c in range(halves):
            oc = _dot(p_half[c], vb)
            for h in range(heads):
                col = (h * halves + c) * LANES
                oh_scr[pl.ds(pl.multiple_of(b * t, t), t), col:col + LANES] = oc[h * t:(h + 1) * t, :]
        return carry

    lax.fori_loop(0, nb, per_batch, 0)
    a = _dot(oh_scr[...].astype(BF16), wo_ref[...])
    o_ref[...] = _layer_norm(alpha * x + a, g_ref[...], b_ref[...]).reshape(nb, t, d)


def _mem_attend_cache(x, k8, v8, layer, wq, wo, ln_g, ln_b, alpha, nb):
    b, t, d = x.shape
    rows = k8.shape[2]
    vec = pl.BlockSpec((1, d), lambda i: (0, 0))
    mat = pl.BlockSpec((d, d), lambda i: (0, 0))
    kv = pl.BlockSpec((None, nb, rows, LANES), lambda i: (layer, i, 0, 0))
    return pl.pallas_call(
        functools.partial(_attn_cache_kernel, alpha=alpha),
        grid=(b // nb,),
        in_specs=[pl.BlockSpec((nb, t, d), lambda i: (i, 0, 0)), kv, kv, mat, mat, vec, vec],
        out_specs=pl.BlockSpec((nb, t, d), lambda i: (i, 0, 0)),
        out_shape=jax.ShapeDtypeStruct(x.shape, F32),
        scratch_shapes=[pltpu.VMEM((2, nb, XA_HEADS * t, LANES), F32), pltpu.VMEM((nb * t, d), F32)],
        compiler_params=_cparams("parallel"),
        name=f"mem_attend_cache_l{layer}",
    )(x, k8, v8, wq, wo, ln_g, ln_b)


def _swiglu_step(x_ref, wg_ref, wu_ref, wd_ref, xb_ref, acc_ref):
    f = pl.program_id(1)

    @pl.when(f == 0)
    def _():
        xb_ref[...] = x_ref[...].astype(BF16)
        acc_ref[...] = jnp.zeros_like(acc_ref)

    xb = xb_ref[...]
    gate = _dot(xb, wg_ref[0])
    up = _dot(xb, wu_ref[0])
    h = (gate * jax.nn.sigmoid(gate) * up).astype(BF16)
    acc_ref[...] += _dot(h, wd_ref[0])


def _ffn_ln_kernel(x_ref, wg_ref, wu_ref, wd_ref, g_ref, b_ref, o_ref, xb_ref, acc_ref, *, alpha):
    _swiglu_step(x_ref, wg_ref, wu_ref, wd_ref, xb_ref, acc_ref)

    @pl.when(pl.program_id(1) == pl.num_programs(1) - 1)
    def _():
        o_ref[...] = _layer_norm(alpha * x_ref[...] + acc_ref[...], g_ref[...], b_ref[...])


def _ffn_ln(x2d, w_gu, w_d, ln_g, ln_b, alpha, tm, tf):
    n, d = x2d.shape
    tm = min(tm, n)
    f = w_d.shape[1]
    nf = f // tf
    vec = pl.BlockSpec((1, d), lambda i, j: (0, 0))
    return pl.pallas_call(
        functools.partial(_ffn_ln_kernel, alpha=alpha),
        grid=(n // tm, nf),
        in_specs=[pl.BlockSpec((tm, d), lambda i, j: (i, 0)),
                  pl.BlockSpec((1, d, tf), lambda i, j: (0, 0, j)),
                  pl.BlockSpec((1, d, tf), lambda i, j: (0, 0, nf + j)),
                  pl.BlockSpec((1, tf, d), lambda i, j: (0, j, 0)),
                  vec, vec],
        out_specs=pl.BlockSpec((tm, d), lambda i, j: (i, 0)),
        out_shape=jax.ShapeDtypeStruct((n, d), F32),
        scratch_shapes=[pltpu.VMEM((tm, d), BF16), pltpu.VMEM((tm, d), F32)],
        compiler_params=_cparams("parallel", "arbitrary"),
        name="ffn_dense",
    )(x2d, w_gu, w_gu, w_d, ln_g, ln_b)


def _expert_kernel(eid_ref, used_ref, x_ref, wg_ref, wu_ref, wd_ref, o_ref, xb_ref, acc_ref):
    i = pl.program_id(0)
    last = pl.program_id(1) == pl.num_programs(1) - 1

    @pl.when(used_ref[i] > 0)
    def _():
        _swiglu_step(x_ref, wg_ref, wu_ref, wd_ref, xb_ref, acc_ref)

        @pl.when(last)
        def _():
            o_ref[...] = acc_ref[...]

    @pl.when(jnp.logical_and(used_ref[i] == 0, last))
    def _():
        o_ref[...] = jnp.zeros_like(o_ref)


def _expert_ffn(xs, tile_eid, tile_used, w_gu, w_d, tm, tf):
    r, d = xs.shape
    f = w_d.shape[1]
    nf = f // tf
    grid_spec = pltpu.PrefetchScalarGridSpec(
        num_scalar_prefetch=2,
        grid=(r // tm, nf),
        in_specs=[pl.BlockSpec((tm, d), lambda i, j, e, u: (i, 0)),
                  pl.BlockSpec((1, d, tf), lambda i, j, e, u: (e[i], 0, j)),
                  pl.BlockSpec((1, d, tf), lambda i, j, e, u: (e[i], 0, nf + j)),
                  pl.BlockSpec((1, tf, d), lambda i, j, e, u: (e[i], j, 0))],
        out_specs=pl.BlockSpec((tm, d), lambda i, j, e, u: (i, 0)),
        scratch_shapes=[pltpu.VMEM((tm, d), BF16), pltpu.VMEM((tm, d), F32)],
    )
    return pl.pallas_call(
        _expert_kernel,
        grid_spec=grid_spec,
        out_shape=jax.ShapeDtypeStruct((r, d), F32),
        compiler_params=_cparams("parallel", "arbitrary"),
        name="ffn_experts",
    )(tile_eid, tile_used, xs, w_gu, w_gu, w_d)


def _hgrn_kernel(*refs, alpha, layer, chunk, has_state):
    if has_state:
        (x_ref, s0_ref, w_in_ref, lbp_ref, ng_ref, w_o_ref, g_ref, b_ref,
         o_ref, sfin_ref, st_ref, proj_ref, og_ref) = refs
    else:
        (x_ref, w_in_ref, lbp_ref, ng_ref, w_o_ref, g_ref, b_ref,
         o_ref, sfin_ref, st_ref, proj_ref, og_ref) = refs
        s0_ref = None
    t = pl.program_id(1)
    nb, tt, d = x_ref.shape
    heads = HG_HEADS
    dv = ng_ref.shape[-1]
    vdim = heads * dv
    kdim = (w_in_ref.shape[1] - 2 * vdim) // 2
    dk = kdim // heads
    m = nb * tt
    c = chunk
    chunks_per_seq = tt // c

    @pl.when(t == 0)
    def _():
        if has_state:
            for bi in range(nb):
                for h in range(heads):
                    st_ref[bi, h] = s0_ref[bi, h].T
        else:
            st_ref[...] = jnp.zeros_like(st_ref)

    x = x_ref[...].reshape(m, d)
    proj_ref[...] = _dot(x.astype(BF16), w_in_ref[...])

    lbp = lbp_ref[...]
    e = jnp.exp(lbp - jnp.max(lbp, axis=0, keepdims=True))
    sm = e / jnp.sum(e, axis=0, keepdims=True)
    lb = jnp.sum(sm[1:layer + 1], axis=0, keepdims=True)

    ri = lax.broadcasted_iota(jnp.int32, (c, c), 0)
    ci = lax.broadcasted_iota(jnp.int32, (c, c), 1)
    causal = ri >= ci
    tri = causal.astype(F32)
    mid = max(c // 2 - 1, 0)
    ng = ng_ref[...]

    def chunk_body(j, carry):
        r0 = pl.multiple_of(j * c, c)
        bi = j // chunks_per_seq
        rows = pl.ds(r0, c)
        q = proj_ref[rows, 0:kdim]
        f_pre = proj_ref[rows, kdim:2 * kdim]
        v = proj_ref[rows, 2 * kdim:2 * kdim + vdim]
        gate = proj_ref[rows, 2 * kdim + vdim:]
        f = lb + (1.0 - lb) * jax.nn.sigmoid(f_pre)
        k = 1.0 - f
        bcum = jnp.dot(tri, jnp.log(f), preferred_element_type=F32, precision=lax.Precision.HIGHEST)
        bmid = bcum[mid:mid + 1]
        blast = bcum[c - 1:c]
        q_dec = q * jnp.exp(bcum)
        q_rel = (q_dec * jnp.exp(-bmid)).astype(BF16)
        k_rel32 = k * jnp.exp(bmid - bcum)
        k_rel = k_rel32.astype(BF16)
        k_end = (k_rel32 * jnp.exp(blast - bmid)).astype(BF16)
        q_dec = q_dec.astype(BF16)
        vb = v.astype(BF16)
        decay = jnp.exp(blast)
        out_gate = gate * jax.nn.sigmoid(gate)
        for h in range(heads):
            sk = slice(h * dk, (h + 1) * dk)
            sv = slice(h * dv, (h + 1) * dv)
            a = lax.dot_general(q_rel[:, sk], k_rel[:, sk], _NT, preferred_element_type=F32)
            a = jnp.where(causal, a, 0.0).astype(BF16)
            st = st_ref[bi, h]
            o = _dot(a, vb[:, sv]) + lax.dot_general(q_dec[:, sk], st.astype(BF16), _NT,
                                                      preferred_element_type=F32)
            st_ref[bi, h] = st * decay[:, sk] + lax.dot_general(vb[:, sv], k_end[:, sk], _TN,
                                                                preferred_element_type=F32)
            o = o * lax.rsqrt(jnp.mean(o * o, axis=-1, keepdims=True) + RMS_EPS) * ng
            og_ref[rows, sv] = (o * out_gate[:, sv]).astype(og_ref.dtype)
        return carry

    lax.fori_loop(0, m // c, chunk_body, 0)

    out = _dot(og_ref[...].astype(BF16), w_o_ref[...])
    o_ref[...] = _layer_norm(alpha * x + out, g_ref[...], b_ref[...]).reshape(nb, tt, d)

    @pl.when(t == pl.num_programs(1) - 1)
    def _():
        for bi in range(nb):
            for h in range(heads):
                sfin_ref[bi, h] = st_ref[bi, h].T


def _hgrn(x, s0, w_in, lb_param, norm_g, w_o, ln_g, ln_b, alpha, layer, chunk, nb, tt):
    b, t, d = x.shape
    heads = HG_HEADS
    dv = norm_g.shape[-1]
    dk = (w_in.shape[1] - 2 * heads * dv) // (2 * heads)
    m = nb * tt
    has_state = s0 is not None
    const2 = lambda i, j: (0, 0)
    state_spec = pl.BlockSpec((nb, heads, dk, dv), lambda i, j: (i, 0, 0, 0))
    in_specs = [pl.BlockSpec((nb, tt, d), lambda i, j: (i, j, 0))]
    args = [x]
    if has_state:
        in_specs.append(state_spec)
        args.append(s0)
    in_specs += [pl.BlockSpec(w_in.shape, const2), pl.BlockSpec(lb_param.shape, const2),
                 pl.BlockSpec(norm_g.shape, const2), pl.BlockSpec(w_o.shape, const2),
                 pl.BlockSpec((1, d), const2), pl.BlockSpec((1, d), const2)]
    args += [w_in, lb_param, norm_g, w_o, ln_g, ln_b]
    return pl.pallas_call(
        functools.partial(_hgrn_kernel, alpha=alpha, layer=layer, chunk=chunk, has_state=has_state),
        grid=(b // nb, t // tt),
        in_specs=in_specs,
        out_specs=[pl.BlockSpec((nb, tt, d), lambda i, j: (i, j, 0)), state_spec],
        out_shape=[jax.ShapeDtypeStruct(x.shape, F32),
                   jax.ShapeDtypeStruct((b, heads, dk, dv), F32)],
        scratch_shapes=[pltpu.VMEM((nb, heads, dv, dk), F32),
                        pltpu.VMEM((m, w_in.shape[1]), F32),
                        pltpu.VMEM((m, heads * dv), BF16 if chunk % 16 == 0 else F32)],
        compiler_params=_cparams("parallel", "arbitrary"),
        name="hgrn_state" if has_state else "hgrn_fresh",
    )(*args)


_I_E0, _I_E1, _I_G0, _I_G1, _I_R0, _I_R1 = range(6)


def _router_kernel(x_ref, w_ref, cnt0_ref, info_ref, cnt_ref, carry_ref):
    i = pl.program_id(0)
    tm = x_ref.shape[0]

    @pl.when(i == 0)
    def _():
        carry_ref[...] = cnt0_ref[...]

    logits = jnp.dot(x_ref[...], w_ref[...], preferred_element_type=F32, precision=lax.Precision.HIGHEST)
    lane = lax.broadcasted_iota(jnp.int32, (tm, LANES), 1)
    valid = lane < N_EXPERTS
    logits = jnp.where(valid, logits, -jnp.inf)
    p = jnp.exp(logits - jnp.max(logits, axis=-1, keepdims=True))
    p = p / jnp.sum(p, axis=-1, keepdims=True)
    p0 = jnp.where(valid, p, -1.0)
    v0 = jnp.max(p0, axis=-1, keepdims=True)
    e0 = jnp.min(jnp.where(p0 == v0, lane, LANES), axis=-1, keepdims=True)
    p1 = jnp.where(lane == e0, -1.0, p0)
    v1 = jnp.max(p1, axis=-1, keepdims=True)
    e1 = jnp.min(jnp.where(p1 == v1, lane, LANES), axis=-1, keepdims=True)
    den = v0 + v1
    sel = jnp.logical_or(lane == e0, lane == e1)
    ri = lax.broadcasted_iota(jnp.int32, (tm, tm), 0)
    ci = lax.broadcasted_iota(jnp.int32, (tm, tm), 1)
    before = (ri > ci).astype(BF16)
    rank = _dot(before, sel.astype(BF16)) + carry_ref[...]
    r0 = jnp.sum(jnp.where(lane == e0, rank, 0.0), axis=-1, keepdims=True)
    r1 = jnp.sum(jnp.where(lane == e1, rank, 0.0), axis=-1, keepdims=True)
    carry_ref[...] += jnp.sum(sel.astype(F32), axis=0, keepdims=True)
    info = jnp.zeros((tm, LANES), F32)
    for idx, val in ((_I_E0, e0.astype(F32)), (_I_E1, e1.astype(F32)), (_I_G0, v0 / den),
                     (_I_G1, v1 / den), (_I_R0, r0), (_I_R1, r1)):
        info = jnp.where(lane == idx, val, info)
    info_ref[...] = info
    cnt_ref[...] = carry_ref[...]


def _router(x2d, w_router_padded, counts_in, tm):
    n, d = x2d.shape
    return pl.pallas_call(
        _router_kernel,
        grid=(n // tm,),
        in_specs=[pl.BlockSpec((tm, d), lambda i: (i, 0)),
                  pl.BlockSpec((d, LANES), lambda i: (0, 0)),
                  pl.BlockSpec((1, LANES), lambda i: (0, 0))],
        out_specs=[pl.BlockSpec((tm, LANES), lambda i: (i, 0)),
                   pl.BlockSpec((1, LANES), lambda i: (0, 0))],
        out_shape=[jax.ShapeDtypeStruct((n, LANES), F32), jax.ShapeDtypeStruct((1, LANES), F32)],
        scratch_shapes=[pltpu.VMEM((1, LANES), F32)],
        compiler_params=_cparams("arbitrary"),
        name="moe_router",
    )(x2d, w_router_padded, counts_in)


def _row_copy(src, src_row, dst, dst_row, sem):
    return pltpu.make_async_copy(src.at[pl.ds(src_row, 1)], dst.at[pl.ds(dst_row, 1)], sem)


def _dispatch_kernel(pos_ref, x_ref, buf_hbm, out_hbm, sem):
    del buf_hbm
    tb = x_ref.shape[0]

    def issue(j, carry):
        for kk in range(TOP_K):
            _row_copy(x_ref, j, out_hbm, pos_ref[TOP_K * j + kk], sem).start()
        return carry

    lax.fori_loop(0, tb, issue, 0, unroll=4)
    for _ in range(TOP_K):
        pltpu.make_async_copy(x_ref, out_hbm.at[pl.ds(0, tb)], sem).wait()


def _dispatch(x2d, pos_flat, buf, tb):
    n, d = x2d.shape
    return pl.pallas_call(
        _dispatch_kernel,
        grid=(n // tb,),
        in_specs=[pl.BlockSpec((TOP_K * tb,), lambda i: (i,), memory_space=pltpu.SMEM),
                  pl.BlockSpec((tb, d), lambda i: (i, 0)),
                  pl.BlockSpec(memory_space=pl.ANY)],
        out_specs=pl.BlockSpec(memory_space=pl.ANY),
        out_shape=jax.ShapeDtypeStruct(buf.shape, buf.dtype),
        scratch_shapes=[pltpu.SemaphoreType.DMA(())],
        input_output_aliases={2: 0},
        compiler_params=_cparams("arbitrary"),
        name="moe_dispatch",
    )(pos_flat, x2d, buf)


def _combine_kernel(pos_ref, posn_ref, x_ref, info_ref, y_hbm, g_ref, b_ref, o_ref, ybuf, sem, *, alpha):
    i = pl.program_id(0)
    n = pl.num_programs(0)
    tm = x_ref.shape[0]

    def issue(p_ref, slot):
        def body(j, carry):
            for kk in range(TOP_K):
                _row_copy(y_hbm, p_ref[TOP_K * j + kk], ybuf.at[slot, kk], j, sem.at[slot]).start()
            return carry
        lax.fori_loop(0, tm, body, 0)

    @pl.when(i == 0)
    def _():
        issue(pos_ref, 0)

    @pl.when(i + 1 < n)
    def _():
        issue(posn_ref, (i + 1) % 2)

    slot = i % 2
    for kk in range(TOP_K):
        pltpu.make_async_copy(y_hbm.at[pl.ds(0, tm)], ybuf.at[slot, kk], sem.at[slot]).wait()
    info = info_ref[...]
    moe = info[:, _I_G0:_I_G0 + 1] * ybuf[slot, 0] + info[:, _I_G1:_I_G1 + 1] * ybuf[slot, 1]
    o_ref[...] = _layer_norm(alpha * x_ref[...] + moe, g_ref[...], b_ref[...])


def _combine_ln(x2d, info, pos_flat, y, ln_g, ln_b, alpha, tm):
    n, d = x2d.shape
    nt = n // tm
    vec = pl.BlockSpec((1, d), lambda i: (0, 0))
    return pl.pallas_call(
        functools.partial(_combine_kernel, alpha=alpha),
        grid=(nt,),
        in_specs=[pl.BlockSpec((TOP_K * tm,), lambda i: (i,), memory_space=pltpu.SMEM),
                  pl.BlockSpec((TOP_K * tm,), lambda i: (jnp.minimum(i + 1, nt - 1),), memory_space=pltpu.SMEM),
                  pl.BlockSpec((tm, d), lambda i: (i, 0)),
                  pl.BlockSpec((tm, LANES), lambda i: (i, 0)),
                  pl.BlockSpec(memory_space=pl.ANY),
                  vec, vec],
        out_specs=pl.BlockSpec((tm, d), lambda i: (i, 0)),
        out_shape=jax.ShapeDtypeStruct((n, d), F32),
        scratch_shapes=[pltpu.VMEM((2, TOP_K, tm, d), F32), pltpu.SemaphoreType.DMA((2,))],
        compiler_params=_cparams("arbitrary"),
        name="moe_combine",
    )(pos_flat, pos_flat, x2d, info, y, ln_g, ln_b)


def _moe_ln(xs2d, w_router, w_gu, w_d, ln_g, ln_b, alpha, tm_expert, tf):
    d = xs2d[0].shape[1]
    n_total = sum(x.shape[0] for x in xs2d)
    w_r = jnp.pad(w_router, ((0, 0), (0, LANES - N_EXPERTS)))
    counts = jnp.zeros((1, LANES), F32)
    infos = []
    for x in xs2d:
        info, counts = _router(x, w_r, counts, min(x.shape[0], 1024))
        infos.append(info)
    cnt = counts[0, :N_EXPERTS].astype(jnp.int32)
    padded = (cnt + tm_expert - 1) // tm_expert * tm_expert
    ends = jnp.cumsum(padded)
    offs = ends - padded
    n_rows = -(-(TOP_K * n_total) // tm_expert) * tm_expert + N_EXPERTS * tm_expert
    tile_start = jnp.arange(n_rows // tm_expert, dtype=jnp.int32) * tm_expert
    tile_eid = jnp.minimum(jnp.sum(tile_start[:, None] >= ends[None, :], axis=1), N_EXPERTS - 1).astype(jnp.int32)
    tile_used = (tile_start < offs[tile_eid] + cnt[tile_eid]).astype(jnp.int32)
    buf = jnp.zeros((n_rows, d), F32)
    poss = []
    for x, info in zip(xs2d, infos):
        eid = info[:, _I_E0:_I_E1 + 1].astype(jnp.int32)
        rank = info[:, _I_R0:_I_R1 + 1].astype(jnp.int32)
        pos = (offs[eid] + rank).reshape(-1)
        poss.append(pos)
        buf = _dispatch(x, pos, buf, min(x.shape[0], 1024))
    y = _expert_ffn(buf, tile_eid, tile_used, w_gu, w_d, tm_expert, tf)
    return [_combine_ln(x, info, pos, y, ln_g, ln_b, alpha, min(x.shape[0], 256))
            for x, info, pos in zip(xs2d, infos, poss)]


def kernel(x_prompt, x_sample, mem_prompt, state_pool, state_hgrn, cache_mem_k, cache_mem_v, pool_w, pool_scale, hg_w_in, hg_lb, hg_norm_g, hg_w_o, xa_w_q, xa_w_kv, xa_w_o, ffn_w_gu, ffn_w_d, moe_w_router, moe_w_gu, moe_w_d, ln_g, ln_b):
    b, t, d = x_prompt.shape
    sb, st, _ = x_sample.shape
    depth = ln_g.shape[0]
    mem_len = mem_prompt.shape[1]
    alpha = (2 * depth) ** 0.25
    n_prev = state_pool.shape[2]

    pool_w_b = pool_w.astype(BF16)
    hg_w_in_b = hg_w_in.astype(BF16)
    hg_w_o_b = hg_w_o.astype(BF16)
    xa_w_q_b = xa_w_q.astype(BF16)
    xa_w_kv_b = xa_w_kv.astype(BF16)
    xa_w_o_b = xa_w_o.astype(BF16)
    ffn_w_gu_b = ffn_w_gu.astype(BF16)
    ffn_w_d_b = ffn_w_d.astype(BF16)
    moe_w_gu_b = moe_w_gu.astype(BF16)
    moe_w_d_b = moe_w_d.astype(BF16)

    def vec(a):
        return a.reshape(1, -1)

    mk, mv = _mem_kv(mem_prompt.reshape(b * mem_len, d), xa_w_kv_b)
    mk4 = mk.reshape(depth, b, mem_len, d)
    mv4 = mv.reshape(depth, b, mem_len, d)

    def stored_order(c):
        halves = c.shape[4] // LANES
        c = c.reshape(depth, sb, mem_len, XA_HEADS, halves, LANES).transpose(0, 1, 2, 4, 3, 5)
        return c.reshape(depth, sb, mem_len * halves * XA_HEADS, LANES)

    ck8 = stored_order(cache_mem_k)
    cv8 = stored_order(cache_mem_v)

    tt_attn = min(t, 512)
    nb_attn = min(sb, 8)

    def attend(xp, xs, layer):
        args = (xa_w_q_b[layer], xa_w_o_b[layer], vec(ln_g[layer, 1]), vec(ln_b[layer, 1]), alpha)
        return (_mem_attend(xp, mk4, mv4, layer, *args, 1, tt_attn),
                _mem_attend_cache(xs, ck8, cv8, layer, *args, nb_attn))

    pool_args = (pool_w_b[0], vec(pool_scale[0]), vec(ln_g[0, 0]), vec(ln_b[0, 0]), alpha)
    xp = _pool_prompt(x_prompt, *pool_args)
    prev = jnp.pad(state_pool[0], ((0, 0), (POOL_HALO - n_prev, 0), (0, 0)))
    xs = _pool_sample(x_sample, prev, n_prev, *pool_args)
    xp, xs = attend(xp, xs, 0)
    ffn_args = (ffn_w_gu_b, ffn_w_d_b, vec(ln_g[0, 2]), vec(ln_b[0, 2]), alpha)
    xp = _ffn_ln(xp.reshape(b * t, d), *ffn_args, 512, 512).reshape(b, t, d)
    xs = _ffn_ln(xs.reshape(sb * st, d), *ffn_args, 512, 512).reshape(sb, st, d)

    hg_args = (hg_w_in_b[0], hg_lb, vec(hg_norm_g[0]), hg_w_o_b[0], vec(ln_g[1, 0]), vec(ln_b[1, 0]), alpha, 1)
    xp, hg_p = _hgrn(xp, None, *hg_args, HG_CHUNK, 1, min(t, 256))
    xs, hg_s = _hgrn(xs, state_hgrn[0], *hg_args, math.gcd(st, HG_CHUNK), min(sb, 8), st)
    xp, xs = attend(xp, xs, 1)
    yp, ys = _moe_ln([xp.reshape(b * t, d), xs.reshape(sb * st, d)], moe_w_router[0], moe_w_gu_b[0],
                     moe_w_d_b[0], vec(ln_g[1, 2]), vec(ln_b[1, 2]), alpha, 512, 512)

    heads_hd = (depth, b, mem_len, XA_HEADS, d // XA_HEADS)
    pool_p = x_prompt[:, t - n_prev:, :][None]
    pool_s = jnp.concatenate([state_pool[0], x_sample], axis=1)[:, st:, :][None]
    return (yp.reshape(b, t, d), ys.reshape(sb, st, d), pool_p, hg_p[None],
            mk.reshape(heads_hd), mv.reshape(heads_hd), pool_s, hg_s[None])
```

```python
import functools
import math

import jax
import jax.numpy as jnp
from jax import lax
from jax.experimental import pallas as pl
from jax.experimental.pallas import tpu as pltpu

F32 = jnp.float32
BF16 = jnp.bfloat16

POOL_WINDOWS = (2, 4, 8, 16)
POOL_HALO = 16
HG_HEADS = 8
HG_CHUNK = 32
RMS_EPS = 1e-6
XA_HEADS = 4
N_EXPERTS = 8
TOP_K = 2
LN_EPS = 1e-5
LANES = 128
VMEM_LIMIT = 56 * 1024 * 1024

_NT = (((1,), (1,)), ((), ()))
_TN = (((0,), (0,)), ((), ()))


def _cparams(*sem):
    return pltpu.CompilerParams(dimension_semantics=sem, vmem_limit_bytes=VMEM_LIMIT)


def _layer_norm(z, g, b):
    mu = jnp.mean(z, axis=-1, keepdims=True)
    zc = z - mu
    var = jnp.mean(zc * zc, axis=-1, keepdims=True)
    return zc * lax.rsqrt(var + LN_EPS) * g + b


def _dot(a, b):
    return jnp.dot(a, b, preferred_element_type=F32)


def _kv_kernel(mem_ref, w_ref, k_ref, v_ref):
    kv = _dot(mem_ref[...].astype(BF16), w_ref[0])
    n = k_ref.shape[-1]
    k_ref[0] = kv[:, :n]
    v_ref[0] = kv[:, n:]


def _mem_kv(mem2d, w_kv):
    rows, d = mem2d.shape
    depth, _, e2 = w_kv.shape
    e = e2 // 2
    tm = min(rows, 512)
    out = jax.ShapeDtypeStruct((depth, rows, e), F32)
    return pl.pallas_call(
        _kv_kernel,
        grid=(depth, rows // tm),
        in_specs=[pl.BlockSpec((tm, d), lambda l, i: (i, 0)),
                  pl.BlockSpec((1, d, e2), lambda l, i: (l, 0, 0))],
        out_specs=[pl.BlockSpec((1, tm, e), lambda l, i: (l, i, 0)),
                   pl.BlockSpec((1, tm, e), lambda l, i: (l, i, 0))],
        out_shape=[out, out],
        compiler_params=_cparams("parallel", "parallel"),
        name="mem_kv",
    )(mem2d, w_kv)


def _window_sums(a, w):
    s, k = a, 1
    while k < w:
        s = s + pltpu.roll(s, k, 0)
        k *= 2
    return s


def _pool_mix(pooled_minus_tok, w_ref, scale):
    parts = [_dot(pooled_minus_tok[g].astype(BF16), w_ref[g]) for g in range(len(POOL_WINDOWS))]
    return jnp.concatenate(parts, axis=1) * scale


def _pool_prompt_kernel(x_ref, halo_ref, w_ref, scale_ref, g_ref, b_ref, o_ref, buf_ref, *, alpha):
    t = pl.program_id(1)
    tt = x_ref.shape[1]
    gw = w_ref.shape[-1]
    x = x_ref[0]
    buf_ref[0:POOL_HALO, :] = jnp.where(t > 0, halo_ref[0], 0.0)
    buf_ref[POOL_HALO:, :] = x
    pos = t * tt + lax.broadcasted_iota(jnp.int32, (tt, 1), 0)
    diffs = []
    for g, w in enumerate(POOL_WINDOWS):
        sl = slice(g * gw, (g + 1) * gw)
        s = _window_sums(buf_ref[:, sl], w)[POOL_HALO:, :]
        cnt = jnp.minimum(pos + 1, w).astype(F32)
        diffs.append(s / cnt - x[:, sl])
    mixed = _pool_mix(diffs, w_ref, scale_ref[...])
    o_ref[0] = _layer_norm(alpha * x + mixed, g_ref[...], b_ref[...])


def _pool_prompt(x, pool_w, scale, ln_g, ln_b, alpha):
    b, t, d = x.shape
    tt = min(t, 512)
    hb = tt // POOL_HALO
    vec = pl.BlockSpec((1, d), lambda i, j: (0, 0))
    return pl.pallas_call(
        functools.partial(_pool_prompt_kernel, alpha=alpha),
        grid=(b, t // tt),
        in_specs=[pl.BlockSpec((1, tt, d), lambda i, j: (i, j, 0)),
                  pl.BlockSpec((1, POOL_HALO, d), lambda i, j: (i, jnp.maximum(j * hb - 1, 0), 0)),
                  pl.BlockSpec(pool_w.shape, lambda i, j: (0, 0, 0)),
                  vec, vec, vec],
        out_specs=pl.BlockSpec((1, tt, d), lambda i, j: (i, j, 0)),
        out_shape=jax.ShapeDtypeStruct(x.shape, F32),
        scratch_shapes=[pltpu.VMEM((tt + POOL_HALO, d), F32)],
        compiler_params=_cparams("parallel", "parallel"),
        name="pool_prompt",
    )(x, x, pool_w, scale, ln_g, ln_b)


def _pool_sample_kernel(x_ref, prev_ref, w_ref, scale_ref, g_ref, b_ref, o_ref, buf_ref, *, alpha, n_prev):
    nb, t, d = x_ref.shape
    gw = w_ref.shape[-1]
    rows = POOL_HALO + t
    buf_ref[:, 0:POOL_HALO, :] = prev_ref[...]
    buf_ref[:, POOL_HALO:, :] = x_ref[...]
    x = x_ref[...].reshape(nb * t, d)
    pos = lax.broadcasted_iota(jnp.int32, (nb, t, 1), 1).reshape(nb * t, 1)
    diffs = []
    for g, w in enumerate(POOL_WINDOWS):
        sl = slice(g * gw, (g + 1) * gw)
        s = _window_sums(buf_ref[:, :, sl].reshape(nb * rows, gw), w)
        s = s.reshape(nb, rows, gw)[:, POOL_HALO:, :].reshape(nb * t, gw)
        cnt = jnp.minimum(pos + 1 + n_prev, w).astype(F32)
        diffs.append(s / cnt - x[:, sl])
    mixed = _pool_mix(diffs, w_ref, scale_ref[...])
    o_ref[...] = _layer_norm(alpha * x + mixed, g_ref[...], b_ref[...]).reshape(nb, t, d)


def _pool_sample(x, prev_padded, n_prev, pool_w, scale, ln_g, ln_b, alpha):
    b, t, d = x.shape
    nb = min(b, 16)
    vec = pl.BlockSpec((1, d), lambda i: (0, 0))
    return pl.pallas_call(
        functools.partial(_pool_sample_kernel, alpha=alpha, n_prev=n_prev),
        grid=(b // nb,),
        in_specs=[pl.BlockSpec((nb, t, d), lambda i: (i, 0, 0)),
                  pl.BlockSpec((nb, POOL_HALO, d), lambda i: (i, 0, 0)),
                  pl.BlockSpec(pool_w.shape, lambda i: (0, 0, 0)),
                  vec, vec, vec],
        out_specs=pl.BlockSpec((nb, t, d), lambda i: (i, 0, 0)),
        out_shape=jax.ShapeDtypeStruct(x.shape, F32),
        scratch_shapes=[pltpu.VMEM((nb, POOL_HALO + t, d), F32)],
        compiler_params=_cparams("parallel"),
        name="pool_sample",
    )(x, prev_padded, pool_w, scale, ln_g, ln_b)


def _attn_kernel(x_ref, k_ref, v_ref, wq_ref, wo_ref, g_ref, b_ref, o_ref, oh_ref, *, alpha):
    nb, tt, d = x_ref.shape
    hd = d // XA_HEADS
    scale = hd ** -0.5
    x = x_ref[...].reshape(nb * tt, d)
    q = _dot(x.astype(BF16), wq_ref[...])
    for h in range(XA_HEADS):
        sl = slice(h * hd, (h + 1) * hd)
        qh = q[:, sl].astype(BF16).reshape(nb, tt, hd)
        kh = k_ref[:, :, sl].astype(BF16)
        vh = v_ref[:, :, sl].astype(BF16)
        s = jnp.einsum("btd,bmd->btm", qh, kh, preferred_element_type=F32) * scale
        p = jnp.exp(s - jnp.max(s, axis=-1, keepdims=True))
        p = (p / jnp.sum(p, axis=-1, keepdims=True)).astype(BF16)
        oh = jnp.einsum("btm,bmd->btd", p, vh, preferred_element_type=F32)
        oh_ref[:, sl] = oh.reshape(nb * tt, hd).astype(BF16)
    a = _dot(oh_ref[...], wo_ref[...])
    o_ref[...] = _layer_norm(alpha * x + a, g_ref[...], b_ref[...]).reshape(nb, tt, d)


def _mem_attend(x, k4, v4, layer, wq, wo, ln_g, ln_b, alpha, nb, tt):
    b, t, d = x.shape
    m = k4.shape[2]
    vec = pl.BlockSpec((1, d), lambda i, j: (0, 0))
    mat = pl.BlockSpec((d, d), lambda i, j: (0, 0))
    kv = pl.BlockSpec((None, nb, m, d), lambda i, j: (layer, i, 0, 0))
    return pl.pallas_call(
        functools.partial(_attn_kernel, alpha=alpha),
        grid=(b // nb, t // tt),
        in_specs=[pl.BlockSpec((nb, tt, d), lambda i, j: (i, j, 0)), kv, kv, mat, mat, vec, vec],
        out_specs=pl.BlockSpec((nb, tt, d), lambda i, j: (i, j, 0)),
        out_shape=jax.ShapeDtypeStruct(x.shape, F32),
        scratch_shapes=[pltpu.VMEM((nb * tt, d), BF16)],
        compiler_params=_cparams("parallel", "parallel"),
        name=f"mem_attend_l{layer}",
    )(x, k4, v4, wq, wo, ln_g, ln_b)


def _attn_cache_kernel(x_ref, k_ref, v_ref, wq_ref, wo_ref, g_ref, b_ref, o_ref, q_scr, oh_scr, *, alpha):
    nb, t, d = x_ref.shape
    heads = XA_HEADS
    hd = d // heads
    halves = hd // LANES
    assert halves == 2
    rows = k_ref.shape[1]
    qr = heads * t
    scale = hd ** -0.5
    x = x_ref[...].reshape(nb * t, d)
    q = _dot(x.astype(BF16), wq_ref[...])
    for c in range(halves):
        for h in range(heads):
            col = (h * halves + c) * LANES
            q_scr[c, :, h * t:(h + 1) * t, :] = q[:, col:col + LANES].reshape(nb, t, LANES)

    col_id = lax.broadcasted_iota(jnp.int32, (qr, rows), 1)
    row_head = lax.broadcasted_iota(jnp.int32, (qr, rows), 0) // t
    col_half = (col_id // heads) % halves
    col_head = col_id % heads
    keep = jnp.logical_and(col_half == 1, col_head == row_head)

    def per_batch(b, carry):
        kb = k_ref[b].astype(BF16)
        vb = v_ref[b].astype(BF16)
        g0 = lax.dot_general(q_scr[0, b].astype(BF16), kb, _NT, preferred_element_type=F32)
        g1 = lax.dot_general(q_scr[1, b].astype(BF16), kb, _NT, preferred_element_type=F32)
        part = jnp.where(col_half == 0, g0, g1)
        s = (part + pltpu.roll(part, heads, 1)) * scale
        s = jnp.where(keep, s, -jnp.inf)
        p = jnp.exp(s - jnp.max(s, axis=-1, keepdims=True))
        p = p / jnp.sum(p, axis=-1, keepdims=True)
        p_half = (pltpu.roll(p, rows - heads, 1).astype(BF16), p.astype(BF16))
        for c in range(halves):
            oc = _dot(p_half[c], vb)
            for h in range(heads):
                col = (h * halves + c) * LANES
                oh_scr[pl.ds(pl.multiple_of(b * t, t), t), col:col + LANES] = oc[h * t:(h + 1) * t, :]
        return carry

    lax.fori_loop(0, nb, per_batch, 0)
    a = _dot(oh_scr[...].astype(BF16), wo_ref[...])
    o_ref[...] = _layer_norm(alpha * x + a, g_ref[...], b_ref[...]).reshape(nb, t, d)


def _mem_attend_cache(x, k8, v8, layer, wq, wo, ln_g, ln_b, alpha, nb):
    b, t, d = x.shape
    rows = k8.shape[2]
    vec = pl.BlockSpec((1, d), lambda i: (0, 0))
    mat = pl.BlockSpec((d, d), lambda i: (0, 0))
    kv = pl.BlockSpec((None, nb, rows, LANES), lambda i: (layer, i, 0, 0))
    return pl.pallas_call(
        functools.partial(_attn_cache_kernel, alpha=alpha),
        grid=(b // nb,),
        in_specs=[pl.BlockSpec((nb, t, d), lambda i: (i, 0, 0)), kv, kv, mat, mat, vec, vec],
        out_specs=pl.BlockSpec((nb, t, d), lambda i: (i, 0, 0)),
        out_shape=jax.ShapeDtypeStruct(x.shape, F32),
        scratch_shapes=[pltpu.VMEM((2, nb, XA_HEADS * t, LANES), F32), pltpu.VMEM((nb * t, d), F32)],
        compiler_params=_cparams("parallel"),
        name=f"mem_attend_cache_l{layer}",
    )(x, k8, v8, wq, wo, ln_g, ln_b)


def _swiglu_step(x_ref, wg_ref, wu_ref, wd_ref, xb_ref, acc_ref):
    f = pl.program_id(1)

    @pl.when(f == 0)
    def _():
        xb_ref[...] = x_ref[...].astype(BF16)
        acc_ref[...] = jnp.zeros_like(acc_ref)

    xb = xb_ref[...]
    gate = _dot(xb, wg_ref[0])
    up = _dot(xb, wu_ref[0])
    h = (gate * jax.nn.sigmoid(gate) * up).astype(BF16)
    acc_ref[...] += _dot(h, wd_ref[0])


def _ffn_ln_kernel(x_ref, wg_ref, wu_ref, wd_ref, g_ref, b_ref, o_ref, xb_ref, acc_ref, *, alpha):
    _swiglu_step(x_ref, wg_ref, wu_ref, wd_ref, xb_ref, acc_ref)

    @pl.when(pl.program_id(1) == pl.num_programs(1) - 1)
    def _():
        o_ref[...] = _layer_norm(alpha * x_ref[...] + acc_ref[...], g_ref[...], b_ref[...])


def _ffn_ln(x2d, w_gu, w_d, ln_g, ln_b, alpha, tm, tf):
    n, d = x2d.shape
    tm = min(tm, n)
    f = w_d.shape[1]
    nf = f // tf
    vec = pl.BlockSpec((1, d), lambda i, j: (0, 0))
    return pl.pallas_call(
        functools.partial(_ffn_ln_kernel, alpha=alpha),
        grid=(n // tm, nf),
        in_specs=[pl.BlockSpec((tm, d), lambda i, j: (i, 0)),
                  pl.BlockSpec((1, d, tf), lambda i, j: (0, 0, j)),
                  pl.BlockSpec((1, d, tf), lambda i, j: (0, 0, nf + j)),
                  pl.BlockSpec((1, tf, d), lambda i, j: (0, j, 0)),
                  vec, vec],
        out_specs=pl.BlockSpec((tm, d), lambda i, j: (i, 0)),
        out_shape=jax.ShapeDtypeStruct((n, d), F32),
        scratch_shapes=[pltpu.VMEM((tm, d), BF16), pltpu.VMEM((tm, d), F32)],
        compiler_params=_cparams("parallel", "arbitrary"),
        name="ffn_dense",
    )(x2d, w_gu, w_gu, w_d, ln_g, ln_b)


def _expert_kernel(eid_ref, used_ref, x_ref, wg_ref, wu_ref, wd_ref, o_ref, xb_ref, acc_ref):
    i = pl.program_id(0)
    last = pl.program_id(1) == pl.num_programs(1) - 1

    @pl.when(used_ref[i] > 0)
    def _():
        _swiglu_step(x_ref, wg_ref, wu_ref, wd_ref, xb_ref, acc_ref)

        @pl.when(last)
        def _():
            o_ref[...] = acc_ref[...]

    @pl.when(jnp.logical_and(used_ref[i] == 0, last))
    def _():
        o_ref[...] = jnp.zeros_like(o_ref)


def _expert_ffn(xs, tile_eid, tile_used, w_gu, w_d, tm, tf):
    r, d = xs.shape
    f = w_d.shape[1]
    nf = f // tf
    grid_spec = pltpu.PrefetchScalarGridSpec(
        num_scalar_prefetch=2,
        grid=(r // tm, nf),
        in_specs=[pl.BlockSpec((tm, d), lambda i, j, e, u: (i, 0)),
                  pl.BlockSpec((1, d, tf), lambda i, j, e, u: (e[i], 0, j)),
                  pl.BlockSpec((1, d, tf), lambda i, j, e, u: (e[i], 0, nf + j)),
                  pl.BlockSpec((1, tf, d), lambda i, j, e, u: (e[i], j, 0))],
        out_specs=pl.BlockSpec((tm, d), lambda i, j, e, u: (i, 0)),
        scratch_shapes=[pltpu.VMEM((tm, d), BF16), pltpu.VMEM((tm, d), F32)],
    )
    return pl.pallas_call(
        _expert_kernel,
        grid_spec=grid_spec,
        out_shape=jax.ShapeDtypeStruct((r, d), F32),
        compiler_params=_cparams("parallel", "arbitrary"),
        name="ffn_experts",
    )(tile_eid, tile_used, xs, w_gu, w_gu, w_d)


def _hgrn_kernel(*refs, alpha, layer, chunk, has_state):
    st_refs = refs[-HG_HEADS:]
    refs = refs[:-HG_HEADS]
    if has_state:
        (x_ref, s0_ref, w_in_ref, lbp_ref, ng_ref, w_o_ref, g_ref, b_ref,
         o_ref, sfin_ref, proj_ref, og_ref) = refs
    else:
        (x_ref, w_in_ref, lbp_ref, ng_ref, w_o_ref, g_ref, b_ref,
         o_ref, sfin_ref, proj_ref, og_ref) = refs
        s0_ref = None
    t = pl.program_id(1)
    nb, tt, d = x_ref.shape
    heads = HG_HEADS
    dv = ng_ref.shape[-1]
    vdim = heads * dv
    kdim = (w_in_ref.shape[1] - 2 * vdim) // 2
    dk = kdim // heads
    m = nb * tt
    c = chunk
    chunks_per_seq = tt // c

    @pl.when(t == 0)
    def _():
        for h in range(heads):
            if has_state:
                for bi in range(nb):
                    st_refs[h][bi] = s0_ref[bi, h].T
            else:
                st_refs[h][...] = jnp.zeros_like(st_refs[h])

    x = x_ref[...].reshape(m, d)
    proj_ref[...] = _dot(x.astype(BF16), w_in_ref[...])

    lbp = lbp_ref[...]
    e = jnp.exp(lbp - jnp.max(lbp, axis=0, keepdims=True))
    sm = e / jnp.sum(e, axis=0, keepdims=True)
    lb = jnp.sum(sm[1:layer + 1], axis=0, keepdims=True)

    ri = lax.broadcasted_iota(jnp.int32, (c, c), 0)
    ci = lax.broadcasted_iota(jnp.int32, (c, c), 1)
    causal = ri >= ci
    tri = causal.astype(F32)
    mid = max(c // 2 - 1, 0)
    ng = ng_ref[...]

    def chunk_body(j, carry):
        r0 = pl.multiple_of(j * c, c)
        bi = j // chunks_per_seq
        rows = pl.ds(r0, c)
        q = proj_ref[rows, 0:kdim]
        f_pre = proj_ref[rows, kdim:2 * kdim]
        v = proj_ref[rows, 2 * kdim:2 * kdim + vdim]
        gate = proj_ref[rows, 2 * kdim + vdim:]
        f = lb + (1.0 - lb) * jax.nn.sigmoid(f_pre)
        k = 1.0 - f
        bcum = jnp.dot(tri, jnp.log(f), preferred_element_type=F32, precision=lax.Precision.HIGHEST)
        bmid = bcum[mid:mid + 1]
        blast = bcum[c - 1:c]
        q_dec = q * jnp.exp(bcum)
        q_rel = (q_dec * jnp.exp(-bmid)).astype(BF16)
        k_rel32 = k * jnp.exp(bmid - bcum)
        k_rel = k_rel32.astype(BF16)
        k_end = (k_rel32 * jnp.exp(blast - bmid)).astype(BF16)
        q_dec = q_dec.astype(BF16)
        vb = v.astype(BF16)
        decay = jnp.exp(blast)
        out_gate = gate * jax.nn.sigmoid(gate)
        for h in range(heads):
            sk = slice(h * dk, (h + 1) * dk)
            sv = slice(h * dv, (h + 1) * dv)
            a = lax.dot_general(q_rel[:, sk], k_rel[:, sk], _NT, preferred_element_type=F32)
            a = jnp.where(causal, a, 0.0).astype(BF16)
            st = st_refs[h][bi]
            o = _dot(a, vb[:, sv]) + lax.dot_general(q_dec[:, sk], st.astype(BF16), _NT,
                                                      preferred_element_type=F32)
            st_refs[h][bi] = st * decay[:, sk] + lax.dot_general(vb[:, sv], k_end[:, sk], _TN,
                                                                 preferred_element_type=F32)
            o = o * lax.rsqrt(jnp.mean(o * o, axis=-1, keepdims=True) + RMS_EPS) * ng
            og_ref[rows, sv] = (o * out_gate[:, sv]).astype(og_ref.dtype)
        return carry

    lax.fori_loop(0, m // c, chunk_body, 0, unroll=4)

    out = _dot(og_ref[...].astype(BF16), w_o_ref[...])
    o_ref[...] = _layer_norm(alpha * x + out, g_ref[...], b_ref[...]).reshape(nb, tt, d)

    @pl.when(t == pl.num_programs(1) - 1)
    def _():
        for bi in range(nb):
            for h in range(heads):
                sfin_ref[bi, h] = st_refs[h][bi].T


def _hgrn(x, s0, w_in, lb_param, norm_g, w_o, ln_g, ln_b, alpha, layer, chunk, nb, tt):
    b, t, d = x.shape
    heads = HG_HEADS
    dv = norm_g.shape[-1]
    dk = (w_in.shape[1] - 2 * heads * dv) // (2 * heads)
    m = nb * tt
    has_state = s0 is not None
    const2 = lambda i, j: (0, 0)
    state_spec = pl.BlockSpec((nb, heads, dk, dv), lambda i, j: (i, 0, 0, 0))
    in_specs = [pl.BlockSpec((nb, tt, d), lambda i, j: (i, j, 0))]
    args = [x]
    if has_state:
        in_specs.append(state_spec)
        args.append(s0)
    in_specs += [pl.BlockSpec(w_in.shape, const2), pl.BlockSpec(lb_param.shape, const2),
                 pl.BlockSpec(norm_g.shape, const2), pl.BlockSpec(w_o.shape, const2),
                 pl.BlockSpec((1, d), const2), pl.BlockSpec((1, d), const2)]
    args += [w_in, lb_param, norm_g, w_o, ln_g, ln_b]
    return pl.pallas_call(
        functools.partial(_hgrn_kernel, alpha=alpha, layer=layer, chunk=chunk, has_state=has_state),
        grid=(b // nb, t // tt),
        in_specs=in_specs,
        out_specs=[pl.BlockSpec((nb, tt, d), lambda i, j: (i, j, 0)), state_spec],
        out_shape=[jax.ShapeDtypeStruct(x.shape, F32),
                   jax.ShapeDtypeStruct((b, heads, dk, dv), F32)],
        scratch_shapes=[pltpu.VMEM((m, w_in.shape[1]), F32),
                        pltpu.VMEM((m, heads * dv), BF16 if chunk % 16 == 0 else F32)]
        + [pltpu.VMEM((nb, dv, dk), F32) for _ in range(heads)],
        compiler_params=_cparams("parallel", "arbitrary"),
        name="hgrn_state" if has_state else "hgrn_fresh",
    )(*args)


def _split3(x):
    hi = x.astype(BF16).astype(F32)
    r = x - hi
    mid = r.astype(BF16).astype(F32)
    lo = (r - mid).astype(BF16).astype(F32)
    return hi, mid, lo


def _hgrn_tile_kernel(*refs, alpha, layer, chunk, has_state):
    st_refs = refs[-HG_HEADS:]
    refs = refs[:-HG_HEADS]
    s0_ref = None
    if has_state:
        s0_ref, refs = refs[1], refs[:1] + refs[2:]
    (x_ref, w_in_ref, lbp_ref, ng_ref, w_o_ref, g_ref, b_ref, o_ref, sfin_ref,
     proj_ref, lf_ref, bcum_ref, qrel_ref, krel_ref, kend_ref, qdec_ref, vb_ref, dec_ref,
     u_ref, oacc_ref, og_ref) = refs
    t = pl.program_id(1)
    nb, tt, d = x_ref.shape
    heads = HG_HEADS
    dv = ng_ref.shape[-1]
    vdim = heads * dv
    kdim = (w_in_ref.shape[1] - 2 * vdim) // 2
    dk = kdim // heads
    m = nb * tt
    c = chunk
    n_chunks = m // c
    chunks_per_seq = tt // c
    mid = max(c // 2 - 1, 0)

    @pl.when(t == 0)
    def _():
        for h in range(heads):
            if has_state:
                for bi in range(nb):
                    st_refs[h][bi] = s0_ref[bi, h].T
            else:
                st_refs[h][...] = jnp.zeros_like(st_refs[h])

    x = x_ref[...].reshape(m, d)
    proj_ref[...] = _dot(x.astype(BF16), w_in_ref[...])

    lbp = lbp_ref[...]
    e = jnp.exp(lbp - jnp.max(lbp, axis=0, keepdims=True))
    sm = e / jnp.sum(e, axis=0, keepdims=True)
    lb = jnp.sum(sm[1:layer + 1], axis=0, keepdims=True)

    def gate_body(j, carry):
        rows = pl.ds(pl.multiple_of(j * c, c), c)
        f = lb + (1.0 - lb) * jax.nn.sigmoid(proj_ref[rows, kdim:2 * kdim])
        proj_ref[rows, kdim:2 * kdim] = 1.0 - f
        for p, part in enumerate(_split3(jnp.log(f))):
            lf_ref[p, rows, :] = part.astype(lf_ref.dtype)
        return carry

    lax.fori_loop(0, n_chunks, gate_body, 0)

    ri = lax.broadcasted_iota(jnp.int32, (m, m), 0)
    ci = lax.broadcasted_iota(jnp.int32, (m, m), 1)
    causal = jnp.logical_and(ri // c == ci // c, ri >= ci)
    tri = causal.astype(BF16)
    bcum_ref[...] = (_dot(tri, lf_ref[0].astype(BF16)) + _dot(tri, lf_ref[1].astype(BF16))
                     + _dot(tri, lf_ref[2].astype(BF16)))

    def decay_body(j, carry):
        rows = pl.ds(pl.multiple_of(j * c, c), c)
        bcum = bcum_ref[rows, :]
        bmid = bcum[mid:mid + 1]
        blast = bcum[c - 1:c]
        q_dec = proj_ref[rows, 0:kdim] * jnp.exp(bcum)
        k_rel = proj_ref[rows, kdim:2 * kdim] * jnp.exp(bmid - bcum)
        sdt = qrel_ref.dtype
        qdec_ref[rows, :] = q_dec.astype(sdt)
        qrel_ref[rows, :] = (q_dec * jnp.exp(-bmid)).astype(sdt)
        krel_ref[rows, :] = k_rel.astype(sdt)
        kend_ref[rows, :] = (k_rel * jnp.exp(blast - bmid)).astype(sdt)
        vb_ref[rows, :] = proj_ref[rows, 2 * kdim:2 * kdim + vdim].astype(sdt)
        dec_ref[pl.ds(j, 1), :] = jnp.exp(blast)
        return carry

    lax.fori_loop(0, n_chunks, decay_body, 0)

    for h in range(heads):
        sk = slice(h * dk, (h + 1) * dk)
        sv = slice(h * dv, (h + 1) * dv)
        a = lax.dot_general(qrel_ref[:, sk].astype(BF16), krel_ref[:, sk].astype(BF16), _NT,
                            preferred_element_type=F32)
        a = jnp.where(causal, a, 0.0).astype(BF16)
        oacc_ref[:, sv] = _dot(a, vb_ref[:, sv].astype(BF16))

    for j in range(n_chunks):
        rows = slice(j * c, (j + 1) * c)
        for h in range(heads):
            sk = slice(h * dk, (h + 1) * dk)
            sv = slice(h * dv, (h + 1) * dv)
            u_ref[j, h] = lax.dot_general(vb_ref[rows, sv].astype(BF16), kend_ref[rows, sk].astype(BF16), _TN,
                                          preferred_element_type=F32)

    for j in range(n_chunks):
        rows = slice(j * c, (j + 1) * c)
        bi = j // chunks_per_seq
        for h in range(heads):
            sk = slice(h * dk, (h + 1) * dk)
            sv = slice(h * dv, (h + 1) * dv)
            st = st_refs[h][bi]
            oacc_ref[rows, sv] += lax.dot_general(qdec_ref[rows, sk].astype(BF16), st.astype(BF16), _NT,
                                                  preferred_element_type=F32)
            st_refs[h][bi] = st * dec_ref[j:j + 1, sk] + u_ref[j, h]

    ng = ng_ref[...]
    for h in range(heads):
        sv = slice(h * dv, (h + 1) * dv)
        o = oacc_ref[:, sv]
        o = o * lax.rsqrt(jnp.mean(o * o, axis=-1, keepdims=True) + RMS_EPS) * ng
        gate = proj_ref[:, 2 * kdim + vdim + h * dv:2 * kdim + vdim + (h + 1) * dv]
        og_ref[:, sv] = (o * (gate * jax.nn.sigmoid(gate))).astype(BF16)
    out = _dot(og_ref[...], w_o_ref[...])
    o_ref[...] = _layer_norm(alpha * x + out, g_ref[...], b_ref[...]).reshape(nb, tt, d)

    @pl.when(t == pl.num_programs(1) - 1)
    def _():
        for bi in range(nb):
            for h in range(heads):
                sfin_ref[bi, h] = st_refs[h][bi].T


def _hgrn_tiled(x, s0, w_in, lb_param, norm_g, w_o, ln_g, ln_b, alpha, layer, chunk, nb, tt):
    b, t, d = x.shape
    heads = HG_HEADS
    dv = norm_g.shape[-1]
    vdim = heads * dv
    kdim = (w_in.shape[1] - 2 * vdim) // 2
    dk = kdim // heads
    m = nb * tt
    n_chunks = m // chunk
    has_state = s0 is not None
    const2 = lambda i, j: (0, 0)
    state_spec = pl.BlockSpec((nb, heads, dk, dv), lambda i, j: (i, 0, 0, 0))
    in_specs = [pl.BlockSpec((nb, tt, d), lambda i, j: (i, j, 0))]
    args = [x]
    if has_state:
        in_specs.append(state_spec)
        args.append(s0)
    in_specs += [pl.BlockSpec(w_in.shape, const2), pl.BlockSpec(lb_param.shape, const2),
                 pl.BlockSpec(norm_g.shape, const2), pl.BlockSpec(w_o.shape, const2),
                 pl.BlockSpec((1, d), const2), pl.BlockSpec((1, d), const2)]
    args += [w_in, lb_param, norm_g, w_o, ln_g, ln_b]
    sdt = BF16 if chunk % 16 == 0 else F32
    scratch = [pltpu.VMEM((m, w_in.shape[1]), F32),
               pltpu.VMEM((3, m, kdim), sdt),
               pltpu.VMEM((m, kdim), F32),
               pltpu.VMEM((m, kdim), sdt),
               pltpu.VMEM((m, kdim), sdt),
               pltpu.VMEM((m, kdim), sdt),
               pltpu.VMEM((m, kdim), sdt),
               pltpu.VMEM((m, vdim), sdt),
               pltpu.VMEM((n_chunks, kdim), F32),
               pltpu.VMEM((n_chunks, heads, dv, dk), F32),
               pltpu.VMEM((m, vdim), F32),
               pltpu.VMEM((m, vdim), BF16)]
    scratch += [pltpu.VMEM((nb, dv, dk), F32) for _ in range(heads)]
    return pl.pallas_call(
        functools.partial(_hgrn_tile_kernel, alpha=alpha, layer=layer, chunk=chunk, has_state=has_state),
        grid=(b // nb, t // tt),
        in_specs=in_specs,
        out_specs=[pl.BlockSpec((nb, tt, d), lambda i, j: (i, j, 0)), state_spec],
        out_shape=[jax.ShapeDtypeStruct(x.shape, F32),
                   jax.ShapeDtypeStruct((b, heads, dk, dv), F32)],
        scratch_shapes=scratch,
        compiler_params=_cparams("parallel", "arbitrary"),
        name="hgrn_state" if has_state else "hgrn_fresh",
    )(*args)


_I_E0, _I_E1, _I_G0, _I_G1, _I_R0, _I_R1 = range(6)


def _router_kernel(x_ref, w_ref, cnt0_ref, info_ref, cnt_ref, carry_ref):
    i = pl.program_id(0)
    tm = x_ref.shape[0]

    @pl.when(i == 0)
    def _():
        carry_ref[...] = cnt0_ref[...]

    logits = jnp.dot(x_ref[...], w_ref[...], preferred_element_type=F32, precision=lax.Precision.HIGHEST)
    lane = lax.broadcasted_iota(jnp.int32, (tm, LANES), 1)
    valid = lane < N_EXPERTS
    logits = jnp.where(valid, logits, -jnp.inf)
    p = jnp.exp(logits - jnp.max(logits, axis=-1, keepdims=True))
    p = p / jnp.sum(p, axis=-1, keepdims=True)
    p0 = jnp.where(valid, p, -1.0)
    v0 = jnp.max(p0, axis=-1, keepdims=True)
    e0 = jnp.min(jnp.where(p0 == v0, lane, LANES), axis=-1, keepdims=True)
    p1 = jnp.where(lane == e0, -1.0, p0)
    v1 = jnp.max(p1, axis=-1, keepdims=True)
    e1 = jnp.min(jnp.where(p1 == v1, lane, LANES), axis=-1, keepdims=True)
    den = v0 + v1
    sel = jnp.logical_or(lane == e0, lane == e1)
    ri = lax.broadcasted_iota(jnp.int32, (tm, tm), 0)
    ci = lax.broadcasted_iota(jnp.int32, (tm, tm), 1)
    before = (ri > ci).astype(BF16)
    rank = _dot(before, sel.astype(BF16)) + carry_ref[...]
    r0 = jnp.sum(jnp.where(lane == e0, rank, 0.0), axis=-1, keepdims=True)
    r1 = jnp.sum(jnp.where(lane == e1, rank, 0.0), axis=-1, keepdims=True)
    carry_ref[...] += jnp.sum(sel.astype(F32), axis=0, keepdims=True)
    info = jnp.zeros((tm, LANES), F32)
    for idx, val in ((_I_E0, e0.astype(F32)), (_I_E1, e1.astype(F32)), (_I_G0, v0 / den),
                     (_I_G1, v1 / den), (_I_R0, r0), (_I_R1, r1)):
        info = jnp.where(lane == idx, val, info)
    info_ref[...] = info
    cnt_ref[...] = carry_ref[...]


def _router(x2d, w_router_padded, counts_in, tm):
    n, d = x2d.shape
    return pl.pallas_call(
        _router_kernel,
        grid=(n // tm,),
        in_specs=[pl.BlockSpec((tm, d), lambda i: (i, 0)),
                  pl.BlockSpec((d, LANES), lambda i: (0, 0)),
                  pl.BlockSpec((1, LANES), lambda i: (0, 0))],
        out_specs=[pl.BlockSpec((tm, LANES), lambda i: (i, 0)),
                   pl.BlockSpec((1, LANES), lambda i: (0, 0))],
        out_shape=[jax.ShapeDtypeStruct((n, LANES), F32), jax.ShapeDtypeStruct((1, LANES), F32)],
        scratch_shapes=[pltpu.VMEM((1, LANES), F32)],
        compiler_params=_cparams("arbitrary"),
        name="moe_router",
    )(x2d, w_router_padded, counts_in)


def _row_copy(src, src_row, dst, dst_row, sem):
    return pltpu.make_async_copy(src.at[pl.ds(src_row, 1)], dst.at[pl.ds(dst_row, 1)], sem)


def _dispatch_kernel(pos_ref, x_ref, buf_hbm, out_hbm, sem):
    del buf_hbm
    tb = x_ref.shape[0]

    def issue(j, carry):
        for kk in range(TOP_K):
            _row_copy(x_ref, j, out_hbm, pos_ref[TOP_K * j + kk], sem).start()
        return carry

    lax.fori_loop(0, tb, issue, 0, unroll=4)
    for _ in range(TOP_K):
        pltpu.make_async_copy(x_ref, out_hbm.at[pl.ds(0, tb)], sem).wait()


def _dispatch(x2d, pos_flat, buf, tb):
    n, d = x2d.shape
    return pl.pallas_call(
        _dispatch_kernel,
        grid=(n // tb,),
        in_specs=[pl.BlockSpec((TOP_K * tb,), lambda i: (i,), memory_space=pltpu.SMEM),
                  pl.BlockSpec((tb, d), lambda i: (i, 0)),
                  pl.BlockSpec(memory_space=pl.ANY)],
        out_specs=pl.BlockSpec(memory_space=pl.ANY),
        out_shape=jax.ShapeDtypeStruct(buf.shape, buf.dtype),
        scratch_shapes=[pltpu.SemaphoreType.DMA(())],
        input_output_aliases={2: 0},
        compiler_params=_cparams("arbitrary"),
        name="moe_dispatch",
    )(pos_flat, x2d, buf)


def _combine_kernel(pos_ref, posn_ref, x_ref, info_ref, y_hbm, g_ref, b_ref, o_ref, ybuf, sem, *, alpha):
    i = pl.program_id(0)
    n = pl.num_programs(0)
    tm = x_ref.shape[0]

    def issue(p_ref, slot):
        def body(j, carry):
            for kk in range(TOP_K):
                _row_copy(y_hbm, p_ref[TOP_K * j + kk], ybuf.at[slot, kk], j, sem.at[slot]).start()
            return carry
        lax.fori_loop(0, tm, body, 0)

    @pl.when(i == 0)
    def _():
        issue(pos_ref, 0)

    @pl.when(i + 1 < n)
    def _():
        issue(posn_ref, (i + 1) % 2)

    slot = i % 2
    for kk in range(TOP_K):
        pltpu.make_async_copy(y_hbm.at[pl.ds(0, tm)], ybuf.at[slot, kk], sem.at[slot]).wait()
    info = info_ref[...]
    moe = info[:, _I_G0:_I_G0 + 1] * ybuf[slot, 0] + info[:, _I_G1:_I_G1 + 1] * ybuf[slot, 1]
    o_ref[...] = _layer_norm(alpha * x_ref[...] + moe, g_ref[...], b_ref[...])


def _combine_ln(x2d, info, pos_flat, y, ln_g, ln_b, alpha, tm):
    n, d = x2d.shape
    nt = n // tm
    vec = pl.BlockSpec((1, d), lambda i: (0, 0))
    return pl.pallas_call(
        functools.partial(_combine_kernel, alpha=alpha),
        grid=(nt,),
        in_specs=[pl.BlockSpec((TOP_K * tm,), lambda i: (i,), memory_space=pltpu.SMEM),
                  pl.BlockSpec((TOP_K * tm,), lambda i: (jnp.minimum(i + 1, nt - 1),), memory_space=pltpu.SMEM),
                  pl.BlockSpec((tm, d), lambda i: (i, 0)),
                  pl.BlockSpec((tm, LANES), lambda i: (i, 0)),
                  pl.BlockSpec(memory_space=pl.ANY),
                  vec, vec],
        out_specs=pl.BlockSpec((tm, d), lambda i: (i, 0)),
        out_shape=jax.ShapeDtypeStruct((n, d), F32),
        scratch_shapes=[pltpu.VMEM((2, TOP_K, tm, d), F32), pltpu.SemaphoreType.DMA((2,))],
        compiler_params=_cparams("arbitrary"),
        name="moe_combine",
    )(pos_flat, pos_flat, x2d, info, y, ln_g, ln_b)


def _moe_ln(xs2d, w_router, w_gu, w_d, ln_g, ln_b, alpha, tm_expert, tf):
    d = xs2d[0].shape[1]
    n_total = sum(x.shape[0] for x in xs2d)
    w_r = jnp.pad(w_router, ((0, 0), (0, LANES - N_EXPERTS)))
    counts = jnp.zeros((1, LANES), F32)
    infos = []
    for x in xs2d:
        info, counts = _router(x, w_r, counts, min(x.shape[0], 1024))
        infos.append(info)
    cnt = counts[0, :N_EXPERTS].astype(jnp.int32)
    padded = (cnt + tm_expert - 1) // tm_expert * tm_expert
    ends = jnp.cumsum(padded)
    offs = ends - padded
    n_rows = -(-(TOP_K * n_total) // tm_expert) * tm_expert + N_EXPERTS * tm_expert
    tile_start = jnp.arange(n_rows // tm_expert, dtype=jnp.int32) * tm_expert
    tile_eid = jnp.minimum(jnp.sum(tile_start[:, None] >= ends[None, :], axis=1), N_EXPERTS - 1).astype(jnp.int32)
    tile_used = (tile_start < offs[tile_eid] + cnt[tile_eid]).astype(jnp.int32)
    buf = jnp.zeros((n_rows, d), F32)
    poss = []
    for x, info in zip(xs2d, infos):
        eid = info[:, _I_E0:_I_E1 + 1].astype(jnp.int32)
        rank = info[:, _I_R0:_I_R1 + 1].astype(jnp.int32)
        pos = (offs[eid] + rank).reshape(-1)
        poss.append(pos)
        buf = _dispatch(x, pos, buf, min(x.shape[0], 1024))
    y = _expert_ffn(buf, tile_eid, tile_used, w_gu, w_d, tm_expert, tf)
    return [_combine_ln(x, info, pos, y, ln_g, ln_b, alpha, min(x.shape[0], 256))
            for x, info, pos in zip(xs2d, infos, poss)]


def kernel(x_prompt, x_sample, mem_prompt, state_pool, state_hgrn, cache_mem_k, cache_mem_v, pool_w, pool_scale, hg_w_in, hg_lb, hg_norm_g, hg_w_o, xa_w_q, xa_w_kv, xa_w_o, ffn_w_gu, ffn_w_d, moe_w_router, moe_w_gu, moe_w_d, ln_g, ln_b):
    b, t, d = x_prompt.shape
    sb, st, _ = x_sample.shape
    depth = ln_g.shape[0]
    mem_len = mem_prompt.shape[1]
    alpha = (2 * depth) ** 0.25
    n_prev = state_pool.shape[2]

    pool_w_b = pool_w.astype(BF16)
    hg_w_in_b = hg_w_in.astype(BF16)
    hg_w_o_b = hg_w_o.astype(BF16)
    xa_w_q_b = xa_w_q.astype(BF16)
    xa_w_kv_b = xa_w_kv.astype(BF16)
    xa_w_o_b = xa_w_o.astype(BF16)
    ffn_w_gu_b = ffn_w_gu.astype(BF16)
    ffn_w_d_b = ffn_w_d.astype(BF16)
    moe_w_gu_b = moe_w_gu.astype(BF16)
    moe_w_d_b = moe_w_d.astype(BF16)

    def vec(a):
        return a.reshape(1, -1)

    mk, mv = _mem_kv(mem_prompt.reshape(b * mem_len, d), xa_w_kv_b)
    mk4 = mk.reshape(depth, b, mem_len, d)
    mv4 = mv.reshape(depth, b, mem_len, d)

    def stored_order(c):
        halves = c.shape[4] // LANES
        c = c.reshape(depth, sb, mem_len, XA_HEADS, halves, LANES).transpose(0, 1, 2, 4, 3, 5)
        return c.reshape(depth, sb, mem_len * halves * XA_HEADS, LANES)

    ck8 = stored_order(cache_mem_k)
    cv8 = stored_order(cache_mem_v)

    tt_attn = min(t, 512)
    nb_attn = min(sb, 8)

    def attend(xp, xs, layer):
        args = (xa_w_q_b[layer], xa_w_o_b[layer], vec(ln_g[layer, 1]), vec(ln_b[layer, 1]), alpha)
        return (_mem_attend(xp, mk4, mv4, layer, *args, 1, tt_attn),
                _mem_attend_cache(xs, ck8, cv8, layer, *args, nb_attn))

    pool_args = (pool_w_b[0], vec(pool_scale[0]), vec(ln_g[0, 0]), vec(ln_b[0, 0]), alpha)
    xp = _pool_prompt(x_prompt, *pool_args)
    prev = jnp.pad(state_pool[0], ((0, 0), (POOL_HALO - n_prev, 0), (0, 0)))
    xs = _pool_sample(x_sample, prev, n_prev, *pool_args)
    xp, xs = attend(xp, xs, 0)
    ffn_args = (ffn_w_gu_b, ffn_w_d_b, vec(ln_g[0, 2]), vec(ln_b[0, 2]), alpha)
    xp = _ffn_ln(xp.reshape(b * t, d), *ffn_args, 512, 512).reshape(b, t, d)
    xs = _ffn_ln(xs.reshape(sb * st, d), *ffn_args, 512, 512).reshape(sb, st, d)

    hg_args = (hg_w_in_b[0], hg_lb, vec(hg_norm_g[0]), hg_w_o_b[0], vec(ln_g[1, 0]), vec(ln_b[1, 0]), alpha, 1)
    xp, hg_p = _hgrn_tiled(xp, None, *hg_args, HG_CHUNK, 1, min(t, 256))
    xs, hg_s = _hgrn_tiled(xs, state_hgrn[0], *hg_args, math.gcd(st, HG_CHUNK), min(sb, 8), st)
    xp, xs = attend(xp, xs, 1)
    yp, ys = _moe_ln([xp.reshape(b * t, d), xs.reshape(sb * st, d)], moe_w_router[0], moe_w_gu_b[0],
                     moe_w_d_b[0], vec(ln_g[1, 2]), vec(ln_b[1, 2]), alpha, 512, 512)

    heads_hd = (depth, b, mem_len, XA_HEADS, d // XA_HEADS)
    pool_p = x_prompt[:, t - n_prev:, :][None]
    pool_s = jnp.concatenate([state_pool[0], x_sample], axis=1)[:, st:, :][None]
    return (yp.reshape(b, t, d), ys.reshape(sb, st, d), pool_p, hg_p[None],
            mk.reshape(heads_hd), mv.reshape(heads_hd), pool_s, hg_s[None])
```

```python
import functools
import math

import jax
import jax.numpy as jnp
from jax import lax
from jax.experimental import pallas as pl
from jax.experimental.pallas import tpu as pltpu

F32 = jnp.float32
BF16 = jnp.bfloat16

POOL_WINDOWS = (2, 4, 8, 16)
POOL_HALO = 16
HG_HEADS = 8
HG_CHUNK = 32
RMS_EPS = 1e-6
XA_HEADS = 4
N_EXPERTS = 8
TOP_K = 2
LN_EPS = 1e-5
LANES = 128
VMEM_LIMIT = 56 * 1024 * 1024

_NT = (((1,), (1,)), ((), ()))
_TN = (((0,), (0,)), ((), ()))


def _cparams(*sem):
    return pltpu.CompilerParams(dimension_semantics=sem, vmem_limit_bytes=VMEM_LIMIT)


def _layer_norm(z, g, b):
    mu = jnp.mean(z, axis=-1, keepdims=True)
    zc = z - mu
    var = jnp.mean(zc * zc, axis=-1, keepdims=True)
    return zc * lax.rsqrt(var + LN_EPS) * g + b


def _dot(a, b):
    return jnp.dot(a, b, preferred_element_type=F32)


def _kv_kernel(mem_ref, w_ref, k_ref, v_ref):
    kv = _dot(mem_ref[...].astype(BF16), w_ref[0])
    n = k_ref.shape[-1]
    k_ref[0] = kv[:, :n]
    v_ref[0] = kv[:, n:]


def _mem_kv(mem2d, w_kv):
    rows, d = mem2d.shape
    depth, _, e2 = w_kv.shape
    e = e2 // 2
    tm = min(rows, 512)
    out = jax.ShapeDtypeStruct((depth, rows, e), F32)
    return pl.pallas_call(
        _kv_kernel,
        grid=(depth, rows // tm),
        in_specs=[pl.BlockSpec((tm, d), lambda l, i: (i, 0)),
                  pl.BlockSpec((1, d, e2), lambda l, i: (l, 0, 0))],
        out_specs=[pl.BlockSpec((1, tm, e), lambda l, i: (l, i, 0)),
                   pl.BlockSpec((1, tm, e), lambda l, i: (l, i, 0))],
        out_shape=[out, out],
        compiler_params=_cparams("parallel", "parallel"),
        name="mem_kv",
    )(mem2d, w_kv)


def _window_sums(a, w):
    s, k = a, 1
    while k < w:
        s = s + pltpu.roll(s, k, 0)
        k *= 2
    return s


def _pool_mix(pooled_minus_tok, w_ref, scale):
    parts = [_dot(pooled_minus_tok[g].astype(BF16), w_ref[g]) for g in range(len(POOL_WINDOWS))]
    return jnp.concatenate(parts, axis=1) * scale


def _pool_prompt_kernel(x_ref, halo_ref, w_ref, scale_ref, g_ref, b_ref, o_ref, buf_ref, *, alpha):
    t = pl.program_id(1)
    tt = x_ref.shape[1]
    gw = w_ref.shape[-1]
    x = x_ref[0]
    buf_ref[0:POOL_HALO, :] = jnp.where(t > 0, halo_ref[0], 0.0)
    buf_ref[POOL_HALO:, :] = x
    pos = t * tt + lax.broadcasted_iota(jnp.int32, (tt, 1), 0)
    diffs = []
    for g, w in enumerate(POOL_WINDOWS):
        sl = slice(g * gw, (g + 1) * gw)
        s = _window_sums(buf_ref[:, sl], w)[POOL_HALO:, :]
        cnt = jnp.minimum(pos + 1, w).astype(F32)
        diffs.append(s / cnt - x[:, sl])
    mixed = _pool_mix(diffs, w_ref, scale_ref[...])
    o_ref[0] = _layer_norm(alpha * x + mixed, g_ref[...], b_ref[...])


def _pool_prompt(x, pool_w, scale, ln_g, ln_b, alpha):
    b, t, d = x.shape
    tt = min(t, 512)
    hb = tt // POOL_HALO
    vec = pl.BlockSpec((1, d), lambda i, j: (0, 0))
    return pl.pallas_call(
        functools.partial(_pool_prompt_kernel, alpha=alpha),
        grid=(b, t // tt),
        in_specs=[pl.BlockSpec((1, tt, d), lambda i, j: (i, j, 0)),
                  pl.BlockSpec((1, POOL_HALO, d), lambda i, j: (i, jnp.maximum(j * hb - 1, 0), 0)),
                  pl.BlockSpec(pool_w.shape, lambda i, j: (0, 0, 0)),
                  vec, vec, vec],
        out_specs=pl.BlockSpec((1, tt, d), lambda i, j: (i, j, 0)),
        out_shape=jax.ShapeDtypeStruct(x.shape, F32),
        scratch_shapes=[pltpu.VMEM((tt + POOL_HALO, d), F32)],
        compiler_params=_cparams("parallel", "parallel"),
        name="pool_prompt",
    )(x, x, pool_w, scale, ln_g, ln_b)


def _pool_sample_kernel(x_ref, prev_ref, w_ref, scale_ref, g_ref, b_ref, o_ref, buf_ref, *, alpha, n_prev):
    nb, t, d = x_ref.shape
    gw = w_ref.shape[-1]
    rows = POOL_HALO + t
    buf_ref[:, 0:POOL_HALO, :] = prev_ref[...]
    buf_ref[:, POOL_HALO:, :] = x_ref[...]
    x = x_ref[...].reshape(nb * t, d)
    pos = lax.broadcasted_iota(jnp.int32, (nb, t, 1), 1).reshape(nb * t, 1)
    diffs = []
    for g, w in enumerate(POOL_WINDOWS):
        sl = slice(g * gw, (g + 1) * gw)
        s = _window_sums(buf_ref[:, :, sl].reshape(nb * rows, gw), w)
        s = s.reshape(nb, rows, gw)[:, POOL_HALO:, :].reshape(nb * t, gw)
        cnt = jnp.minimum(pos + 1 + n_prev, w).astype(F32)
        diffs.append(s / cnt - x[:, sl])
    mixed = _pool_mix(diffs, w_ref, scale_ref[...])
    o_ref[...] = _layer_norm(alpha * x + mixed, g_ref[...], b_ref[...]).reshape(nb, t, d)


def _pool_sample(x, prev_padded, n_prev, pool_w, scale, ln_g, ln_b, alpha):
    b, t, d = x.shape
    nb = min(b, 16)
    vec = pl.BlockSpec((1, d), lambda i: (0, 0))
    return pl.pallas_call(
        functools.partial(_pool_sample_kernel, alpha=alpha, n_prev=n_prev),
        grid=(b // nb,),
        in_specs=[pl.BlockSpec((nb, t, d), lambda i: (i, 0, 0)),
                  pl.BlockSpec((nb, POOL_HALO, d), lambda i: (i, 0, 0)),
                  pl.BlockSpec(pool_w.shape, lambda i: (0, 0, 0)),
                  vec, vec, vec],
        out_specs=pl.BlockSpec((nb, t, d), lambda i: (i, 0, 0)),
        out_shape=jax.ShapeDtypeStruct(x.shape, F32),
        scratch_shapes=[pltpu.VMEM((nb, POOL_HALO + t, d), F32)],
        compiler_params=_cparams("parallel"),
        name="pool_sample",
    )(x, prev_padded, pool_w, scale, ln_g, ln_b)


def _attn_kernel(x_ref, k_ref, v_ref, wq_ref, wo_ref, g_ref, b_ref, o_ref, oh_ref, *, alpha):
    nb, tt, d = x_ref.shape
    hd = d // XA_HEADS
    scale = hd ** -0.5
    x = x_ref[...].reshape(nb * tt, d)
    q = _dot(x.astype(BF16), wq_ref[...])
    for h in range(XA_HEADS):
        sl = slice(h * hd, (h + 1) * hd)
        qh = q[:, sl].astype(BF16).reshape(nb, tt, hd)
        kh = k_ref[:, :, sl].astype(BF16)
        vh = v_ref[:, :, sl].astype(BF16)
        s = jnp.einsum("btd,bmd->btm", qh, kh, preferred_element_type=F32) * scale
        p = jnp.exp(s - jnp.max(s, axis=-1, keepdims=True))
        p = (p / jnp.sum(p, axis=-1, keepdims=True)).astype(BF16)
        oh = jnp.einsum("btm,bmd->btd", p, vh, preferred_element_type=F32)
        oh_ref[:, sl] = oh.reshape(nb * tt, hd).astype(BF16)
    a = _dot(oh_ref[...], wo_ref[...])
    o_ref[...] = _layer_norm(alpha * x + a, g_ref[...], b_ref[...]).reshape(nb, tt, d)


def _mem_attend(x, k4, v4, layer, wq, wo, ln_g, ln_b, alpha, nb, tt):
    b, t, d = x.shape
    m = k4.shape[2]
    vec = pl.BlockSpec((1, d), lambda i, j: (0, 0))
    mat = pl.BlockSpec((d, d), lambda i, j: (0, 0))
    kv = pl.BlockSpec((None, nb, m, d), lambda i, j: (layer, i, 0, 0))
    return pl.pallas_call(
        functools.partial(_attn_kernel, alpha=alpha),
        grid=(b // nb, t // tt),
        in_specs=[pl.BlockSpec((nb, tt, d), lambda i, j: (i, j, 0)), kv, kv, mat, mat, vec, vec],
        out_specs=pl.BlockSpec((nb, tt, d), lambda i, j: (i, j, 0)),
        out_shape=jax.ShapeDtypeStruct(x.shape, F32),
        scratch_shapes=[pltpu.VMEM((nb * tt, d), BF16)],
        compiler_params=_cparams("parallel", "parallel"),
        name=f"mem_attend_l{layer}",
    )(x, k4, v4, wq, wo, ln_g, ln_b)


def _attn_cache_kernel(x_ref, k_ref, v_ref, wq_ref, wo_ref, g_ref, b_ref, o_ref, q_scr, oh_scr, *, alpha):
    nb, t, d = x_ref.shape
    heads = XA_HEADS
    hd = d // heads
    halves = hd // LANES
    assert halves == 2
    rows = k_ref.shape[1]
    qr = heads * t
    scale = hd ** -0.5
    x = x_ref[...].reshape(nb * t, d)
    q = _dot(x.astype(BF16), wq_ref[...])
    for c in range(halves):
        for h in range(heads):
            col = (h * halves + c) * LANES
            q_scr[c, :, h * t:(h + 1) * t, :] = q[:, col:col + LANES].reshape(nb, t, LANES)

    col_id = lax.broadcasted_iota(jnp.int32, (qr, rows), 1)
    row_head = lax.broadcasted_iota(jnp.int32, (qr, rows), 0) // t
    col_half = (col_id // heads) % halves
    col_head = col_id % heads
    keep = jnp.logical_and(col_half == 1, col_head == row_head)

    def per_batch(b, carry):
        kb = k_ref[b].astype(BF16)
        vb = v_ref[b].astype(BF16)
        g0 = lax.dot_general(q_scr[0, b].astype(BF16), kb, _NT, preferred_element_type=F32)
        g1 = lax.dot_general(q_scr[1, b].astype(BF16), kb, _NT, preferred_element_type=F32)
        part = jnp.where(col_half == 0, g0, g1)
        s = (part + pltpu.roll(part, heads, 1)) * scale
        s = jnp.where(keep, s, -jnp.inf)
        p = jnp.exp(s - jnp.max(s, axis=-1, keepdims=True))
        p = p / jnp.sum(p, axis=-1, keepdims=True)
        p_half = (pltpu.roll(p, rows - heads, 1).astype(BF16), p.astype(BF16))
        for c in range(halves):
            oc = _dot(p_half[c], vb)
            for h in range(heads):
                col = (h * halves + c) * LANES
                oh_scr[pl.ds(pl.multiple_of(b * t, t), t), col:col + LANES] = oc[h * t:(h + 1) * t, :]
        return carry

    lax.fori_loop(0, nb, per_batch, 0)
    a = _dot(oh_scr[...].astype(BF16), wo_ref[...])
    o_ref[...] = _layer_norm(alpha * x + a, g_ref[...], b_ref[...]).reshape(nb, t, d)


def _mem_attend_cache(x, k8, v8, layer, wq, wo, ln_g, ln_b, alpha, nb):
    b, t, d = x.shape
    rows = k8.shape[2]
    vec = pl.BlockSpec((1, d), lambda i: (0, 0))
    mat = pl.BlockSpec((d, d), lambda i: (0, 0))
    kv = pl.BlockSpec((None, nb, rows, LANES), lambda i: (layer, i, 0, 0))
    return pl.pallas_call(
        functools.partial(_attn_cache_kernel, alpha=alpha),
        grid=(b // nb,),
        in_specs=[pl.BlockSpec((nb, t, d), lambda i: (i, 0, 0)), kv, kv, mat, mat, vec, vec],
        out_specs=pl.BlockSpec((nb, t, d), lambda i: (i, 0, 0)),
        out_shape=jax.ShapeDtypeStruct(x.shape, F32),
        scratch_shapes=[pltpu.VMEM((2, nb, XA_HEADS * t, LANES), F32), pltpu.VMEM((nb * t, d), F32)],
        compiler_params=_cparams("parallel"),
        name=f"mem_attend_cache_l{layer}",
    )(x, k8, v8, wq, wo, ln_g, ln_b)


FFN_SUB = 512


def _swiglu(x_ref, wg_ref, wu_ref, wd_ref, h_ref):
    xb = x_ref[...].astype(BF16)
    f = h_ref.shape[1]
    for s in range(f // FFN_SUB):
        sl = slice(s * FFN_SUB, (s + 1) * FFN_SUB)
        gate = _dot(xb, wg_ref[0, :, sl])
        up = _dot(xb, wu_ref[0, :, sl])
        h_ref[:, sl] = (gate * jax.nn.sigmoid(gate) * up).astype(BF16)
    return _dot(h_ref[...], wd_ref[0])


def _ffn_ln_kernel(x_ref, wg_ref, wu_ref, wd_ref, g_ref, b_ref, o_ref, h_ref, *, alpha):
    y = _swiglu(x_ref, wg_ref, wu_ref, wd_ref, h_ref)
    o_ref[...] = _layer_norm(alpha * x_ref[...] + y, g_ref[...], b_ref[...])


def _ffn_ln(x2d, w_gu, w_d, ln_g, ln_b, alpha, tm):
    n, d = x2d.shape
    tm = min(tm, n)
    f = w_d.shape[1]
    vec = pl.BlockSpec((1, d), lambda i: (0, 0))
    once = pl.Buffered(1)
    return pl.pallas_call(
        functools.partial(_ffn_ln_kernel, alpha=alpha),
        grid=(n // tm,),
        in_specs=[pl.BlockSpec((tm, d), lambda i: (i, 0)),
                  pl.BlockSpec((1, d, f), lambda i: (0, 0, 0), pipeline_mode=once),
                  pl.BlockSpec((1, d, f), lambda i: (0, 0, 1), pipeline_mode=once),
                  pl.BlockSpec((1, f, d), lambda i: (0, 0, 0), pipeline_mode=once),
                  vec, vec],
        out_specs=pl.BlockSpec((tm, d), lambda i: (i, 0)),
        out_shape=jax.ShapeDtypeStruct((n, d), F32),
        scratch_shapes=[pltpu.VMEM((tm, f), BF16)],
        compiler_params=_cparams("parallel"),
        name="ffn_dense",
    )(x2d, w_gu, w_gu, w_d, ln_g, ln_b)


def _expert_kernel(eid_ref, used_ref, x_ref, wg_ref, wu_ref, wd_ref, o_ref, h_ref):
    i = pl.program_id(0)

    @pl.when(used_ref[i] > 0)
    def _():
        o_ref[...] = _swiglu(x_ref, wg_ref, wu_ref, wd_ref, h_ref)

    @pl.when(used_ref[i] == 0)
    def _():
        o_ref[...] = jnp.zeros_like(o_ref)


def _expert_ffn(xs, tile_eid, tile_used, w_gu, w_d, tm):
    r, d = xs.shape
    f = w_d.shape[1]
    grid_spec = pltpu.PrefetchScalarGridSpec(
        num_scalar_prefetch=2,
        grid=(r // tm,),
        in_specs=[pl.BlockSpec((tm, d), lambda i, e, u: (i, 0)),
                  pl.BlockSpec((1, d, f), lambda i, e, u: (e[i], 0, 0)),
                  pl.BlockSpec((1, d, f), lambda i, e, u: (e[i], 0, 1)),
                  pl.BlockSpec((1, f, d), lambda i, e, u: (e[i], 0, 0))],
        out_specs=pl.BlockSpec((tm, d), lambda i, e, u: (i, 0)),
        scratch_shapes=[pltpu.VMEM((tm, f), BF16)],
    )
    return pl.pallas_call(
        _expert_kernel,
        grid_spec=grid_spec,
        out_shape=jax.ShapeDtypeStruct((r, d), F32),
        compiler_params=_cparams("arbitrary"),
        name="ffn_experts",
    )(tile_eid, tile_used, xs, w_gu, w_gu, w_d)


def _hgrn_kernel(*refs, alpha, layer, chunk, has_state):
    st_refs = refs[-HG_HEADS:]
    refs = refs[:-HG_HEADS]
    if has_state:
        (x_ref, s0_ref, w_in_ref, lbp_ref, ng_ref, w_o_ref, g_ref, b_ref,
         o_ref, sfin_ref, proj_ref, og_ref) = refs
    else:
        (x_ref, w_in_ref, lbp_ref, ng_ref, w_o_ref, g_ref, b_ref,
         o_ref, sfin_ref, proj_ref, og_ref) = refs
        s0_ref = None
    t = pl.program_id(1)
    nb, tt, d = x_ref.shape
    heads = HG_HEADS
    dv = ng_ref.shape[-1]
    vdim = heads * dv
    kdim = (w_in_ref.shape[1] - 2 * vdim) // 2
    dk = kdim // heads
    m = nb * tt
    c = chunk
    chunks_per_seq = tt // c

    @pl.when(t == 0)
    def _():
        for h in range(heads):
            if has_state:
                for bi in range(nb):
                    st_refs[h][bi] = s0_ref[bi, h].T
            else:
                st_refs[h][...] = jnp.zeros_like(st_refs[h])

    x = x_ref[...].reshape(m, d)
    proj_ref[...] = _dot(x.astype(BF16), w_in_ref[...])

    lbp = lbp_ref[...]
    e = jnp.exp(lbp - jnp.max(lbp, axis=0, keepdims=True))
    sm = e / jnp.sum(e, axis=0, keepdims=True)
    lb = jnp.sum(sm[1:layer + 1], axis=0, keepdims=True)

    ri = lax.broadcasted_iota(jnp.int32, (c, c), 0)
    ci = lax.broadcasted_iota(jnp.int32, (c, c), 1)
    causal = ri >= ci
    tri = causal.astype(F32)
    mid = max(c // 2 - 1, 0)
    ng = ng_ref[...]

    def chunk_body(j, carry):
        r0 = pl.multiple_of(j * c, c)
        bi = j // chunks_per_seq
        rows = pl.ds(r0, c)
        q = proj_ref[rows, 0:kdim]
        f_pre = proj_ref[rows, kdim:2 * kdim]
        v = proj_ref[rows, 2 * kdim:2 * kdim + vdim]
        gate = proj_ref[rows, 2 * kdim + vdim:]
        f = lb + (1.0 - lb) * jax.nn.sigmoid(f_pre)
        k = 1.0 - f
        bcum = jnp.dot(tri, jnp.log(f), preferred_element_type=F32, precision=lax.Precision.HIGHEST)
        bmid = bcum[mid:mid + 1]
        blast = bcum[c - 1:c]
        q_dec = q * jnp.exp(bcum)
        q_rel = (q_dec * jnp.exp(-bmid)).astype(BF16)
        k_rel32 = k * jnp.exp(bmid - bcum)
        k_rel = k_rel32.astype(BF16)
        k_end = (k_rel32 * jnp.exp(blast - bmid)).astype(BF16)
        q_dec = q_dec.astype(BF16)
        vb = v.astype(BF16)
        decay = jnp.exp(blast)
        out_gate = gate * jax.nn.sigmoid(gate)
        for h in range(heads):
            sk = slice(h * dk, (h + 1) * dk)
            sv = slice(h * dv, (h + 1) * dv)
            a = lax.dot_general(q_rel[:, sk], k_rel[:, sk], _NT, preferred_element_type=F32)
            a = jnp.where(causal, a, 0.0).astype(BF16)
            st = st_refs[h][bi]
            o = _dot(a, vb[:, sv]) + lax.dot_general(q_dec[:, sk], st.astype(BF16), _NT,
                                                      preferred_element_type=F32)
            st_refs[h][bi] = st * decay[:, sk] + lax.dot_general(vb[:, sv], k_end[:, sk], _TN,
                                                                 preferred_element_type=F32)
            o = o * lax.rsqrt(jnp.mean(o * o, axis=-1, keepdims=True) + RMS_EPS) * ng
            og_ref[rows, sv] = (o * out_gate[:, sv]).astype(og_ref.dtype)
        return carry

    lax.fori_loop(0, m // c, chunk_body, 0, unroll=4)

    out = _dot(og_ref[...].astype(BF16), w_o_ref[...])
    o_ref[...] = _layer_norm(alpha * x + out, g_ref[...], b_ref[...]).reshape(nb, tt, d)

    @pl.when(t == pl.num_programs(1) - 1)
    def _():
        for bi in range(nb):
            for h in range(heads):
                sfin_ref[bi, h] = st_refs[h][bi].T


def _hgrn(x, s0, w_in, lb_param, norm_g, w_o, ln_g, ln_b, alpha, layer, chunk, nb, tt):
    b, t, d = x.shape
    heads = HG_HEADS
    dv = norm_g.shape[-1]
    dk = (w_in.shape[1] - 2 * heads * dv) // (2 * heads)
    m = nb * tt
    has_state = s0 is not None
    const2 = lambda i, j: (0, 0)
    state_spec = pl.BlockSpec((nb, heads, dk, dv), lambda i, j: (i, 0, 0, 0))
    in_specs = [pl.BlockSpec((nb, tt, d), lambda i, j: (i, j, 0))]
    args = [x]
    if has_state:
        in_specs.append(state_spec)
        args.append(s0)
    in_specs += [pl.BlockSpec(w_in.shape, const2), pl.BlockSpec(lb_param.shape, const2),
                 pl.BlockSpec(norm_g.shape, const2), pl.BlockSpec(w_o.shape, const2),
                 pl.BlockSpec((1, d), const2), pl.BlockSpec((1, d), const2)]
    args += [w_in, lb_param, norm_g, w_o, ln_g, ln_b]
    return pl.pallas_call(
        functools.partial(_hgrn_kernel, alpha=alpha, layer=layer, chunk=chunk, has_state=has_state),
        grid=(b // nb, t // tt),
        in_specs=in_specs,
        out_specs=[pl.BlockSpec((nb, tt, d), lambda i, j: (i, j, 0)), state_spec],
        out_shape=[jax.ShapeDtypeStruct(x.shape, F32),
                   jax.ShapeDtypeStruct((b, heads, dk, dv), F32)],
        scratch_shapes=[pltpu.VMEM((m, w_in.shape[1]), F32),
                        pltpu.VMEM((m, heads * dv), BF16 if chunk % 16 == 0 else F32)]
        + [pltpu.VMEM((nb, dv, dk), F32) for _ in range(heads)],
        compiler_params=_cparams("parallel", "arbitrary"),
        name="hgrn_state" if has_state else "hgrn_fresh",
    )(*args)


def _split3(x):
    hi = x.astype(BF16).astype(F32)
    r = x - hi
    mid = r.astype(BF16).astype(F32)
    lo = (r - mid).astype(BF16).astype(F32)
    return hi, mid, lo


def _hgrn_tile_kernel(*refs, alpha, layer, chunk, has_state):
    st_refs = refs[-HG_HEADS:]
    refs = refs[:-HG_HEADS]
    s0_ref = None
    if has_state:
        s0_ref, refs = refs[1], refs[:1] + refs[2:]
    (x_ref, w_in_ref, lbp_ref, ng_ref, w_o_ref, g_ref, b_ref, o_ref, sfin_ref,
     proj_ref, lf_ref, bcum_ref, qrel_ref, krel_ref, kend_ref, qdec_ref, vb_ref, dec_ref,
     u_ref, oacc_ref, og_ref) = refs
    t = pl.program_id(1)
    nb, tt, d = x_ref.shape
    heads = HG_HEADS
    dv = ng_ref.shape[-1]
    vdim = heads * dv
    kdim = (w_in_ref.shape[1] - 2 * vdim) // 2
    dk = kdim // heads
    m = nb * tt
    c = chunk
    n_chunks = m // c
    chunks_per_seq = tt // c
    mid = max(c // 2 - 1, 0)

    @pl.when(t == 0)
    def _():
        for h in range(heads):
            if has_state:
                for bi in range(nb):
                    st_refs[h][bi] = s0_ref[bi, h].T
            else:
                st_refs[h][...] = jnp.zeros_like(st_refs[h])

    x = x_ref[...].reshape(m, d)
    proj_ref[...] = _dot(x.astype(BF16), w_in_ref[...])

    lbp = lbp_ref[...]
    e = jnp.exp(lbp - jnp.max(lbp, axis=0, keepdims=True))
    sm = e / jnp.sum(e, axis=0, keepdims=True)
    lb = jnp.sum(sm[1:layer + 1], axis=0, keepdims=True)

    def gate_body(j, carry):
        rows = pl.ds(pl.multiple_of(j * c, c), c)
        f = lb + (1.0 - lb) * jax.nn.sigmoid(proj_ref[rows, kdim:2 * kdim])
        proj_ref[rows, kdim:2 * kdim] = 1.0 - f
        for p, part in enumerate(_split3(jnp.log(f))):
            lf_ref[p, rows, :] = part.astype(lf_ref.dtype)
        return carry

    lax.fori_loop(0, n_chunks, gate_body, 0)

    ri = lax.broadcasted_iota(jnp.int32, (m, m), 0)
    ci = lax.broadcasted_iota(jnp.int32, (m, m), 1)
    causal = jnp.logical_and(ri // c == ci // c, ri >= ci)
    tri = causal.astype(BF16)
    bcum_ref[...] = (_dot(tri, lf_ref[0].astype(BF16)) + _dot(tri, lf_ref[1].astype(BF16))
                     + _dot(tri, lf_ref[2].astype(BF16)))

    def decay_body(j, carry):
        rows = pl.ds(pl.multiple_of(j * c, c), c)
        bcum = bcum_ref[rows, :]
        bmid = bcum[mid:mid + 1]
        blast = bcum[c - 1:c]
        q_dec = proj_ref[rows, 0:kdim] * jnp.exp(bcum)
        k_rel = proj_ref[rows, kdim:2 * kdim] * jnp.exp(bmid - bcum)
        sdt = qrel_ref.dtype
        qdec_ref[rows, :] = q_dec.astype(sdt)
        qrel_ref[rows, :] = (q_dec * jnp.exp(-bmid)).astype(sdt)
        krel_ref[rows, :] = k_rel.astype(sdt)
        kend_ref[rows, :] = (k_rel * jnp.exp(blast - bmid)).astype(sdt)
        vb_ref[rows, :] = proj_ref[rows, 2 * kdim:2 * kdim + vdim].astype(sdt)
        dec_ref[pl.ds(j, 1), :] = jnp.exp(blast)
        return carry

    lax.fori_loop(0, n_chunks, decay_body, 0)

    for h in range(heads):
        sk = slice(h * dk, (h + 1) * dk)
        sv = slice(h * dv, (h + 1) * dv)
        a = lax.dot_general(qrel_ref[:, sk].astype(BF16), krel_ref[:, sk].astype(BF16), _NT,
                            preferred_element_type=F32)
        a = jnp.where(causal, a, 0.0).astype(BF16)
        oacc_ref[:, sv] = _dot(a, vb_ref[:, sv].astype(BF16))

    for j in range(n_chunks):
        rows = slice(j * c, (j + 1) * c)
        for h in range(heads):
            sk = slice(h * dk, (h + 1) * dk)
            sv = slice(h * dv, (h + 1) * dv)
            u_ref[j, h] = lax.dot_general(vb_ref[rows, sv].astype(BF16), kend_ref[rows, sk].astype(BF16), _TN,
                                          preferred_element_type=F32)

    for j in range(n_chunks):
        rows = slice(j * c, (j + 1) * c)
        bi = j // chunks_per_seq
        for h in range(heads):
            sk = slice(h * dk, (h + 1) * dk)
            sv = slice(h * dv, (h + 1) * dv)
            st = st_refs[h][bi]
            oacc_ref[rows, sv] += lax.dot_general(qdec_ref[rows, sk].astype(BF16), st.astype(BF16), _NT,
                                                  preferred_element_type=F32)
            st_refs[h][bi] = st * dec_ref[j:j + 1, sk] + u_ref[j, h]

    ng = ng_ref[...]
    for h in range(heads):
        sv = slice(h * dv, (h + 1) * dv)
        o = oacc_ref[:, sv]
        o = o * lax.rsqrt(jnp.mean(o * o, axis=-1, keepdims=True) + RMS_EPS) * ng
        gate = proj_ref[:, 2 * kdim + vdim + h * dv:2 * kdim + vdim + (h + 1) * dv]
        og_ref[:, sv] = (o * (gate * jax.nn.sigmoid(gate))).astype(BF16)
    out = _dot(og_ref[...], w_o_ref[...])
    o_ref[...] = _layer_norm(alpha * x + out, g_ref[...], b_ref[...]).reshape(nb, tt, d)

    @pl.when(t == pl.num_programs(1) - 1)
    def _():
        for bi in range(nb):
            for h in range(heads):
                sfin_ref[bi, h] = st_refs[h][bi].T


def _hgrn_tiled(x, s0, w_in, lb_param, norm_g, w_o, ln_g, ln_b, alpha, layer, chunk, nb, tt):
    b, t, d = x.shape
    heads = HG_HEADS
    dv = norm_g.shape[-1]
    vdim = heads * dv
    kdim = (w_in.shape[1] - 2 * vdim) // 2
    dk = kdim // heads
    m = nb * tt
    n_chunks = m // chunk
    has_state = s0 is not None
    const2 = lambda i, j: (0, 0)
    state_spec = pl.BlockSpec((nb, heads, dk, dv), lambda i, j: (i, 0, 0, 0))
    in_specs = [pl.BlockSpec((nb, tt, d), lambda i, j: (i, j, 0))]
    args = [x]
    if has_state:
        in_specs.append(state_spec)
        args.append(s0)
    in_specs += [pl.BlockSpec(w_in.shape, const2), pl.BlockSpec(lb_param.shape, const2),
                 pl.BlockSpec(norm_g.shape, const2), pl.BlockSpec(w_o.shape, const2),
                 pl.BlockSpec((1, d), const2), pl.BlockSpec((1, d), const2)]
    args += [w_in, lb_param, norm_g, w_o, ln_g, ln_b]
    sdt = BF16 if chunk % 16 == 0 else F32
    scratch = [pltpu.VMEM((m, w_in.shape[1]), F32),
               pltpu.VMEM((3, m, kdim), sdt),
               pltpu.VMEM((m, kdim), F32),
               pltpu.VMEM((m, kdim), sdt),
               pltpu.VMEM((m, kdim), sdt),
               pltpu.VMEM((m, kdim), sdt),
               pltpu.VMEM((m, kdim), sdt),
               pltpu.VMEM((m, vdim), sdt),
               pltpu.VMEM((n_chunks, kdim), F32),
               pltpu.VMEM((n_chunks, heads, dv, dk), F32),
               pltpu.VMEM((m, vdim), F32),
               pltpu.VMEM((m, vdim), BF16)]
    scratch += [pltpu.VMEM((nb, dv, dk), F32) for _ in range(heads)]
    return pl.pallas_call(
        functools.partial(_hgrn_tile_kernel, alpha=alpha, layer=layer, chunk=chunk, has_state=has_state),
        grid=(b // nb, t // tt),
        in_specs=in_specs,
        out_specs=[pl.BlockSpec((nb, tt, d), lambda i, j: (i, j, 0)), state_spec],
        out_shape=[jax.ShapeDtypeStruct(x.shape, F32),
                   jax.ShapeDtypeStruct((b, heads, dk, dv), F32)],
        scratch_shapes=scratch,
        compiler_params=_cparams("parallel", "arbitrary"),
        name="hgrn_state" if has_state else "hgrn_fresh",
    )(*args)


_I_E0, _I_E1, _I_G0, _I_G1, _I_R0, _I_R1 = range(6)


def _router_kernel(x_ref, w_ref, cnt0_ref, info_ref, cnt_ref, carry_ref):
    i = pl.program_id(0)
    tm = x_ref.shape[0]

    @pl.when(i == 0)
    def _():
        carry_ref[...] = cnt0_ref[...]

    logits = jnp.dot(x_ref[...], w_ref[...], preferred_element_type=F32, precision=lax.Precision.HIGHEST)
    lane = lax.broadcasted_iota(jnp.int32, (tm, LANES), 1)
    valid = lane < N_EXPERTS
    logits = jnp.where(valid, logits, -jnp.inf)
    p = jnp.exp(logits - jnp.max(logits, axis=-1, keepdims=True))
    p = p / jnp.sum(p, axis=-1, keepdims=True)
    p0 = jnp.where(valid, p, -1.0)
    v0 = jnp.max(p0, axis=-1, keepdims=True)
    e0 = jnp.min(jnp.where(p0 == v0, lane, LANES), axis=-1, keepdims=True)
    p1 = jnp.where(lane == e0, -1.0, p0)
    v1 = jnp.max(p1, axis=-1, keepdims=True)
    e1 = jnp.min(jnp.where(p1 == v1, lane, LANES), axis=-1, keepdims=True)
    den = v0 + v1
    sel = jnp.logical_or(lane == e0, lane == e1)
    ri = lax.broadcasted_iota(jnp.int32, (tm, tm), 0)
    ci = lax.broadcasted_iota(jnp.int32, (tm, tm), 1)
    before = (ri > ci).astype(BF16)
    rank = _dot(before, sel.astype(BF16)) + carry_ref[...]
    r0 = jnp.sum(jnp.where(lane == e0, rank, 0.0), axis=-1, keepdims=True)
    r1 = jnp.sum(jnp.where(lane == e1, rank, 0.0), axis=-1, keepdims=True)
    carry_ref[...] += jnp.sum(sel.astype(F32), axis=0, keepdims=True)
    info = jnp.zeros((tm, LANES), F32)
    for idx, val in ((_I_E0, e0.astype(F32)), (_I_E1, e1.astype(F32)), (_I_G0, v0 / den),
                     (_I_G1, v1 / den), (_I_R0, r0), (_I_R1, r1)):
        info = jnp.where(lane == idx, val, info)
    info_ref[...] = info
    cnt_ref[...] = carry_ref[...]


def _router(x2d, w_router_padded, counts_in, tm):
    n, d = x2d.shape
    return pl.pallas_call(
        _router_kernel,
        grid=(n // tm,),
        in_specs=[pl.BlockSpec((tm, d), lambda i: (i, 0)),
                  pl.BlockSpec((d, LANES), lambda i: (0, 0)),
                  pl.BlockSpec((1, LANES), lambda i: (0, 0))],
        out_specs=[pl.BlockSpec((tm, LANES), lambda i: (i, 0)),
                   pl.BlockSpec((1, LANES), lambda i: (0, 0))],
        out_shape=[jax.ShapeDtypeStruct((n, LANES), F32), jax.ShapeDtypeStruct((1, LANES), F32)],
        scratch_shapes=[pltpu.VMEM((1, LANES), F32)],
        compiler_params=_cparams("arbitrary"),
        name="moe_router",
    )(x2d, w_router_padded, counts_in)


def _row_copy(src, src_row, dst, dst_row, sem):
    return pltpu.make_async_copy(src.at[pl.ds(src_row, 1)], dst.at[pl.ds(dst_row, 1)], sem)


def _scatter_rows(pos_ref, x_ref, out_hbm, sem):
    tb = x_ref.shape[0]

    def issue(j, carry):
        for kk in range(TOP_K):
            _row_copy(x_ref, j, out_hbm, pos_ref[TOP_K * j + kk], sem).start()
        return carry

    lax.fori_loop(0, tb, issue, 0, unroll=4)
    for _ in range(TOP_K):
        pltpu.make_async_copy(x_ref, out_hbm.at[pl.ds(0, tb)], sem).wait()


def _dispatch_kernel(pos_ref, fill_ref, xa_ref, xb_ref, out_hbm, zero_ref, sem, zsem, *, steps_a):
    i = pl.program_id(0)

    @pl.when(i == 0)
    def _():
        zero_ref[...] = jnp.zeros_like(zero_ref)
        tz = zero_ref.shape[0]

        def fill(e):
            row = pl.multiple_of(jnp.maximum(fill_ref[e], 0), tz)
            return pltpu.make_async_copy(zero_ref, out_hbm.at[pl.ds(row, tz)], zsem)

        for e in range(fill_ref.shape[0]):
            pl.when(fill_ref[e] >= 0)(lambda e=e: fill(e).start())
        for e in range(fill_ref.shape[0]):
            pl.when(fill_ref[e] >= 0)(lambda e=e: fill(e).wait())

    pl.when(i < steps_a)(lambda: _scatter_rows(pos_ref, xa_ref, out_hbm, sem))
    pl.when(i >= steps_a)(lambda: _scatter_rows(pos_ref, xb_ref, out_hbm, sem))


def _dispatch(xa, xb, pos_flat, fill_rows, n_rows, tz):
    d = xa.shape[1]
    tb = min(1024, xa.shape[0], xb.shape[0])
    steps_a = xa.shape[0] // tb
    steps_b = xb.shape[0] // tb
    return pl.pallas_call(
        functools.partial(_dispatch_kernel, steps_a=steps_a),
        grid=(steps_a + steps_b,),
        in_specs=[pl.BlockSpec((TOP_K * tb,), lambda i: (i,), memory_space=pltpu.SMEM),
                  pl.BlockSpec(memory_space=pltpu.SMEM),
                  pl.BlockSpec((tb, d), lambda i: (jnp.minimum(i, steps_a - 1), 0)),
                  pl.BlockSpec((tb, d), lambda i: (jnp.maximum(i - steps_a, 0), 0))],
        out_specs=pl.BlockSpec(memory_space=pl.ANY),
        out_shape=jax.ShapeDtypeStruct((n_rows, d), xa.dtype),
        scratch_shapes=[pltpu.VMEM((tz, d), xa.dtype), pltpu.SemaphoreType.DMA(()),
                        pltpu.SemaphoreType.DMA(())],
        compiler_params=_cparams("arbitrary"),
        name="moe_dispatch",
    )(pos_flat, fill_rows, xa, xb)


def _combine_kernel(pos_ref, posn_ref, x_ref, info_ref, y_hbm, g_ref, b_ref, o_ref, ybuf, sem, *, alpha):
    i = pl.program_id(0)
    n = pl.num_programs(0)
    tm = x_ref.shape[0]

    def issue(p_ref, slot):
        def body(j, carry):
            for kk in range(TOP_K):
                _row_copy(y_hbm, p_ref[TOP_K * j + kk], ybuf.at[slot, kk], j, sem.at[slot]).start()
            return carry
        lax.fori_loop(0, tm, body, 0, unroll=8)

    @pl.when(i == 0)
    def _():
        issue(pos_ref, 0)

    @pl.when(i + 1 < n)
    def _():
        issue(posn_ref, (i + 1) % 2)

    slot = i % 2
    for kk in range(TOP_K):
        pltpu.make_async_copy(y_hbm.at[pl.ds(0, tm)], ybuf.at[slot, kk], sem.at[slot]).wait()
    info = info_ref[...]
    moe = info[:, _I_G0:_I_G0 + 1] * ybuf[slot, 0] + info[:, _I_G1:_I_G1 + 1] * ybuf[slot, 1]
    o_ref[...] = _layer_norm(alpha * x_ref[...] + moe, g_ref[...], b_ref[...])


def _combine_ln(x2d, info, pos_flat, y, ln_g, ln_b, alpha, tm):
    n, d = x2d.shape
    nt = n // tm
    vec = pl.BlockSpec((1, d), lambda i: (0, 0))
    return pl.pallas_call(
        functools.partial(_combine_kernel, alpha=alpha),
        grid=(nt,),
        in_specs=[pl.BlockSpec((TOP_K * tm,), lambda i: (i,), memory_space=pltpu.SMEM),
                  pl.BlockSpec((TOP_K * tm,), lambda i: (jnp.minimum(i + 1, nt - 1),), memory_space=pltpu.SMEM),
                  pl.BlockSpec((tm, d), lambda i: (i, 0)),
                  pl.BlockSpec((tm, LANES), lambda i: (i, 0)),
                  pl.BlockSpec(memory_space=pl.ANY),
                  vec, vec],
        out_specs=pl.BlockSpec((tm, d), lambda i: (i, 0)),
        out_shape=jax.ShapeDtypeStruct((n, d), F32),
        scratch_shapes=[pltpu.VMEM((2, TOP_K, tm, d), F32), pltpu.SemaphoreType.DMA((2,))],
        compiler_params=_cparams("arbitrary"),
        name="moe_combine",
    )(pos_flat, pos_flat, x2d, info, y, ln_g, ln_b)


def _moe_ln(xs2d, w_router, w_gu, w_d, ln_g, ln_b, alpha, tm_expert):
    d = xs2d[0].shape[1]
    n_total = sum(x.shape[0] for x in xs2d)
    w_r = jnp.pad(w_router, ((0, 0), (0, LANES - N_EXPERTS)))
    counts = jnp.zeros((1, LANES), F32)
    infos = []
    for x in xs2d:
        info, counts = _router(x, w_r, counts, min(x.shape[0], 1024))
        infos.append(info)
    cnt = counts[0, :N_EXPERTS].astype(jnp.int32)
    padded = (cnt + tm_expert - 1) // tm_expert * tm_expert
    ends = jnp.cumsum(padded)
    offs = ends - padded
    n_rows = -(-(TOP_K * n_total) // tm_expert) * tm_expert + N_EXPERTS * tm_expert
    tile_start = jnp.arange(n_rows // tm_expert, dtype=jnp.int32) * tm_expert
    tile_eid = jnp.minimum(jnp.sum(tile_start[:, None] >= ends[None, :], axis=1), N_EXPERTS - 1).astype(jnp.int32)
    tile_used = (tile_start < offs[tile_eid] + cnt[tile_eid]).astype(jnp.int32)
    tail = ends[-1] + jnp.arange(N_EXPERTS, dtype=jnp.int32) * tm_expert
    fill_rows = jnp.concatenate([jnp.where(padded > 0, ends - tm_expert, -1),
                                 jnp.where(tail < n_rows, tail, -1)]).astype(jnp.int32)
    poss = []
    for info in infos:
        eid = info[:, _I_E0:_I_E1 + 1].astype(jnp.int32)
        rank = info[:, _I_R0:_I_R1 + 1].astype(jnp.int32)
        poss.append((offs[eid] + rank).reshape(-1))
    xa, xb = xs2d
    buf = _dispatch(xa, xb, jnp.concatenate(poss), fill_rows, n_rows, tm_expert)
    y = _expert_ffn(buf, tile_eid, tile_used, w_gu, w_d, tm_expert)
    return [_combine_ln(x, info, pos, y, ln_g, ln_b, alpha, min(x.shape[0], 512))
            for x, info, pos in zip(xs2d, infos, poss)]


def kernel(x_prompt, x_sample, mem_prompt, state_pool, state_hgrn, cache_mem_k, cache_mem_v, pool_w, pool_scale, hg_w_in, hg_lb, hg_norm_g, hg_w_o, xa_w_q, xa_w_kv, xa_w_o, ffn_w_gu, ffn_w_d, moe_w_router, moe_w_gu, moe_w_d, ln_g, ln_b):
    b, t, d = x_prompt.shape
    sb, st, _ = x_sample.shape
    depth = ln_g.shape[0]
    mem_len = mem_prompt.shape[1]
    alpha = (2 * depth) ** 0.25
    n_prev = state_pool.shape[2]

    pool_w_b = pool_w.astype(BF16)
    hg_w_in_b = hg_w_in.astype(BF16)
    hg_w_o_b = hg_w_o.astype(BF16)
    xa_w_q_b = xa_w_q.astype(BF16)
    xa_w_kv_b = xa_w_kv.astype(BF16)
    xa_w_o_b = xa_w_o.astype(BF16)
    ffn_w_gu_b = ffn_w_gu.astype(BF16)
    ffn_w_d_b = ffn_w_d.astype(BF16)
    moe_w_gu_b = moe_w_gu.astype(BF16)
    moe_w_d_b = moe_w_d.astype(BF16)

    def vec(a):
        return a.reshape(1, -1)

    mk, mv = _mem_kv(mem_prompt.reshape(b * mem_len, d), xa_w_kv_b)
    mk4 = mk.reshape(depth, b, mem_len, d)
    mv4 = mv.reshape(depth, b, mem_len, d)

    def stored_order(c):
        halves = c.shape[4] // LANES
        c = c.reshape(depth, sb, mem_len, XA_HEADS, halves, LANES).transpose(0, 1, 2, 4, 3, 5)
        return c.reshape(depth, sb, mem_len * halves * XA_HEADS, LANES)

    ck8 = stored_order(cache_mem_k)
    cv8 = stored_order(cache_mem_v)

    tt_attn = min(t, 512)
    nb_attn = min(sb, 8)

    def attend(xp, xs, layer):
        args = (xa_w_q_b[layer], xa_w_o_b[layer], vec(ln_g[layer, 1]), vec(ln_b[layer, 1]), alpha)
        return (_mem_attend(xp, mk4, mv4, layer, *args, 1, tt_attn),
                _mem_attend_cache(xs, ck8, cv8, layer, *args, nb_attn))

    pool_args = (pool_w_b[0], vec(pool_scale[0]), vec(ln_g[0, 0]), vec(ln_b[0, 0]), alpha)
    xp = _pool_prompt(x_prompt, *pool_args)
    prev = jnp.pad(state_pool[0], ((0, 0), (POOL_HALO - n_prev, 0), (0, 0)))
    xs = _pool_sample(x_sample, prev, n_prev, *pool_args)
    xp, xs = attend(xp, xs, 0)
    ffn_args = (ffn_w_gu_b, ffn_w_d_b, vec(ln_g[0, 2]), vec(ln_b[0, 2]), alpha)
    xp = _ffn_ln(xp.reshape(b * t, d), *ffn_args, 512).reshape(b, t, d)
    xs = _ffn_ln(xs.reshape(sb * st, d), *ffn_args, 512).reshape(sb, st, d)

    hg_args = (hg_w_in_b[0], hg_lb, vec(hg_norm_g[0]), hg_w_o_b[0], vec(ln_g[1, 0]), vec(ln_b[1, 0]), alpha, 1)
    xp, hg_p = _hgrn_tiled(xp, None, *hg_args, HG_CHUNK, 1, min(t, 256))
    xs, hg_s = _hgrn_tiled(xs, state_hgrn[0], *hg_args, math.gcd(st, HG_CHUNK), min(sb, 8), st)
    xp, xs = attend(xp, xs, 1)
    yp, ys = _moe_ln([xp.reshape(b * t, d), xs.reshape(sb * st, d)], moe_w_router[0], moe_w_gu_b[0],
                     moe_w_d_b[0], vec(ln_g[1, 2]), vec(ln_b[1, 2]), alpha, 256)

    heads_hd = (depth, b, mem_len, XA_HEADS, d // XA_HEADS)
    pool_p = x_prompt[:, t - n_prev:, :][None]
    pool_s = jnp.concatenate([state_pool[0], x_sample], axis=1)[:, st:, :][None]
    return (yp.reshape(b, t, d), ys.reshape(sb, st, d), pool_p, hg_p[None],
            mk.reshape(heads_hd), mv.reshape(heads_hd), pool_s, hg_s[None])
```

```python
import functools
import math

import jax
import jax.numpy as jnp
from jax import lax
from jax.experimental import pallas as pl
from jax.experimental.pallas import tpu as pltpu

F32 = jnp.float32
BF16 = jnp.bfloat16

POOL_WINDOWS = (2, 4, 8, 16)
POOL_HALO = 16
HG_HEADS = 8
HG_CHUNK = 32
RMS_EPS = 1e-6
XA_HEADS = 4
N_EXPERTS = 8
TOP_K = 2
LN_EPS = 1e-5
LANES = 128
VMEM_LIMIT = 56 * 1024 * 1024

_NT = (((1,), (1,)), ((), ()))
_TN = (((0,), (0,)), ((), ()))


def _cparams(*sem):
    return pltpu.CompilerParams(dimension_semantics=sem, vmem_limit_bytes=VMEM_LIMIT)


def _layer_norm(z, g, b):
    mu = jnp.mean(z, axis=-1, keepdims=True)
    zc = z - mu
    var = jnp.mean(zc * zc, axis=-1, keepdims=True)
    return zc * lax.rsqrt(var + LN_EPS) * g + b


def _dot(a, b):
    return jnp.dot(a, b, preferred_element_type=F32)


def _kv_kernel(mem_ref, w_ref, k_ref, v_ref):
    kv = _dot(mem_ref[...].astype(BF16), w_ref[0])
    n = k_ref.shape[-1]
    k_ref[0] = kv[:, :n]
    v_ref[0] = kv[:, n:]


def _mem_kv(mem2d, w_kv):
    rows, d = mem2d.shape
    depth, _, e2 = w_kv.shape
    e = e2 // 2
    tm = min(rows, 512)
    out = jax.ShapeDtypeStruct((depth, rows, e), F32)
    return pl.pallas_call(
        _kv_kernel,
        grid=(depth, rows // tm),
        in_specs=[pl.BlockSpec((tm, d), lambda l, i: (i, 0)),
                  pl.BlockSpec((1, d, e2), lambda l, i: (l, 0, 0))],
        out_specs=[pl.BlockSpec((1, tm, e), lambda l, i: (l, i, 0)),
                   pl.BlockSpec((1, tm, e), lambda l, i: (l, i, 0))],
        out_shape=[out, out],
        compiler_params=_cparams("parallel", "parallel"),
        name="mem_kv",
    )(mem2d, w_kv)


def _window_sums(a, w):
    s, k = a, 1
    while k < w:
        s = s + pltpu.roll(s, k, 0)
        k *= 2
    return s


def _pool_mix(pooled_minus_tok, w_ref, scale):
    parts = [_dot(pooled_minus_tok[g].astype(BF16), w_ref[g]) for g in range(len(POOL_WINDOWS))]
    return jnp.concatenate(parts, axis=1) * scale


def _pool_prompt_kernel(x_ref, halo_ref, w_ref, scale_ref, g_ref, b_ref, o_ref, buf_ref, *, alpha):
    t = pl.program_id(1)
    tt = x_ref.shape[1]
    gw = w_ref.shape[-1]
    x = x_ref[0]
    buf_ref[0:POOL_HALO, :] = jnp.where(t > 0, halo_ref[0], 0.0)
    buf_ref[POOL_HALO:, :] = x
    pos = t * tt + lax.broadcasted_iota(jnp.int32, (tt, 1), 0)
    diffs = []
    for g, w in enumerate(POOL_WINDOWS):
        sl = slice(g * gw, (g + 1) * gw)
        s = _window_sums(buf_ref[:, sl], w)[POOL_HALO:, :]
        cnt = jnp.minimum(pos + 1, w).astype(F32)
        diffs.append(s / cnt - x[:, sl])
    mixed = _pool_mix(diffs, w_ref, scale_ref[...])
    o_ref[0] = _layer_norm(alpha * x + mixed, g_ref[...], b_ref[...])


def _pool_prompt(x, pool_w, scale, ln_g, ln_b, alpha):
    b, t, d = x.shape
    tt = min(t, 512)
    hb = tt // POOL_HALO
    vec = pl.BlockSpec((1, d), lambda i, j: (0, 0))
    return pl.pallas_call(
        functools.partial(_pool_prompt_kernel, alpha=alpha),
        grid=(b, t // tt),
        in_specs=[pl.BlockSpec((1, tt, d), lambda i, j: (i, j, 0)),
                  pl.BlockSpec((1, POOL_HALO, d), lambda i, j: (i, jnp.maximum(j * hb - 1, 0), 0)),
                  pl.BlockSpec(pool_w.shape, lambda i, j: (0, 0, 0)),
                  vec, vec, vec],
        out_specs=pl.BlockSpec((1, tt, d), lambda i, j: (i, j, 0)),
        out_shape=jax.ShapeDtypeStruct(x.shape, F32),
        scratch_shapes=[pltpu.VMEM((tt + POOL_HALO, d), F32)],
        compiler_params=_cparams("parallel", "parallel"),
        name="pool_prompt",
    )(x, x, pool_w, scale, ln_g, ln_b)


def _pool_sample_kernel(x_ref, prev_ref, w_ref, scale_ref, g_ref, b_ref, o_ref, buf_ref, *, alpha, n_prev):
    nb, t, d = x_ref.shape
    gw = w_ref.shape[-1]
    rows = POOL_HALO + t
    buf_ref[:, 0:POOL_HALO, :] = prev_ref[...]
    buf_ref[:, POOL_HALO:, :] = x_ref[...]
    x = x_ref[...].reshape(nb * t, d)
    pos = lax.broadcasted_iota(jnp.int32, (nb, t, 1), 1).reshape(nb * t, 1)
    diffs = []
    for g, w in enumerate(POOL_WINDOWS):
        sl = slice(g * gw, (g + 1) * gw)
        s = _window_sums(buf_ref[:, :, sl].reshape(nb * rows, gw), w)
        s = s.reshape(nb, rows, gw)[:, POOL_HALO:, :].reshape(nb * t, gw)
        cnt = jnp.minimum(pos + 1 + n_prev, w).astype(F32)
        diffs.append(s / cnt - x[:, sl])
    mixed = _pool_mix(diffs, w_ref, scale_ref[...])
    o_ref[...] = _layer_norm(alpha * x + mixed, g_ref[...], b_ref[...]).reshape(nb, t, d)


def _pool_sample(x, prev_padded, n_prev, pool_w, scale, ln_g, ln_b, alpha):
    b, t, d = x.shape
    nb = min(b, 16)
    vec = pl.BlockSpec((1, d), lambda i: (0, 0))
    return pl.pallas_call(
        functools.partial(_pool_sample_kernel, alpha=alpha, n_prev=n_prev),
        grid=(b // nb,),
        in_specs=[pl.BlockSpec((nb, t, d), lambda i: (i, 0, 0)),
                  pl.BlockSpec((nb, POOL_HALO, d), lambda i: (i, 0, 0)),
                  pl.BlockSpec(pool_w.shape, lambda i: (0, 0, 0)),
                  vec, vec, vec],
        out_specs=pl.BlockSpec((nb, t, d), lambda i: (i, 0, 0)),
        out_shape=jax.ShapeDtypeStruct(x.shape, F32),
        scratch_shapes=[pltpu.VMEM((nb, POOL_HALO + t, d), F32)],
        compiler_params=_cparams("parallel"),
        name="pool_sample",
    )(x, prev_padded, pool_w, scale, ln_g, ln_b)


def _attn_kernel(x_ref, k_ref, v_ref, wq_ref, wo_ref, g_ref, b_ref, o_ref, oh_ref, s_ref, p_ref, *, alpha):
    nb, tt, d = x_ref.shape
    hd = d // XA_HEADS
    scale = hd ** -0.5
    x = x_ref[...].reshape(nb * tt, d)
    q = _dot(x.astype(BF16), wq_ref[...])
    for h in range(XA_HEADS):
        sl = slice(h * hd, (h + 1) * hd)
        qh = q[:, sl].astype(BF16).reshape(nb, tt, hd)
        kh = k_ref[:, :, sl].astype(BF16)
        s_ref[h] = jnp.einsum("btd,bmd->btm", qh, kh, preferred_element_type=F32) * scale
    for h in range(XA_HEADS):
        s = s_ref[h]
        p = jnp.exp(s - jnp.max(s, axis=-1, keepdims=True))
        p_ref[h] = (p / jnp.sum(p, axis=-1, keepdims=True)).astype(BF16)
    for h in range(XA_HEADS):
        sl = slice(h * hd, (h + 1) * hd)
        vh = v_ref[:, :, sl].astype(BF16)
        oh = jnp.einsum("btm,bmd->btd", p_ref[h], vh, preferred_element_type=F32)
        oh_ref[:, sl] = oh.reshape(nb * tt, hd).astype(BF16)
    a = _dot(oh_ref[...], wo_ref[...])
    o_ref[...] = _layer_norm(alpha * x + a, g_ref[...], b_ref[...]).reshape(nb, tt, d)


def _mem_attend(x, k4, v4, layer, wq, wo, ln_g, ln_b, alpha, nb, tt):
    b, t, d = x.shape
    m = k4.shape[2]
    vec = pl.BlockSpec((1, d), lambda i, j: (0, 0))
    mat = pl.BlockSpec((d, d), lambda i, j: (0, 0))
    kv = pl.BlockSpec((None, nb, m, d), lambda i, j: (layer, i, 0, 0))
    return pl.pallas_call(
        functools.partial(_attn_kernel, alpha=alpha),
        grid=(b // nb, t // tt),
        in_specs=[pl.BlockSpec((nb, tt, d), lambda i, j: (i, j, 0)), kv, kv, mat, mat, vec, vec],
        out_specs=pl.BlockSpec((nb, tt, d), lambda i, j: (i, j, 0)),
        out_shape=jax.ShapeDtypeStruct(x.shape, F32),
        scratch_shapes=[pltpu.VMEM((nb * tt, d), BF16), pltpu.VMEM((XA_HEADS, nb, tt, m), F32),
                        pltpu.VMEM((XA_HEADS, nb, tt, m), BF16)],
        compiler_params=_cparams("parallel", "parallel"),
        name=f"mem_attend_l{layer}",
    )(x, k4, v4, wq, wo, ln_g, ln_b)


def _attn_cache_kernel(x_ref, k_ref, v_ref, wq_ref, wo_ref, g_ref, b_ref, o_ref, q_scr, oh_scr, s_scr, p_scr,
                       *, alpha):
    nb, t, d = x_ref.shape
    heads = XA_HEADS
    hd = d // heads
    halves = hd // LANES
    assert halves == 2
    rows = k_ref.shape[1]
    qr = heads * t
    scale = hd ** -0.5
    x = x_ref[...].reshape(nb * t, d)
    q = _dot(x.astype(BF16), wq_ref[...])
    for c in range(halves):
        for h in range(heads):
            col = (h * halves + c) * LANES
            q_scr[c, :, h * t:(h + 1) * t, :] = q[:, col:col + LANES].reshape(nb, t, LANES)

    col_id = lax.broadcasted_iota(jnp.int32, (qr, rows), 1)
    row_head = lax.broadcasted_iota(jnp.int32, (qr, rows), 0) // t
    col_half = (col_id // heads) % halves
    col_head = col_id % heads
    keep = jnp.logical_and(col_half == 1, col_head == row_head)

    for b in range(nb):
        kb = k_ref[b].astype(BF16)
        g0 = lax.dot_general(q_scr[0, b].astype(BF16), kb, _NT, preferred_element_type=F32)
        g1 = lax.dot_general(q_scr[1, b].astype(BF16), kb, _NT, preferred_element_type=F32)
        s_scr[b] = jnp.where(col_half == 0, g0, g1)
    for b in range(nb):
        part = s_scr[b]
        s = (part + pltpu.roll(part, heads, 1)) * scale
        s = jnp.where(keep, s, -jnp.inf)
        p = jnp.exp(s - jnp.max(s, axis=-1, keepdims=True))
        p = p / jnp.sum(p, axis=-1, keepdims=True)
        p_scr[0, b] = pltpu.roll(p, rows - heads, 1).astype(BF16)
        p_scr[1, b] = p.astype(BF16)
    for b in range(nb):
        vb = v_ref[b].astype(BF16)
        for c in range(halves):
            oc = _dot(p_scr[c, b], vb)
            for h in range(heads):
                col = (h * halves + c) * LANES
                oh_scr[b * t:(b + 1) * t, col:col + LANES] = oc[h * t:(h + 1) * t, :]
    a = _dot(oh_scr[...].astype(BF16), wo_ref[...])
    o_ref[...] = _layer_norm(alpha * x + a, g_ref[...], b_ref[...]).reshape(nb, t, d)


def _mem_attend_cache(x, k8, v8, layer, wq, wo, ln_g, ln_b, alpha, nb):
    b, t, d = x.shape
    rows = k8.shape[2]
    vec = pl.BlockSpec((1, d), lambda i: (0, 0))
    mat = pl.BlockSpec((d, d), lambda i: (0, 0))
    kv = pl.BlockSpec((None, nb, rows, LANES), lambda i: (layer, i, 0, 0))
    return pl.pallas_call(
        functools.partial(_attn_cache_kernel, alpha=alpha),
        grid=(b // nb,),
        in_specs=[pl.BlockSpec((nb, t, d), lambda i: (i, 0, 0)), kv, kv, mat, mat, vec, vec],
        out_specs=pl.BlockSpec((nb, t, d), lambda i: (i, 0, 0)),
        out_shape=jax.ShapeDtypeStruct(x.shape, F32),
        scratch_shapes=[pltpu.VMEM((2, nb, XA_HEADS * t, LANES), F32), pltpu.VMEM((nb * t, d), F32),
                        pltpu.VMEM((nb, XA_HEADS * t, rows), F32), pltpu.VMEM((2, nb, XA_HEADS * t, rows), BF16)],
        compiler_params=_cparams("parallel"),
        name=f"mem_attend_cache_l{layer}",
    )(x, k8, v8, wq, wo, ln_g, ln_b)


FFN_SUB = 512


def _swiglu(x_ref, wg_ref, wu_ref, wd_ref, h_ref):
    xb = x_ref[...].astype(BF16)
    f = h_ref.shape[1]
    for s in range(f // FFN_SUB):
        sl = slice(s * FFN_SUB, (s + 1) * FFN_SUB)
        gate = _dot(xb, wg_ref[0, :, sl])
        up = _dot(xb, wu_ref[0, :, sl])
        h_ref[:, sl] = (gate * jax.nn.sigmoid(gate) * up).astype(BF16)
    return _dot(h_ref[...], wd_ref[0])


def _ffn_ln_kernel(x_ref, wg_ref, wu_ref, wd_ref, g_ref, b_ref, o_ref, h_ref, *, alpha):
    y = _swiglu(x_ref, wg_ref, wu_ref, wd_ref, h_ref)
    o_ref[...] = _layer_norm(alpha * x_ref[...] + y, g_ref[...], b_ref[...])


def _ffn_ln(x2d, w_gu, w_d, ln_g, ln_b, alpha, tm):
    n, d = x2d.shape
    tm = min(tm, n)
    f = w_d.shape[1]
    vec = pl.BlockSpec((1, d), lambda i: (0, 0))
    once = pl.Buffered(1)
    return pl.pallas_call(
        functools.partial(_ffn_ln_kernel, alpha=alpha),
        grid=(n // tm,),
        in_specs=[pl.BlockSpec((tm, d), lambda i: (i, 0)),
                  pl.BlockSpec((1, d, f), lambda i: (0, 0, 0), pipeline_mode=once),
                  pl.BlockSpec((1, d, f), lambda i: (0, 0, 1), pipeline_mode=once),
                  pl.BlockSpec((1, f, d), lambda i: (0, 0, 0), pipeline_mode=once),
                  vec, vec],
        out_specs=pl.BlockSpec((tm, d), lambda i: (i, 0)),
        out_shape=jax.ShapeDtypeStruct((n, d), F32),
        scratch_shapes=[pltpu.VMEM((tm, f), BF16)],
        compiler_params=_cparams("parallel"),
        name="ffn_dense",
    )(x2d, w_gu, w_gu, w_d, ln_g, ln_b)


def _expert_kernel(eid_ref, used_ref, x_ref, wg_ref, wu_ref, wd_ref, o_ref, h_ref):
    i = pl.program_id(0)

    @pl.when(used_ref[i] > 0)
    def _():
        o_ref[...] = _swiglu(x_ref, wg_ref, wu_ref, wd_ref, h_ref)

    @pl.when(used_ref[i] == 0)
    def _():
        o_ref[...] = jnp.zeros_like(o_ref)


def _expert_ffn(xs, tile_eid, tile_used, w_gu, w_d, tm):
    r, d = xs.shape
    f = w_d.shape[1]
    grid_spec = pltpu.PrefetchScalarGridSpec(
        num_scalar_prefetch=2,
        grid=(r // tm,),
        in_specs=[pl.BlockSpec((tm, d), lambda i, e, u: (i, 0)),
                  pl.BlockSpec((1, d, f), lambda i, e, u: (e[i], 0, 0)),
                  pl.BlockSpec((1, d, f), lambda i, e, u: (e[i], 0, 1)),
                  pl.BlockSpec((1, f, d), lambda i, e, u: (e[i], 0, 0))],
        out_specs=pl.BlockSpec((tm, d), lambda i, e, u: (i, 0)),
        scratch_shapes=[pltpu.VMEM((tm, f), BF16)],
    )
    return pl.pallas_call(
        _expert_kernel,
        grid_spec=grid_spec,
        out_shape=jax.ShapeDtypeStruct((r, d), F32),
        compiler_params=_cparams("arbitrary"),
        name="ffn_experts",
    )(tile_eid, tile_used, xs, w_gu, w_gu, w_d)


def _split3(x):
    hi = x.astype(BF16).astype(F32)
    r = x - hi
    mid = r.astype(BF16).astype(F32)
    lo = (r - mid).astype(BF16).astype(F32)
    return hi, mid, lo


def _hgrn_tile_kernel(*refs, alpha, layer, chunk, has_state):
    st_refs = refs[-HG_HEADS:]
    refs = refs[:-HG_HEADS]
    s0_ref = None
    if has_state:
        s0_ref, refs = refs[1], refs[:1] + refs[2:]
    (x_ref, w_in_ref, lbp_ref, ng_ref, w_o_ref, g_ref, b_ref, o_ref, sfin_ref,
     proj_ref, lf_ref, bcum_ref, qrel_ref, krel_ref, kend_ref, qdec_ref, vb_ref, dec_ref,
     u_ref, oacc_ref, og_ref) = refs
    t = pl.program_id(1)
    nb, tt, d = x_ref.shape
    heads = HG_HEADS
    dv = ng_ref.shape[-1]
    vdim = heads * dv
    kdim = (w_in_ref.shape[1] - 2 * vdim) // 2
    dk = kdim // heads
    m = nb * tt
    c = chunk
    n_chunks = m // c
    chunks_per_seq = tt // c
    mid = max(c // 2 - 1, 0)

    @pl.when(t == 0)
    def _():
        for h in range(heads):
            if has_state:
                for bi in range(nb):
                    st_refs[h][bi] = s0_ref[bi, h].T
            else:
                st_refs[h][...] = jnp.zeros_like(st_refs[h])

    x = x_ref[...].reshape(m, d)
    xb = x.astype(BF16)

    lbp = lbp_ref[...]
    e = jnp.exp(lbp - jnp.max(lbp, axis=0, keepdims=True))
    sm = e / jnp.sum(e, axis=0, keepdims=True)
    lb = jnp.sum(sm[1:layer + 1], axis=0, keepdims=True)

    f_cols = slice(kdim, 2 * kdim)
    proj_ref[:, f_cols] = _dot(xb, w_in_ref[:, f_cols])
    for j in range(n_chunks):
        rows = slice(j * c, (j + 1) * c)
        f = lb + (1.0 - lb) * jax.nn.sigmoid(proj_ref[rows, f_cols])
        proj_ref[rows, f_cols] = 1.0 - f
        for p, part in enumerate(_split3(jnp.log(f))):
            lf_ref[p, rows, :] = part.astype(lf_ref.dtype)
    proj_ref[:, 0:kdim] = _dot(xb, w_in_ref[:, 0:kdim])
    proj_ref[:, 2 * kdim:] = _dot(xb, w_in_ref[:, 2 * kdim:])

    ri = lax.broadcasted_iota(jnp.int32, (m, m), 0)
    ci = lax.broadcasted_iota(jnp.int32, (m, m), 1)
    causal = jnp.logical_and(ri // c == ci // c, ri >= ci)
    tri = causal.astype(BF16)
    bcum_ref[...] = (_dot(tri, lf_ref[0].astype(BF16)) + _dot(tri, lf_ref[1].astype(BF16))
                     + _dot(tri, lf_ref[2].astype(BF16)))

    sdt = qrel_ref.dtype
    vb_ref[...] = proj_ref[:, 2 * kdim:2 * kdim + vdim].astype(sdt)
    for j in range(n_chunks):
        rows = slice(j * c, (j + 1) * c)
        bcum = bcum_ref[rows, :]
        bmid = bcum[mid:mid + 1]
        blast = bcum[c - 1:c]
        q_dec = proj_ref[rows, 0:kdim] * jnp.exp(bcum)
        k_rel = proj_ref[rows, f_cols] * jnp.exp(bmid - bcum)
        qdec_ref[rows, :] = q_dec.astype(sdt)
        qrel_ref[rows, :] = (q_dec * jnp.exp(-bmid)).astype(sdt)
        krel_ref[rows, :] = k_rel.astype(sdt)
        kend_ref[rows, :] = (k_rel * jnp.exp(blast - bmid)).astype(sdt)
        dec_ref[j:j + 1, :] = jnp.exp(blast)

    for h in range(heads):
        sk = slice(h * dk, (h + 1) * dk)
        sv = slice(h * dv, (h + 1) * dv)
        a = lax.dot_general(qrel_ref[:, sk].astype(BF16), krel_ref[:, sk].astype(BF16), _NT,
                            preferred_element_type=F32)
        a = jnp.where(causal, a, 0.0).astype(BF16)
        oacc_ref[:, sv] = _dot(a, vb_ref[:, sv].astype(BF16))

    for j in range(n_chunks):
        rows = slice(j * c, (j + 1) * c)
        for h in range(heads):
            sk = slice(h * dk, (h + 1) * dk)
            sv = slice(h * dv, (h + 1) * dv)
            u_ref[j, h] = lax.dot_general(vb_ref[rows, sv].astype(BF16), kend_ref[rows, sk].astype(BF16), _TN,
                                          preferred_element_type=F32)

    for j in range(n_chunks):
        rows = slice(j * c, (j + 1) * c)
        bi = j // chunks_per_seq
        for h in range(heads):
            sk = slice(h * dk, (h + 1) * dk)
            sv = slice(h * dv, (h + 1) * dv)
            st = st_refs[h][bi]
            oacc_ref[rows, sv] += lax.dot_general(qdec_ref[rows, sk].astype(BF16), st.astype(BF16), _NT,
                                                  preferred_element_type=F32)
            st_refs[h][bi] = st * dec_ref[j:j + 1, sk] + u_ref[j, h]

    ng = ng_ref[...]
    for h in range(heads):
        sv = slice(h * dv, (h + 1) * dv)
        o = oacc_ref[:, sv]
        o = o * lax.rsqrt(jnp.mean(o * o, axis=-1, keepdims=True) + RMS_EPS) * ng
        gate = proj_ref[:, 2 * kdim + vdim + h * dv:2 * kdim + vdim + (h + 1) * dv]
        og_ref[:, sv] = (o * (gate * jax.nn.sigmoid(gate))).astype(BF16)
    out = _dot(og_ref[...], w_o_ref[...])
    o_ref[...] = _layer_norm(alpha * x + out, g_ref[...], b_ref[...]).reshape(nb, tt, d)

    @pl.when(t == pl.num_programs(1) - 1)
    def _():
        for bi in range(nb):
            for h in range(heads):
                sfin_ref[bi, h] = st_refs[h][bi].T


def _hgrn_tiled(x, s0, w_in, lb_param, norm_g, w_o, ln_g, ln_b, alpha, layer, chunk, nb, tt):
    b, t, d = x.shape
    heads = HG_HEADS
    dv = norm_g.shape[-1]
    vdim = heads * dv
    kdim = (w_in.shape[1] - 2 * vdim) // 2
    dk = kdim // heads
    m = nb * tt
    n_chunks = m // chunk
    has_state = s0 is not None
    const2 = lambda i, j: (0, 0)
    state_spec = pl.BlockSpec((nb, heads, dk, dv), lambda i, j: (i, 0, 0, 0))
    in_specs = [pl.BlockSpec((nb, tt, d), lambda i, j: (i, j, 0))]
    args = [x]
    if has_state:
        in_specs.append(state_spec)
        args.append(s0)
    in_specs += [pl.BlockSpec(w_in.shape, const2), pl.BlockSpec(lb_param.shape, const2),
                 pl.BlockSpec(norm_g.shape, const2), pl.BlockSpec(w_o.shape, const2),
                 pl.BlockSpec((1, d), const2), pl.BlockSpec((1, d), const2)]
    args += [w_in, lb_param, norm_g, w_o, ln_g, ln_b]
    sdt = BF16 if chunk % 16 == 0 else F32
    scratch = [pltpu.VMEM((m, w_in.shape[1]), F32),
               pltpu.VMEM((3, m, kdim), sdt),
               pltpu.VMEM((m, kdim), F32),
               pltpu.VMEM((m, kdim), sdt),
               pltpu.VMEM((m, kdim), sdt),
               pltpu.VMEM((m, kdim), sdt),
               pltpu.VMEM((m, kdim), sdt),
               pltpu.VMEM((m, vdim), sdt),
               pltpu.VMEM((n_chunks, kdim), F32),
               pltpu.VMEM((n_chunks, heads, dv, dk), F32),
               pltpu.VMEM((m, vdim), F32),
               pltpu.VMEM((m, vdim), BF16)]
    scratch += [pltpu.VMEM((nb, dv, dk), F32) for _ in range(heads)]
    return pl.pallas_call(
        functools.partial(_hgrn_tile_kernel, alpha=alpha, layer=layer, chunk=chunk, has_state=has_state),
        grid=(b // nb, t // tt),
        in_specs=in_specs,
        out_specs=[pl.BlockSpec((nb, tt, d), lambda i, j: (i, j, 0)), state_spec],
        out_shape=[jax.ShapeDtypeStruct(x.shape, F32),
                   jax.ShapeDtypeStruct((b, heads, dk, dv), F32)],
        scratch_shapes=scratch,
        compiler_params=_cparams("parallel", "arbitrary"),
        name="hgrn_state" if has_state else "hgrn_fresh",
    )(*args)


_I_E0, _I_E1, _I_G0, _I_G1, _I_R0, _I_R1 = range(6)


def _router_kernel(x_ref, w_ref, cnt0_ref, info_ref, cnt_ref, carry_ref):
    i = pl.program_id(0)
    tm = x_ref.shape[0]

    @pl.when(i == 0)
    def _():
        carry_ref[...] = cnt0_ref[...]

    x = x_ref[...]
    w = w_ref[...]
    x_hi = x.astype(BF16)
    x_lo = (x - x_hi.astype(F32)).astype(BF16)
    w_hi = w.astype(BF16)
    w_lo = (w - w_hi.astype(F32)).astype(BF16)
    logits = _dot(x_hi, w_hi) + (_dot(x_hi, w_lo) + _dot(x_lo, w_hi))
    lane = lax.broadcasted_iota(jnp.int32, (tm, LANES), 1)
    valid = lane < N_EXPERTS
    logits = jnp.where(valid, logits, -jnp.inf)
    p = jnp.exp(logits - jnp.max(logits, axis=-1, keepdims=True))
    p = p / jnp.sum(p, axis=-1, keepdims=True)
    p0 = jnp.where(valid, p, -1.0)
    v0 = jnp.max(p0, axis=-1, keepdims=True)
    e0 = jnp.min(jnp.where(p0 == v0, lane, LANES), axis=-1, keepdims=True)
    p1 = jnp.where(lane == e0, -1.0, p0)
    v1 = jnp.max(p1, axis=-1, keepdims=True)
    e1 = jnp.min(jnp.where(p1 == v1, lane, LANES), axis=-1, keepdims=True)
    den = v0 + v1
    sel = jnp.logical_or(lane == e0, lane == e1)
    ri = lax.broadcasted_iota(jnp.int32, (tm, tm), 0)
    ci = lax.broadcasted_iota(jnp.int32, (tm, tm), 1)
    before = (ri > ci).astype(BF16)
    rank = _dot(before, sel.astype(BF16)) + carry_ref[...]
    r0 = jnp.sum(jnp.where(lane == e0, rank, 0.0), axis=-1, keepdims=True)
    r1 = jnp.sum(jnp.where(lane == e1, rank, 0.0), axis=-1, keepdims=True)
    carry_ref[...] += jnp.sum(sel.astype(F32), axis=0, keepdims=True)
    info = jnp.zeros((tm, LANES), F32)
    for idx, val in ((_I_E0, e0.astype(F32)), (_I_E1, e1.astype(F32)), (_I_G0, v0 / den),
                     (_I_G1, v1 / den), (_I_R0, r0), (_I_R1, r1)):
        info = jnp.where(lane == idx, val, info)
    info_ref[...] = info
    cnt_ref[...] = carry_ref[...]


def _router(x2d, w_router_padded, counts_in, tm):
    n, d = x2d.shape
    return pl.pallas_call(
        _router_kernel,
        grid=(n // tm,),
        in_specs=[pl.BlockSpec((tm, d), lambda i: (i, 0)),
                  pl.BlockSpec((d, LANES), lambda i: (0, 0)),
                  pl.BlockSpec((1, LANES), lambda i: (0, 0))],
        out_specs=[pl.BlockSpec((tm, LANES), lambda i: (i, 0)),
                   pl.BlockSpec((1, LANES), lambda i: (0, 0))],
        out_shape=[jax.ShapeDtypeStruct((n, LANES), F32), jax.ShapeDtypeStruct((1, LANES), F32)],
        scratch_shapes=[pltpu.VMEM((1, LANES), F32)],
        compiler_params=_cparams("arbitrary"),
        name="moe_router",
    )(x2d, w_router_padded, counts_in)


def _row_copy(src, src_row, dst, dst_row, sem):
    return pltpu.make_async_copy(src.at[pl.ds(src_row, 1)], dst.at[pl.ds(dst_row, 1)], sem)


def _scatter_rows(pos_ref, x_ref, out_hbm, sem):
    tb = x_ref.shape[0]

    def issue(j, carry):
        for kk in range(TOP_K):
            _row_copy(x_ref, j, out_hbm, pos_ref[TOP_K * j + kk], sem).start()
        return carry

    lax.fori_loop(0, tb, issue, 0, unroll=4)
    for _ in range(TOP_K):
        pltpu.make_async_copy(x_ref, out_hbm.at[pl.ds(0, tb)], sem).wait()


def _dispatch_kernel(pos_ref, fill_ref, xa_ref, xb_ref, out_hbm, zero_ref, sem, zsem, *, steps_a):
    i = pl.program_id(0)

    @pl.when(i == 0)
    def _():
        zero_ref[...] = jnp.zeros_like(zero_ref)
        tz = zero_ref.shape[0]

        def fill(e):
            row = pl.multiple_of(jnp.maximum(fill_ref[e], 0), tz)
            return pltpu.make_async_copy(zero_ref, out_hbm.at[pl.ds(row, tz)], zsem)

        for e in range(fill_ref.shape[0]):
            pl.when(fill_ref[e] >= 0)(lambda e=e: fill(e).start())
        for e in range(fill_ref.shape[0]):
            pl.when(fill_ref[e] >= 0)(lambda e=e: fill(e).wait())

    pl.when(i < steps_a)(lambda: _scatter_rows(pos_ref, xa_ref, out_hbm, sem))
    pl.when(i >= steps_a)(lambda: _scatter_rows(pos_ref, xb_ref, out_hbm, sem))


def _dispatch(xa, xb, pos_flat, fill_rows, n_rows, tz):
    d = xa.shape[1]
    tb = min(1024, xa.shape[0], xb.shape[0])
    steps_a = xa.shape[0] // tb
    steps_b = xb.shape[0] // tb
    return pl.pallas_call(
        functools.partial(_dispatch_kernel, steps_a=steps_a),
        grid=(steps_a + steps_b,),
        in_specs=[pl.BlockSpec((TOP_K * tb,), lambda i: (i,), memory_space=pltpu.SMEM),
                  pl.BlockSpec(memory_space=pltpu.SMEM),
                  pl.BlockSpec((tb, d), lambda i: (jnp.minimum(i, steps_a - 1), 0)),
                  pl.BlockSpec((tb, d), lambda i: (jnp.maximum(i - steps_a, 0), 0))],
        out_specs=pl.BlockSpec(memory_space=pl.ANY),
        out_shape=jax.ShapeDtypeStruct((n_rows, d), xa.dtype),
        scratch_shapes=[pltpu.VMEM((tz, d), xa.dtype), pltpu.SemaphoreType.DMA(()),
                        pltpu.SemaphoreType.DMA(())],
        compiler_params=_cparams("arbitrary"),
        name="moe_dispatch",
    )(pos_flat, fill_rows, xa, xb)


def _combine_kernel(pos_ref, posn_ref, x_ref, info_ref, y_hbm, g_ref, b_ref, o_ref, ybuf, sem, *, alpha):
    i = pl.program_id(0)
    n = pl.num_programs(0)
    tm = x_ref.shape[0]

    def issue(p_ref, slot):
        def body(j, carry):
            for kk in range(TOP_K):
                _row_copy(y_hbm, p_ref[TOP_K * j + kk], ybuf.at[slot, kk], j, sem.at[slot]).start()
            return carry
        lax.fori_loop(0, tm, body, 0, unroll=8)

    @pl.when(i == 0)
    def _():
        issue(pos_ref, 0)

    @pl.when(i + 1 < n)
    def _():
        issue(posn_ref, (i + 1) % 2)

    slot = i % 2
    for kk in range(TOP_K):
        pltpu.make_async_copy(y_hbm.at[pl.ds(0, tm)], ybuf.at[slot, kk], sem.at[slot]).wait()
    info = info_ref[...]
    moe = info[:, _I_G0:_I_G0 + 1] * ybuf[slot, 0] + info[:, _I_G1:_I_G1 + 1] * ybuf[slot, 1]
    o_ref[...] = _layer_norm(alpha * x_ref[...] + moe, g_ref[...], b_ref[...])


def _combine_ln(x2d, info, pos_flat, y, ln_g, ln_b, alpha, tm):
    n, d = x2d.shape
    nt = n // tm
    vec = pl.BlockSpec((1, d), lambda i: (0, 0))
    return pl.pallas_call(
        functools.partial(_combine_kernel, alpha=alpha),
        grid=(nt,),
        in_specs=[pl.BlockSpec((TOP_K * tm,), lambda i: (i,), memory_space=pltpu.SMEM),
                  pl.BlockSpec((TOP_K * tm,), lambda i: (jnp.minimum(i + 1, nt - 1),), memory_space=pltpu.SMEM),
                  pl.BlockSpec((tm, d), lambda i: (i, 0)),
                  pl.BlockSpec((tm, LANES), lambda i: (i, 0)),
                  pl.BlockSpec(memory_space=pl.ANY),
                  vec, vec],
        out_specs=pl.BlockSpec((tm, d), lambda i: (i, 0)),
        out_shape=jax.ShapeDtypeStruct((n, d), F32),
        scratch_shapes=[pltpu.VMEM((2, TOP_K, tm, d), F32), pltpu.SemaphoreType.DMA((2,))],
        compiler_params=_cparams("arbitrary"),
        name="moe_combine",
    )(pos_flat, pos_flat, x2d, info, y, ln_g, ln_b)


def _moe_ln(xs2d, w_router, w_gu, w_d, ln_g, ln_b, alpha, tm_expert):
    d = xs2d[0].shape[1]
    n_total = sum(x.shape[0] for x in xs2d)
    w_r = jnp.pad(w_router, ((0, 0), (0, LANES - N_EXPERTS)))
    counts = jnp.zeros((1, LANES), F32)
    infos = []
    for x in xs2d:
        info, counts = _router(x, w_r, counts, min(x.shape[0], 1024))
        infos.append(info)
    cnt = counts[0, :N_EXPERTS].astype(jnp.int32)
    padded = (cnt + tm_expert - 1) // tm_expert * tm_expert
    ends = jnp.cumsum(padded)
    offs = ends - padded
    n_rows = -(-(TOP_K * n_total) // tm_expert) * tm_expert + N_EXPERTS * tm_expert
    tile_start = jnp.arange(n_rows // tm_expert, dtype=jnp.int32) * tm_expert
    tile_eid = jnp.minimum(jnp.sum(tile_start[:, None] >= ends[None, :], axis=1), N_EXPERTS - 1).astype(jnp.int32)
    tile_used = (tile_start < offs[tile_eid] + cnt[tile_eid]).astype(jnp.int32)
    tail = ends[-1] + jnp.arange(N_EXPERTS, dtype=jnp.int32) * tm_expert
    fill_rows = jnp.concatenate([jnp.where(padded > 0, ends - tm_expert, -1),
                                 jnp.where(tail < n_rows, tail, -1)]).astype(jnp.int32)
    poss = []
    for info in infos:
        eid = info[:, _I_E0:_I_E1 + 1].astype(jnp.int32)
        rank = info[:, _I_R0:_I_R1 + 1].astype(jnp.int32)
        poss.append((offs[eid] + rank).reshape(-1))
    xa, xb = xs2d
    buf = _dispatch(xa, xb, jnp.concatenate(poss), fill_rows, n_rows, tm_expert)
    y = _expert_ffn(buf, tile_eid, tile_used, w_gu, w_d, tm_expert)
    return [_combine_ln(x, info, pos, y, ln_g, ln_b, alpha, min(x.shape[0], 512))
            for x, info, pos in zip(xs2d, infos, poss)]


def kernel(x_prompt, x_sample, mem_prompt, state_pool, state_hgrn, cache_mem_k, cache_mem_v, pool_w, pool_scale, hg_w_in, hg_lb, hg_norm_g, hg_w_o, xa_w_q, xa_w_kv, xa_w_o, ffn_w_gu, ffn_w_d, moe_w_router, moe_w_gu, moe_w_d, ln_g, ln_b):
    b, t, d = x_prompt.shape
    sb, st, _ = x_sample.shape
    depth = ln_g.shape[0]
    mem_len = mem_prompt.shape[1]
    alpha = (2 * depth) ** 0.25
    n_prev = state_pool.shape[2]

    pool_w_b = pool_w.astype(BF16)
    hg_w_in_b = hg_w_in.astype(BF16)
    hg_w_o_b = hg_w_o.astype(BF16)
    xa_w_q_b = xa_w_q.astype(BF16)
    xa_w_kv_b = xa_w_kv.astype(BF16)
    xa_w_o_b = xa_w_o.astype(BF16)
    ffn_w_gu_b = ffn_w_gu.astype(BF16)
    ffn_w_d_b = ffn_w_d.astype(BF16)
    moe_w_gu_b = moe_w_gu.astype(BF16)
    moe_w_d_b = moe_w_d.astype(BF16)

    def vec(a):
        return a.reshape(1, -1)

    mk, mv = _mem_kv(mem_prompt.reshape(b * mem_len, d), xa_w_kv_b)
    mk4 = mk.reshape(depth, b, mem_len, d)
    mv4 = mv.reshape(depth, b, mem_len, d)

    def stored_order(c):
        halves = c.shape[4] // LANES
        c = c.reshape(depth, sb, mem_len, XA_HEADS, halves, LANES).transpose(0, 1, 2, 4, 3, 5)
        return c.reshape(depth, sb, mem_len * halves * XA_HEADS, LANES)

    ck8 = stored_order(cache_mem_k)
    cv8 = stored_order(cache_mem_v)

    tt_attn = min(t, 512)
    nb_attn = min(sb, 8)

    def attend(xp, xs, layer):
        args = (xa_w_q_b[layer], xa_w_o_b[layer], vec(ln_g[layer, 1]), vec(ln_b[layer, 1]), alpha)
        return (_mem_attend(xp, mk4, mv4, layer, *args, 1, tt_attn),
                _mem_attend_cache(xs, ck8, cv8, layer, *args, nb_attn))

    pool_args = (pool_w_b[0], vec(pool_scale[0]), vec(ln_g[0, 0]), vec(ln_b[0, 0]), alpha)
    xp = _pool_prompt(x_prompt, *pool_args)
    prev = jnp.pad(state_pool[0], ((0, 0), (POOL_HALO - n_prev, 0), (0, 0)))
    xs = _pool_sample(x_sample, prev, n_prev, *pool_args)
    xp, xs = attend(xp, xs, 0)
    ffn_args = (ffn_w_gu_b, ffn_w_d_b, vec(ln_g[0, 2]), vec(ln_b[0, 2]), alpha)
    xp = _ffn_ln(xp.reshape(b * t, d), *ffn_args, 512).reshape(b, t, d)
    xs = _ffn_ln(xs.reshape(sb * st, d), *ffn_args, 512).reshape(sb, st, d)

    hg_args = (hg_w_in_b[0], hg_lb, vec(hg_norm_g[0]), hg_w_o_b[0], vec(ln_g[1, 0]), vec(ln_b[1, 0]), alpha, 1)
    xp, hg_p = _hgrn_tiled(xp, None, *hg_args, HG_CHUNK, 1, min(t, 256))
    xs, hg_s = _hgrn_tiled(xs, state_hgrn[0], *hg_args, math.gcd(st, HG_CHUNK), min(sb, 8), st)
    xp, xs = attend(xp, xs, 1)
    yp, ys = _moe_ln([xp.reshape(b * t, d), xs.reshape(sb * st, d)], moe_w_router[0], moe_w_gu_b[0],
                     moe_w_d_b[0], vec(ln_g[1, 2]), vec(ln_b[1, 2]), alpha, 256)

    heads_hd = (depth, b, mem_len, XA_HEADS, d // XA_HEADS)
    pool_p = x_prompt[:, t - n_prev:, :][None]
    pool_s = jnp.concatenate([state_pool[0], x_sample], axis=1)[:, st:, :][None]
    return (yp.reshape(b, t, d), ys.reshape(sb, st, d), pool_p, hg_p[None],
            mk.reshape(heads_hd), mv.reshape(heads_hd), pool_s, hg_s[None])
```

```python
import functools
import math

import jax
import jax.numpy as jnp
from jax import lax
from jax.experimental import pallas as pl
from jax.experimental.pallas import tpu as pltpu

F32 = jnp.float32
BF16 = jnp.bfloat16

POOL_WINDOWS = (2, 4, 8, 16)
POOL_HALO = 16
HG_HEADS = 8
HG_CHUNK = 32
RMS_EPS = 1e-6
XA_HEADS = 4
N_EXPERTS = 8
TOP_K = 2
LN_EPS = 1e-5
LANES = 128
VMEM_LIMIT = 62 * 1024 * 1024

_NT = (((1,), (1,)), ((), ()))
_TN = (((0,), (0,)), ((), ()))


def _cparams(*sem):
    return pltpu.CompilerParams(dimension_semantics=sem, vmem_limit_bytes=VMEM_LIMIT)


def _layer_norm(z, g, b):
    mu = jnp.mean(z, axis=-1, keepdims=True)
    zc = z - mu
    var = jnp.mean(zc * zc, axis=-1, keepdims=True)
    return zc * lax.rsqrt(var + LN_EPS) * g + b


def _dot(a, b):
    return jnp.dot(a, b, preferred_element_type=F32)


def _kv_kernel(mem_ref, w_ref, k_ref, v_ref):
    kv = _dot(mem_ref[...].astype(BF16), w_ref[0])
    n = k_ref.shape[-1]
    k_ref[0] = kv[:, :n]
    v_ref[0] = kv[:, n:]


def _mem_kv(mem2d, w_kv):
    rows, d = mem2d.shape
    depth, _, e2 = w_kv.shape
    e = e2 // 2
    tm = min(rows, 512)
    out = jax.ShapeDtypeStruct((depth, rows, e), F32)
    return pl.pallas_call(
        _kv_kernel,
        grid=(depth, rows // tm),
        in_specs=[pl.BlockSpec((tm, d), lambda l, i: (i, 0)),
                  pl.BlockSpec((1, d, e2), lambda l, i: (l, 0, 0))],
        out_specs=[pl.BlockSpec((1, tm, e), lambda l, i: (l, i, 0)),
                   pl.BlockSpec((1, tm, e), lambda l, i: (l, i, 0))],
        out_shape=[out, out],
        compiler_params=_cparams("parallel", "parallel"),
        name="mem_kv",
    )(mem2d, w_kv)


def _window_sums(a, w):
    s, k = a, 1
    while k < w:
        s = s + pltpu.roll(s, k, 0)
        k *= 2
    return s


def _pool_mix(pooled_minus_tok, w_ref, scale):
    parts = [_dot(pooled_minus_tok[g].astype(BF16), w_ref[g]) for g in range(len(POOL_WINDOWS))]
    return jnp.concatenate(parts, axis=1) * scale


def _pool_prompt_kernel(x_ref, halo_ref, w_ref, scale_ref, g_ref, b_ref, o_ref, buf_ref, *, alpha):
    t = pl.program_id(1)
    tt = x_ref.shape[1]
    gw = w_ref.shape[-1]
    x = x_ref[0]
    buf_ref[0:POOL_HALO, :] = jnp.where(t > 0, halo_ref[0], 0.0)
    buf_ref[POOL_HALO:, :] = x
    pos = t * tt + lax.broadcasted_iota(jnp.int32, (tt, 1), 0)
    diffs = []
    for g, w in enumerate(POOL_WINDOWS):
        sl = slice(g * gw, (g + 1) * gw)
        s = _window_sums(buf_ref[:, sl], w)[POOL_HALO:, :]
        cnt = jnp.minimum(pos + 1, w).astype(F32)
        diffs.append(s / cnt - x[:, sl])
    mixed = _pool_mix(diffs, w_ref, scale_ref[...])
    o_ref[0] = _layer_norm(alpha * x + mixed, g_ref[...], b_ref[...])


def _pool_prompt(x, pool_w, scale, ln_g, ln_b, alpha):
    b, t, d = x.shape
    tt = min(t, 1024)
    hb = tt // POOL_HALO
    vec = pl.BlockSpec((1, d), lambda i, j: (0, 0))
    return pl.pallas_call(
        functools.partial(_pool_prompt_kernel, alpha=alpha),
        grid=(b, t // tt),
        in_specs=[pl.BlockSpec((1, tt, d), lambda i, j: (i, j, 0)),
                  pl.BlockSpec((1, POOL_HALO, d), lambda i, j: (i, jnp.maximum(j * hb - 1, 0), 0)),
                  pl.BlockSpec(pool_w.shape, lambda i, j: (0, 0, 0)),
                  vec, vec, vec],
        out_specs=pl.BlockSpec((1, tt, d), lambda i, j: (i, j, 0)),
        out_shape=jax.ShapeDtypeStruct(x.shape, F32),
        scratch_shapes=[pltpu.VMEM((tt + POOL_HALO, d), F32)],
        compiler_params=_cparams("parallel", "parallel"),
        name="pool_prompt",
    )(x, x, pool_w, scale, ln_g, ln_b)


def _pool_sample_kernel(x_ref, prev_ref, w_ref, scale_ref, g_ref, b_ref, o_ref, buf_ref, *, alpha, n_prev):
    nb, t, d = x_ref.shape
    gw = w_ref.shape[-1]
    rows = POOL_HALO + t
    buf_ref[:, 0:POOL_HALO, :] = prev_ref[...]
    buf_ref[:, POOL_HALO:, :] = x_ref[...]
    x = x_ref[...].reshape(nb * t, d)
    pos = lax.broadcasted_iota(jnp.int32, (nb, t, 1), 1).reshape(nb * t, 1)
    diffs = []
    for g, w in enumerate(POOL_WINDOWS):
        sl = slice(g * gw, (g + 1) * gw)
        s = _window_sums(buf_ref[:, :, sl].reshape(nb * rows, gw), w)
        s = s.reshape(nb, rows, gw)[:, POOL_HALO:, :].reshape(nb * t, gw)
        cnt = jnp.minimum(pos + 1 + n_prev, w).astype(F32)
        diffs.append(s / cnt - x[:, sl])
    mixed = _pool_mix(diffs, w_ref, scale_ref[...])
    o_ref[...] = _layer_norm(alpha * x + mixed, g_ref[...], b_ref[...]).reshape(nb, t, d)


def _pool_sample(x, prev_padded, n_prev, pool_w, scale, ln_g, ln_b, alpha):
    b, t, d = x.shape
    nb = min(b, 16)
    vec = pl.BlockSpec((1, d), lambda i: (0, 0))
    return pl.pallas_call(
        functools.partial(_pool_sample_kernel, alpha=alpha, n_prev=n_prev),
        grid=(b // nb,),
        in_specs=[pl.BlockSpec((nb, t, d), lambda i: (i, 0, 0)),
                  pl.BlockSpec((nb, POOL_HALO, d), lambda i: (i, 0, 0)),
                  pl.BlockSpec(pool_w.shape, lambda i: (0, 0, 0)),
                  vec, vec, vec],
        out_specs=pl.BlockSpec((nb, t, d), lambda i: (i, 0, 0)),
        out_shape=jax.ShapeDtypeStruct(x.shape, F32),
        scratch_shapes=[pltpu.VMEM((nb, POOL_HALO + t, d), F32)],
        compiler_params=_cparams("parallel"),
        name="pool_sample",
    )(x, prev_padded, pool_w, scale, ln_g, ln_b)


def _attn_kernel(x_ref, k_ref, v_ref, wq_ref, wo_ref, g_ref, b_ref, o_ref, oh_ref, s_ref, p_ref, *, alpha):
    nb, tt, d = x_ref.shape
    hd = d // XA_HEADS
    scale = hd ** -0.5
    x = x_ref[...].reshape(nb * tt, d)
    q = _dot(x.astype(BF16), wq_ref[...])
    for h in range(XA_HEADS):
        sl = slice(h * hd, (h + 1) * hd)
        qh = q[:, sl].astype(BF16).reshape(nb, tt, hd)
        kh = k_ref[:, :, sl].astype(BF16)
        s_ref[h] = jnp.einsum("btd,bmd->btm", qh, kh, preferred_element_type=F32) * scale
    for h in range(XA_HEADS):
        s = s_ref[h]
        p = jnp.exp(s - jnp.max(s, axis=-1, keepdims=True))
        p_ref[h] = (p / jnp.sum(p, axis=-1, keepdims=True)).astype(BF16)
    for h in range(XA_HEADS):
        sl = slice(h * hd, (h + 1) * hd)
        vh = v_ref[:, :, sl].astype(BF16)
        oh = jnp.einsum("btm,bmd->btd", p_ref[h], vh, preferred_element_type=F32)
        oh_ref[:, sl] = oh.reshape(nb * tt, hd).astype(BF16)
    a = _dot(oh_ref[...], wo_ref[...])
    o_ref[...] = _layer_norm(alpha * x + a, g_ref[...], b_ref[...]).reshape(nb, tt, d)


def _mem_attend(x, k4, v4, layer, wq, wo, ln_g, ln_b, alpha, nb, tt):
    b, t, d = x.shape
    m = k4.shape[2]
    vec = pl.BlockSpec((1, d), lambda i, j: (0, 0))
    mat = pl.BlockSpec((d, d), lambda i, j: (0, 0))
    kv = pl.BlockSpec((None, nb, m, d), lambda i, j: (layer, i, 0, 0))
    return pl.pallas_call(
        functools.partial(_attn_kernel, alpha=alpha),
        grid=(b // nb, t // tt),
        in_specs=[pl.BlockSpec((nb, tt, d), lambda i, j: (i, j, 0)), kv, kv, mat, mat, vec, vec],
        out_specs=pl.BlockSpec((nb, tt, d), lambda i, j: (i, j, 0)),
        out_shape=jax.ShapeDtypeStruct(x.shape, F32),
        scratch_shapes=[pltpu.VMEM((nb * tt, d), BF16), pltpu.VMEM((XA_HEADS, nb, tt, m), F32),
                        pltpu.VMEM((XA_HEADS, nb, tt, m), BF16)],
        compiler_params=_cparams("parallel", "parallel"),
        name=f"mem_attend_l{layer}",
    )(x, k4, v4, wq, wo, ln_g, ln_b)


def _attn_cache_kernel(x_ref, k_ref, v_ref, wq_ref, wo_ref, g_ref, b_ref, o_ref, q_scr, oh_scr, s_scr, p_scr,
                       *, alpha):
    nb, t, d = x_ref.shape
    heads = XA_HEADS
    hd = d // heads
    halves = hd // LANES
    assert halves == 2
    rows = k_ref.shape[1]
    qr = heads * t
    scale = hd ** -0.5
    x = x_ref[...].reshape(nb * t, d)
    q = _dot(x.astype(BF16), wq_ref[...])
    for c in range(halves):
        for h in range(heads):
            col = (h * halves + c) * LANES
            q_scr[c, :, h * t:(h + 1) * t, :] = q[:, col:col + LANES].reshape(nb, t, LANES)

    col_id = lax.broadcasted_iota(jnp.int32, (qr, rows), 1)
    row_head = lax.broadcasted_iota(jnp.int32, (qr, rows), 0) // t
    col_half = (col_id // heads) % halves
    col_head = col_id % heads
    keep = jnp.logical_and(col_half == 1, col_head == row_head)

    for b in range(nb):
        kb = k_ref[b].astype(BF16)
        g0 = lax.dot_general(q_scr[0, b].astype(BF16), kb, _NT, preferred_element_type=F32)
        g1 = lax.dot_general(q_scr[1, b].astype(BF16), kb, _NT, preferred_element_type=F32)
        s_scr[b] = jnp.where(col_half == 0, g0, g1)
    for b in range(nb):
        part = s_scr[b]
        s = (part + pltpu.roll(part, heads, 1)) * scale
        s = jnp.where(keep, s, -jnp.inf)
        p = jnp.exp(s - jnp.max(s, axis=-1, keepdims=True))
        p = p / jnp.sum(p, axis=-1, keepdims=True)
        p_scr[0, b] = pltpu.roll(p, rows - heads, 1).astype(BF16)
        p_scr[1, b] = p.astype(BF16)
    for b in range(nb):
        vb = v_ref[b].astype(BF16)
        for c in range(halves):
            oc = _dot(p_scr[c, b], vb)
            for h in range(heads):
                col = (h * halves + c) * LANES
                oh_scr[b * t:(b + 1) * t, col:col + LANES] = oc[h * t:(h + 1) * t, :]
    a = _dot(oh_scr[...].astype(BF16), wo_ref[...])
    o_ref[...] = _layer_norm(alpha * x + a, g_ref[...], b_ref[...]).reshape(nb, t, d)


def _mem_attend_cache(x, k8, v8, layer, wq, wo, ln_g, ln_b, alpha, nb):
    b, t, d = x.shape
    rows = k8.shape[2]
    vec = pl.BlockSpec((1, d), lambda i: (0, 0))
    mat = pl.BlockSpec((d, d), lambda i: (0, 0))
    kv = pl.BlockSpec((None, nb, rows, LANES), lambda i: (layer, i, 0, 0))
    return pl.pallas_call(
        functools.partial(_attn_cache_kernel, alpha=alpha),
        grid=(b // nb,),
        in_specs=[pl.BlockSpec((nb, t, d), lambda i: (i, 0, 0)), kv, kv, mat, mat, vec, vec],
        out_specs=pl.BlockSpec((nb, t, d), lambda i: (i, 0, 0)),
        out_shape=jax.ShapeDtypeStruct(x.shape, F32),
        scratch_shapes=[pltpu.VMEM((2, nb, XA_HEADS * t, LANES), F32), pltpu.VMEM((nb * t, d), F32),
                        pltpu.VMEM((nb, XA_HEADS * t, rows), F32), pltpu.VMEM((2, nb, XA_HEADS * t, rows), BF16)],
        compiler_params=_cparams("parallel"),
        name=f"mem_attend_cache_l{layer}",
    )(x, k8, v8, wq, wo, ln_g, ln_b)


FFN_SUB = 512


def _swiglu(x_ref, wg_ref, wu_ref, wd_ref, h_ref):
    xb = x_ref[...].astype(BF16)
    f = h_ref.shape[1]
    for s in range(f // FFN_SUB):
        sl = slice(s * FFN_SUB, (s + 1) * FFN_SUB)
        gate = _dot(xb, wg_ref[0, :, sl])
        up = _dot(xb, wu_ref[0, :, sl])
        h_ref[:, sl] = (gate * jax.nn.sigmoid(gate) * up).astype(BF16)
    return _dot(h_ref[...], wd_ref[0])


def _ffn_ln_kernel(x_ref, wg_ref, wu_ref, wd_ref, g_ref, b_ref, o_ref, h_ref, *, alpha):
    y = _swiglu(x_ref, wg_ref, wu_ref, wd_ref, h_ref)
    o_ref[...] = _layer_norm(alpha * x_ref[...] + y, g_ref[...], b_ref[...])


def _ffn_ln(x2d, w_gu, w_d, ln_g, ln_b, alpha, tm):
    n, d = x2d.shape
    tm = min(tm, n)
    f = w_d.shape[1]
    vec = pl.BlockSpec((1, d), lambda i: (0, 0))
    once = pl.Buffered(1)
    return pl.pallas_call(
        functools.partial(_ffn_ln_kernel, alpha=alpha),
        grid=(n // tm,),
        in_specs=[pl.BlockSpec((tm, d), lambda i: (i, 0)),
                  pl.BlockSpec((1, d, f), lambda i: (0, 0, 0), pipeline_mode=once),
                  pl.BlockSpec((1, d, f), lambda i: (0, 0, 1), pipeline_mode=once),
                  pl.BlockSpec((1, f, d), lambda i: (0, 0, 0), pipeline_mode=once),
                  vec, vec],
        out_specs=pl.BlockSpec((tm, d), lambda i: (i, 0)),
        out_shape=jax.ShapeDtypeStruct((n, d), F32),
        scratch_shapes=[pltpu.VMEM((tm, f), BF16)],
        compiler_params=_cparams("parallel"),
        name="ffn_dense",
    )(x2d, w_gu, w_gu, w_d, ln_g, ln_b)


def _expert_kernel(eid_ref, used_ref, x_ref, wg_ref, wu_ref, wd_ref, o_ref, h_ref):
    i = pl.program_id(0)

    @pl.when(used_ref[i] > 0)
    def _():
        o_ref[...] = _swiglu(x_ref, wg_ref, wu_ref, wd_ref, h_ref)

    @pl.when(used_ref[i] == 0)
    def _():
        o_ref[...] = jnp.zeros_like(o_ref)


def _expert_ffn(xs, tile_eid, tile_used, w_gu, w_d, tm):
    r, d = xs.shape
    f = w_d.shape[1]
    grid_spec = pltpu.PrefetchScalarGridSpec(
        num_scalar_prefetch=2,
        grid=(r // tm,),
        in_specs=[pl.BlockSpec((tm, d), lambda i, e, u: (i, 0)),
                  pl.BlockSpec((1, d, f), lambda i, e, u: (e[i], 0, 0)),
                  pl.BlockSpec((1, d, f), lambda i, e, u: (e[i], 0, 1)),
                  pl.BlockSpec((1, f, d), lambda i, e, u: (e[i], 0, 0))],
        out_specs=pl.BlockSpec((tm, d), lambda i, e, u: (i, 0)),
        scratch_shapes=[pltpu.VMEM((tm, f), BF16)],
    )
    return pl.pallas_call(
        _expert_kernel,
        grid_spec=grid_spec,
        out_shape=jax.ShapeDtypeStruct((r, d), F32),
        compiler_params=_cparams("arbitrary"),
        name="ffn_experts",
    )(tile_eid, tile_used, xs, w_gu, w_gu, w_d)


def _split3(x):
    hi = x.astype(BF16).astype(F32)
    r = x - hi
    mid = r.astype(BF16).astype(F32)
    lo = (r - mid).astype(BF16).astype(F32)
    return hi, mid, lo


def _hgrn_tile_kernel(*refs, alpha, layer, chunk, has_state):
    st_refs = refs[-HG_HEADS:]
    refs = refs[:-HG_HEADS]
    s0_ref = None
    if has_state:
        s0_ref, refs = refs[1], refs[:1] + refs[2:]
    (x_ref, w_in_ref, lbp_ref, ng_ref, w_o_ref, g_ref, b_ref, o_ref, sfin_ref,
     proj_ref, lf_ref, bcum_ref, qrel_ref, krel_ref, kend_ref, qdec_ref, vb_ref, dec_ref,
     u_ref, oacc_ref, og_ref) = refs
    t = pl.program_id(1)
    nb, tt, d = x_ref.shape
    heads = HG_HEADS
    dv = ng_ref.shape[-1]
    vdim = heads * dv
    kdim = (w_in_ref.shape[1] - 2 * vdim) // 2
    dk = kdim // heads
    m = nb * tt
    c = chunk
    n_chunks = m // c
    chunks_per_seq = tt // c
    mid = max(c // 2 - 1, 0)

    @pl.when(t == 0)
    def _():
        for h in range(heads):
            if has_state:
                for bi in range(nb):
                    st_refs[h][bi] = s0_ref[bi, h].T
            else:
                st_refs[h][...] = jnp.zeros_like(st_refs[h])

    x = x_ref[...].reshape(m, d)
    xb = x.astype(BF16)

    lbp = lbp_ref[...]
    e = jnp.exp(lbp - jnp.max(lbp, axis=0, keepdims=True))
    sm = e / jnp.sum(e, axis=0, keepdims=True)
    lb = jnp.sum(sm[1:layer + 1], axis=0, keepdims=True)

    f_cols = slice(kdim, 2 * kdim)
    proj_ref[:, f_cols] = _dot(xb, w_in_ref[:, f_cols])
    for j in range(n_chunks):
        rows = slice(j * c, (j + 1) * c)
        f = lb + (1.0 - lb) * jax.nn.sigmoid(proj_ref[rows, f_cols])
        proj_ref[rows, f_cols] = 1.0 - f
        for p, part in enumerate(_split3(jnp.log(f))):
            lf_ref[p, rows, :] = part.astype(lf_ref.dtype)
    proj_ref[:, 0:kdim] = _dot(xb, w_in_ref[:, 0:kdim])
    proj_ref[:, 2 * kdim:] = _dot(xb, w_in_ref[:, 2 * kdim:])

    ri = lax.broadcasted_iota(jnp.int32, (m, m), 0)
    ci = lax.broadcasted_iota(jnp.int32, (m, m), 1)
    causal = jnp.logical_and(ri // c == ci // c, ri >= ci)
    tri = causal.astype(BF16)
    bcum_ref[...] = (_dot(tri, lf_ref[0].astype(BF16)) + _dot(tri, lf_ref[1].astype(BF16))
                     + _dot(tri, lf_ref[2].astype(BF16)))

    sdt = qrel_ref.dtype
    vb_ref[...] = proj_ref[:, 2 * kdim:2 * kdim + vdim].astype(sdt)
    for j in range(n_chunks):
        rows = slice(j * c, (j + 1) * c)
        bcum = bcum_ref[rows, :]
        bmid = bcum[mid:mid + 1]
        blast = bcum[c - 1:c]
        q_dec = proj_ref[rows, 0:kdim] * jnp.exp(bcum)
        k_rel = proj_ref[rows, f_cols] * jnp.exp(bmid - bcum)
        qdec_ref[rows, :] = q_dec.astype(sdt)
        qrel_ref[rows, :] = (q_dec * jnp.exp(-bmid)).astype(sdt)
        krel_ref[rows, :] = k_rel.astype(sdt)
        kend_ref[rows, :] = (k_rel * jnp.exp(blast - bmid)).astype(sdt)
        dec_ref[j:j + 1, :] = jnp.exp(blast)

    for h in range(heads):
        sk = slice(h * dk, (h + 1) * dk)
        sv = slice(h * dv, (h + 1) * dv)
        a = lax.dot_general(qrel_ref[:, sk].astype(BF16), krel_ref[:, sk].astype(BF16), _NT,
                            preferred_element_type=F32)
        a = jnp.where(causal, a, 0.0).astype(BF16)
        oacc_ref[:, sv] = _dot(a, vb_ref[:, sv].astype(BF16))

    for j in range(n_chunks):
        rows = slice(j * c, (j + 1) * c)
        for h in range(heads):
            sk = slice(h * dk, (h + 1) * dk)
            sv = slice(h * dv, (h + 1) * dv)
            u_ref[j, h] = lax.dot_general(vb_ref[rows, sv].astype(BF16), kend_ref[rows, sk].astype(BF16), _TN,
                                          preferred_element_type=F32)

    for j in range(n_chunks):
        rows = slice(j * c, (j + 1) * c)
        bi = j // chunks_per_seq
        for h in range(heads):
            sk = slice(h * dk, (h + 1) * dk)
            sv = slice(h * dv, (h + 1) * dv)
            st = st_refs[h][bi]
            oacc_ref[rows, sv] += lax.dot_general(qdec_ref[rows, sk].astype(BF16), st.astype(BF16), _NT,
                                                  preferred_element_type=F32)
            st_refs[h][bi] = st * dec_ref[j:j + 1, sk] + u_ref[j, h]

    ng = ng_ref[...]
    for h in range(heads):
        sv = slice(h * dv, (h + 1) * dv)
        o = oacc_ref[:, sv]
        o = o * lax.rsqrt(jnp.mean(o * o, axis=-1, keepdims=True) + RMS_EPS) * ng
        gate = proj_ref[:, 2 * kdim + vdim + h * dv:2 * kdim + vdim + (h + 1) * dv]
        og_ref[:, sv] = (o * (gate * jax.nn.sigmoid(gate))).astype(BF16)
    out = _dot(og_ref[...], w_o_ref[...])
    o_ref[...] = _layer_norm(alpha * x + out, g_ref[...], b_ref[...]).reshape(nb, tt, d)

    @pl.when(t == pl.num_programs(1) - 1)
    def _():
        for bi in range(nb):
            for h in range(heads):
                sfin_ref[bi, h] = st_refs[h][bi].T


def _hgrn_tiled(x, s0, w_in, lb_param, norm_g, w_o, ln_g, ln_b, alpha, layer, chunk, nb, tt):
    b, t, d = x.shape
    heads = HG_HEADS
    dv = norm_g.shape[-1]
    vdim = heads * dv
    kdim = (w_in.shape[1] - 2 * vdim) // 2
    dk = kdim // heads
    m = nb * tt
    n_chunks = m // chunk
    has_state = s0 is not None
    const2 = lambda i, j: (0, 0)
    state_spec = pl.BlockSpec((nb, heads, dk, dv), lambda i, j: (i, 0, 0, 0))
    in_specs = [pl.BlockSpec((nb, tt, d), lambda i, j: (i, j, 0))]
    args = [x]
    if has_state:
        in_specs.append(state_spec)
        args.append(s0)
    in_specs += [pl.BlockSpec(w_in.shape, const2), pl.BlockSpec(lb_param.shape, const2),
                 pl.BlockSpec(norm_g.shape, const2), pl.BlockSpec(w_o.shape, const2),
                 pl.BlockSpec((1, d), const2), pl.BlockSpec((1, d), const2)]
    args += [w_in, lb_param, norm_g, w_o, ln_g, ln_b]
    sdt = BF16 if chunk % 16 == 0 else F32
    scratch = [pltpu.VMEM((m, w_in.shape[1]), F32),
               pltpu.VMEM((3, m, kdim), sdt),
               pltpu.VMEM((m, kdim), F32),
               pltpu.VMEM((m, kdim), sdt),
               pltpu.VMEM((m, kdim), sdt),
               pltpu.VMEM((m, kdim), sdt),
               pltpu.VMEM((m, kdim), sdt),
               pltpu.VMEM((m, vdim), sdt),
               pltpu.VMEM((n_chunks, kdim), F32),
               pltpu.VMEM((n_chunks, heads, dv, dk), F32),
               pltpu.VMEM((m, vdim), F32),
               pltpu.VMEM((m, vdim), BF16)]
    scratch += [pltpu.VMEM((nb, dv, dk), F32) for _ in range(heads)]
    return pl.pallas_call(
        functools.partial(_hgrn_tile_kernel, alpha=alpha, layer=layer, chunk=chunk, has_state=has_state),
        grid=(b // nb, t // tt),
        in_specs=in_specs,
        out_specs=[pl.BlockSpec((nb, tt, d), lambda i, j: (i, j, 0)), state_spec],
        out_shape=[jax.ShapeDtypeStruct(x.shape, F32),
                   jax.ShapeDtypeStruct((b, heads, dk, dv), F32)],
        scratch_shapes=scratch,
        compiler_params=_cparams("parallel", "arbitrary"),
        name="hgrn_state" if has_state else "hgrn_fresh",
    )(*args)


_I_E0, _I_E1, _I_G0, _I_G1, _I_R0, _I_R1 = range(6)


def _router_kernel(x_ref, w_ref, cnt0_ref, info_ref, cnt_ref, carry_ref):
    i = pl.program_id(0)
    tm = x_ref.shape[0]

    @pl.when(i == 0)
    def _():
        carry_ref[...] = cnt0_ref[...]

    x = x_ref[...]
    w = w_ref[...]
    x_hi = x.astype(BF16)
    x_lo = (x - x_hi.astype(F32)).astype(BF16)
    w_hi = w.astype(BF16)
    w_lo = (w - w_hi.astype(F32)).astype(BF16)
    logits = _dot(x_hi, w_hi) + (_dot(x_hi, w_lo) + _dot(x_lo, w_hi))
    lane = lax.broadcasted_iota(jnp.int32, (tm, LANES), 1)
    valid = lane < N_EXPERTS
    logits = jnp.where(valid, logits, -jnp.inf)
    p = jnp.exp(logits - jnp.max(logits, axis=-1, keepdims=True))
    p = p / jnp.sum(p, axis=-1, keepdims=True)
    p0 = jnp.where(valid, p, -1.0)
    v0 = jnp.max(p0, axis=-1, keepdims=True)
    e0 = jnp.min(jnp.where(p0 == v0, lane, LANES), axis=-1, keepdims=True)
    p1 = jnp.where(lane == e0, -1.0, p0)
    v1 = jnp.max(p1, axis=-1, keepdims=True)
    e1 = jnp.min(jnp.where(p1 == v1, lane, LANES), axis=-1, keepdims=True)
    den = v0 + v1
    sel = jnp.logical_or(lane == e0, lane == e1)
    ri = lax.broadcasted_iota(jnp.int32, (tm, tm), 0)
    ci = lax.broadcasted_iota(jnp.int32, (tm, tm), 1)
    before = (ri > ci).astype(BF16)
    rank = _dot(before, sel.astype(BF16)) + carry_ref[...]
    r0 = jnp.sum(jnp.where(lane == e0, rank, 0.0), axis=-1, keepdims=True)
    r1 = jnp.sum(jnp.where(lane == e1, rank, 0.0), axis=-1, keepdims=True)
    carry_ref[...] += jnp.sum(sel.astype(F32), axis=0, keepdims=True)
    info = jnp.zeros((tm, LANES), F32)
    for idx, val in ((_I_E0, e0.astype(F32)), (_I_E1, e1.astype(F32)), (_I_G0, v0 / den),
                     (_I_G1, v1 / den), (_I_R0, r0), (_I_R1, r1)):
        info = jnp.where(lane == idx, val, info)
    info_ref[...] = info
    cnt_ref[...] = carry_ref[...]


def _router(x2d, w_router_padded, counts_in, tm):
    n, d = x2d.shape
    return pl.pallas_call(
        _router_kernel,
        grid=(n // tm,),
        in_specs=[pl.BlockSpec((tm, d), lambda i: (i, 0)),
                  pl.BlockSpec((d, LANES), lambda i: (0, 0)),
                  pl.BlockSpec((1, LANES), lambda i: (0, 0))],
        out_specs=[pl.BlockSpec((tm, LANES), lambda i: (i, 0)),
                   pl.BlockSpec((1, LANES), lambda i: (0, 0))],
        out_shape=[jax.ShapeDtypeStruct((n, LANES), F32), jax.ShapeDtypeStruct((1, LANES), F32)],
        scratch_shapes=[pltpu.VMEM((1, LANES), F32)],
        compiler_params=_cparams("arbitrary"),
        name="moe_router",
    )(x2d, w_router_padded, counts_in)


def _row_copy(src, src_row, dst, dst_row, sem):
    return pltpu.make_async_copy(src.at[pl.ds(src_row, 1)], dst.at[pl.ds(dst_row, 1)], sem)


def _scatter_rows(pos_ref, x_ref, out_hbm, sem):
    tb = x_ref.shape[0]

    def issue(j, carry):
        for kk in range(TOP_K):
            _row_copy(x_ref, j, out_hbm, pos_ref[TOP_K * j + kk], sem).start()
        return carry

    lax.fori_loop(0, tb, issue, 0, unroll=4)
    for _ in range(TOP_K):
        pltpu.make_async_copy(x_ref, out_hbm.at[pl.ds(0, tb)], sem).wait()


def _dispatch_kernel(pos_ref, fill_ref, xa_ref, xb_ref, out_hbm, zero_ref, sem, zsem, *, steps_a):
    i = pl.program_id(0)

    @pl.when(i == 0)
    def _():
        zero_ref[...] = jnp.zeros_like(zero_ref)
        tz = zero_ref.shape[0]

        def fill(e):
            row = pl.multiple_of(jnp.maximum(fill_ref[e], 0), tz)
            return pltpu.make_async_copy(zero_ref, out_hbm.at[pl.ds(row, tz)], zsem)

        for e in range(fill_ref.shape[0]):
            pl.when(fill_ref[e] >= 0)(lambda e=e: fill(e).start())
        for e in range(fill_ref.shape[0]):
            pl.when(fill_ref[e] >= 0)(lambda e=e: fill(e).wait())

    pl.when(i < steps_a)(lambda: _scatter_rows(pos_ref, xa_ref, out_hbm, sem))
    pl.when(i >= steps_a)(lambda: _scatter_rows(pos_ref, xb_ref, out_hbm, sem))


def _dispatch(xa, xb, pos_flat, fill_rows, n_rows, tz):
    d = xa.shape[1]
    tb = min(1024, xa.shape[0], xb.shape[0])
    steps_a = xa.shape[0] // tb
    steps_b = xb.shape[0] // tb
    return pl.pallas_call(
        functools.partial(_dispatch_kernel, steps_a=steps_a),
        grid=(steps_a + steps_b,),
        in_specs=[pl.BlockSpec((TOP_K * tb,), lambda i: (i,), memory_space=pltpu.SMEM),
                  pl.BlockSpec(memory_space=pltpu.SMEM),
                  pl.BlockSpec((tb, d), lambda i: (jnp.minimum(i, steps_a - 1), 0)),
                  pl.BlockSpec((tb, d), lambda i: (jnp.maximum(i - steps_a, 0), 0))],
        out_specs=pl.BlockSpec(memory_space=pl.ANY),
        out_shape=jax.ShapeDtypeStruct((n_rows, d), xa.dtype),
        scratch_shapes=[pltpu.VMEM((tz, d), xa.dtype), pltpu.SemaphoreType.DMA(()),
                        pltpu.SemaphoreType.DMA(())],
        compiler_params=_cparams("arbitrary"),
        name="moe_dispatch",
    )(pos_flat, fill_rows, xa, xb)


def _combine_kernel(pos_ref, posn_ref, x_ref, info_ref, y_hbm, g_ref, b_ref, o_ref, ybuf, sem, *, alpha):
    i = pl.program_id(0)
    n = pl.num_programs(0)
    tm = x_ref.shape[0]

    def issue(p_ref, slot):
        def body(j, carry):
            for kk in range(TOP_K):
                _row_copy(y_hbm, p_ref[TOP_K * j + kk], ybuf.at[slot, kk], j, sem.at[slot]).start()
            return carry
        lax.fori_loop(0, tm, body, 0, unroll=8)

    @pl.when(i == 0)
    def _():
        issue(pos_ref, 0)

    @pl.when(i + 1 < n)
    def _():
        issue(posn_ref, (i + 1) % 2)

    slot = i % 2
    for kk in range(TOP_K):
        pltpu.make_async_copy(y_hbm.at[pl.ds(0, tm)], ybuf.at[slot, kk], sem.at[slot]).wait()
    info = info_ref[...]
    moe = info[:, _I_G0:_I_G0 + 1] * ybuf[slot, 0] + info[:, _I_G1:_I_G1 + 1] * ybuf[slot, 1]
    o_ref[...] = _layer_norm(alpha * x_ref[...] + moe, g_ref[...], b_ref[...])


def _combine_ln(x2d, info, pos_flat, y, ln_g, ln_b, alpha, tm):
    n, d = x2d.shape
    nt = n // tm
    vec = pl.BlockSpec((1, d), lambda i: (0, 0))
    return pl.pallas_call(
        functools.partial(_combine_kernel, alpha=alpha),
        grid=(nt,),
        in_specs=[pl.BlockSpec((TOP_K * tm,), lambda i: (i,), memory_space=pltpu.SMEM),
                  pl.BlockSpec((TOP_K * tm,), lambda i: (jnp.minimum(i + 1, nt - 1),), memory_space=pltpu.SMEM),
                  pl.BlockSpec((tm, d), lambda i: (i, 0)),
                  pl.BlockSpec((tm, LANES), lambda i: (i, 0)),
                  pl.BlockSpec(memory_space=pl.ANY),
                  vec, vec],
        out_specs=pl.BlockSpec((tm, d), lambda i: (i, 0)),
        out_shape=jax.ShapeDtypeStruct((n, d), F32),
        scratch_shapes=[pltpu.VMEM((2, TOP_K, tm, d), F32), pltpu.SemaphoreType.DMA((2,))],
        compiler_params=_cparams("arbitrary"),
        name="moe_combine",
    )(pos_flat, pos_flat, x2d, info, y, ln_g, ln_b)


def _moe_ln(xs2d, w_router, w_gu, w_d, ln_g, ln_b, alpha, tm_expert):
    d = xs2d[0].shape[1]
    n_total = sum(x.shape[0] for x in xs2d)
    w_r = jnp.pad(w_router, ((0, 0), (0, LANES - N_EXPERTS)))
    counts = jnp.zeros((1, LANES), F32)
    infos = []
    for x in xs2d:
        info, counts = _router(x, w_r, counts, min(x.shape[0], 1024))
        infos.append(info)
    cnt = counts[0, :N_EXPERTS].astype(jnp.int32)
    padded = (cnt + tm_expert - 1) // tm_expert * tm_expert
    ends = jnp.cumsum(padded)
    offs = ends - padded
    n_rows = -(-(TOP_K * n_total) // tm_expert) * tm_expert + N_EXPERTS * tm_expert
    tile_start = jnp.arange(n_rows // tm_expert, dtype=jnp.int32) * tm_expert
    tile_eid = jnp.minimum(jnp.sum(tile_start[:, None] >= ends[None, :], axis=1), N_EXPERTS - 1).astype(jnp.int32)
    tile_used = (tile_start < offs[tile_eid] + cnt[tile_eid]).astype(jnp.int32)
    tail = ends[-1] + jnp.arange(N_EXPERTS, dtype=jnp.int32) * tm_expert
    fill_rows = jnp.concatenate([jnp.where(padded > 0, ends - tm_expert, -1),
                                 jnp.where(tail < n_rows, tail, -1)]).astype(jnp.int32)
    poss = []
    for info in infos:
        eid = info[:, _I_E0:_I_E1 + 1].astype(jnp.int32)
        rank = info[:, _I_R0:_I_R1 + 1].astype(jnp.int32)
        poss.append((offs[eid] + rank).reshape(-1))
    xa, xb = xs2d
    buf = _dispatch(xa, xb, jnp.concatenate(poss), fill_rows, n_rows, tm_expert)
    y = _expert_ffn(buf, tile_eid, tile_used, w_gu, w_d, tm_expert)
    return [_combine_ln(x, info, pos, y, ln_g, ln_b, alpha, min(x.shape[0], 1024))
            for x, info, pos in zip(xs2d, infos, poss)]


def kernel(x_prompt, x_sample, mem_prompt, state_pool, state_hgrn, cache_mem_k, cache_mem_v, pool_w, pool_scale, hg_w_in, hg_lb, hg_norm_g, hg_w_o, xa_w_q, xa_w_kv, xa_w_o, ffn_w_gu, ffn_w_d, moe_w_router, moe_w_gu, moe_w_d, ln_g, ln_b):
    b, t, d = x_prompt.shape
    sb, st, _ = x_sample.shape
    depth = ln_g.shape[0]
    mem_len = mem_prompt.shape[1]
    alpha = (2 * depth) ** 0.25
    n_prev = state_pool.shape[2]

    pool_w_b = pool_w.astype(BF16)
    hg_w_in_b = hg_w_in.astype(BF16)
    hg_w_o_b = hg_w_o.astype(BF16)
    xa_w_q_b = xa_w_q.astype(BF16)
    xa_w_kv_b = xa_w_kv.astype(BF16)
    xa_w_o_b = xa_w_o.astype(BF16)
    ffn_w_gu_b = ffn_w_gu.astype(BF16)
    ffn_w_d_b = ffn_w_d.astype(BF16)
    moe_w_gu_b = moe_w_gu.astype(BF16)
    moe_w_d_b = moe_w_d.astype(BF16)

    def vec(a):
        return a.reshape(1, -1)

    mk, mv = _mem_kv(mem_prompt.reshape(b * mem_len, d), xa_w_kv_b)
    mk4 = mk.reshape(depth, b, mem_len, d)
    mv4 = mv.reshape(depth, b, mem_len, d)

    def stored_order(c):
        halves = c.shape[4] // LANES
        c = c.reshape(depth, sb, mem_len, XA_HEADS, halves, LANES).transpose(0, 1, 2, 4, 3, 5)
        return c.reshape(depth, sb, mem_len * halves * XA_HEADS, LANES)

    ck8 = stored_order(cache_mem_k)
    cv8 = stored_order(cache_mem_v)

    tt_attn = min(t, 1024)
    nb_attn = min(sb, 8)

    def attend(xp, xs, layer):
        args = (xa_w_q_b[layer], xa_w_o_b[layer], vec(ln_g[layer, 1]), vec(ln_b[layer, 1]), alpha)
        return (_mem_attend(xp, mk4, mv4, layer, *args, 1, tt_attn),
                _mem_attend_cache(xs, ck8, cv8, layer, *args, nb_attn))

    pool_args = (pool_w_b[0], vec(pool_scale[0]), vec(ln_g[0, 0]), vec(ln_b[0, 0]), alpha)
    xp = _pool_prompt(x_prompt, *pool_args)
    prev = jnp.pad(state_pool[0], ((0, 0), (POOL_HALO - n_prev, 0), (0, 0)))
    xs = _pool_sample(x_sample, prev, n_prev, *pool_args)
    xp, xs = attend(xp, xs, 0)
    ffn_args = (ffn_w_gu_b, ffn_w_d_b, vec(ln_g[0, 2]), vec(ln_b[0, 2]), alpha)
    xp = _ffn_ln(xp.reshape(b * t, d), *ffn_args, 1024).reshape(b, t, d)
    xs = _ffn_ln(xs.reshape(sb * st, d), *ffn_args, 1024).reshape(sb, st, d)

    hg_args = (hg_w_in_b[0], hg_lb, vec(hg_norm_g[0]), hg_w_o_b[0], vec(ln_g[1, 0]), vec(ln_b[1, 0]), alpha, 1)
    xp, hg_p = _hgrn_tiled(xp, None, *hg_args, 2 * HG_CHUNK, 1, min(t, 256))
    xs, hg_s = _hgrn_tiled(xs, state_hgrn[0], *hg_args, math.gcd(st, HG_CHUNK), min(sb, 8), st)
    xp, xs = attend(xp, xs, 1)
    yp, ys = _moe_ln([xp.reshape(b * t, d), xs.reshape(sb * st, d)], moe_w_router[0], moe_w_gu_b[0],
                     moe_w_d_b[0], vec(ln_g[1, 2]), vec(ln_b[1, 2]), alpha, 512)

    heads_hd = (depth, b, mem_len, XA_HEADS, d // XA_HEADS)
    pool_p = x_prompt[:, t - n_prev:, :][None]
    pool_s = jnp.concatenate([state_pool[0], x_sample], axis=1)[:, st:, :][None]
    return (yp.reshape(b, t, d), ys.reshape(sb, st, d), pool_p, hg_p[None],
            mk.reshape(heads_hd), mv.reshape(heads_hd), pool_s, hg_s[None])
```

```python
import functools
import math

import jax
import jax.numpy as jnp
from jax import lax
from jax.experimental import pallas as pl
from jax.experimental.pallas import tpu as pltpu

F32 = jnp.float32
BF16 = jnp.bfloat16

POOL_WINDOWS = (2, 4, 8, 16)
POOL_HALO = 16
HG_HEADS = 8
HG_CHUNK = 32
RMS_EPS = 1e-6
XA_HEADS = 4
N_EXPERTS = 8
TOP_K = 2
LN_EPS = 1e-5
LANES = 128
VMEM_LIMIT = 62 * 1024 * 1024

_NT = (((1,), (1,)), ((), ()))
_TN = (((0,), (0,)), ((), ()))


def _cparams(*sem):
    return pltpu.CompilerParams(dimension_semantics=sem, vmem_limit_bytes=VMEM_LIMIT)


def _layer_norm(z, g, b):
    mu = jnp.mean(z, axis=-1, keepdims=True)
    zc = z - mu
    var = jnp.mean(zc * zc, axis=-1, keepdims=True)
    return zc * lax.rsqrt(var + LN_EPS) * g + b


def _dot(a, b):
    return jnp.dot(a, b, preferred_element_type=F32)


def _side_cast_plan(arrays, grid):
    steps = math.prod(grid)

    def block_index(*idx):
        lin = 0
        for g, i in zip(grid, idx):
            lin = lin * g + i
        return lin, 0

    specs, shapes = [], []
    for a in arrays:
        rows, cols = a.shape
        if rows % steps or (rows // steps) % 16:
            return None
        specs.append(pl.BlockSpec((rows // steps, cols), block_index))
        shapes.append(jax.ShapeDtypeStruct(a.shape, BF16))
    return specs, shapes


def _side_cast(in_refs, out_refs):
    for src, dst in zip(in_refs, out_refs):
        dst[...] = src[...].astype(dst.dtype)


def _kv_kernel(mem_ref, w_ref, k_ref, v_ref):
    kv = _dot(mem_ref[...].astype(BF16), w_ref[0])
    n = k_ref.shape[-1]
    k_ref[0] = kv[:, :n]
    v_ref[0] = kv[:, n:]


def _mem_kv(mem2d, w_kv):
    rows, d = mem2d.shape
    depth, _, e2 = w_kv.shape
    e = e2 // 2
    tm = min(rows, 512)
    out = jax.ShapeDtypeStruct((depth, rows, e), F32)
    return pl.pallas_call(
        _kv_kernel,
        grid=(depth, rows // tm),
        in_specs=[pl.BlockSpec((tm, d), lambda l, i: (i, 0)),
                  pl.BlockSpec((1, d, e2), lambda l, i: (l, 0, 0))],
        out_specs=[pl.BlockSpec((1, tm, e), lambda l, i: (l, i, 0)),
                   pl.BlockSpec((1, tm, e), lambda l, i: (l, i, 0))],
        out_shape=[out, out],
        compiler_params=_cparams("parallel", "parallel"),
        name="mem_kv",
    )(mem2d, w_kv)


def _window_sums(a, w):
    s, k = a, 1
    while k < w:
        s = s + pltpu.roll(s, k, 0)
        k *= 2
    return s


def _pool_mix(pooled_minus_tok, w_ref, scale):
    parts = [_dot(pooled_minus_tok[g].astype(BF16), w_ref[g]) for g in range(len(POOL_WINDOWS))]
    return jnp.concatenate(parts, axis=1) * scale


def _pool_prompt_kernel(x_ref, halo_ref, w_ref, scale_ref, g_ref, b_ref, o_ref, buf_ref, *, alpha):
    t = pl.program_id(1)
    tt = x_ref.shape[1]
    gw = w_ref.shape[-1]
    x = x_ref[0]
    buf_ref[0:POOL_HALO, :] = jnp.where(t > 0, halo_ref[0], 0.0)
    buf_ref[POOL_HALO:, :] = x
    pos = t * tt + lax.broadcasted_iota(jnp.int32, (tt, 1), 0)
    diffs = []
    for g, w in enumerate(POOL_WINDOWS):
        sl = slice(g * gw, (g + 1) * gw)
        s = _window_sums(buf_ref[:, sl], w)[POOL_HALO:, :]
        cnt = jnp.minimum(pos + 1, w).astype(F32)
        diffs.append(s / cnt - x[:, sl])
    mixed = _pool_mix(diffs, w_ref, scale_ref[...])
    o_ref[0] = _layer_norm(alpha * x + mixed, g_ref[...], b_ref[...])


def _pool_prompt(x, pool_w, scale, ln_g, ln_b, alpha):
    b, t, d = x.shape
    tt = min(t, 1024)
    hb = tt // POOL_HALO
    vec = pl.BlockSpec((1, d), lambda i, j: (0, 0))
    return pl.pallas_call(
        functools.partial(_pool_prompt_kernel, alpha=alpha),
        grid=(b, t // tt),
        in_specs=[pl.BlockSpec((1, tt, d), lambda i, j: (i, j, 0)),
                  pl.BlockSpec((1, POOL_HALO, d), lambda i, j: (i, jnp.maximum(j * hb - 1, 0), 0)),
                  pl.BlockSpec(pool_w.shape, lambda i, j: (0, 0, 0)),
                  vec, vec, vec],
        out_specs=pl.BlockSpec((1, tt, d), lambda i, j: (i, j, 0)),
        out_shape=jax.ShapeDtypeStruct(x.shape, F32),
        scratch_shapes=[pltpu.VMEM((tt + POOL_HALO, d), F32)],
        compiler_params=_cparams("parallel", "parallel"),
        name="pool_prompt",
    )(x, x, pool_w, scale, ln_g, ln_b)


def _pool_sample_kernel(x_ref, prev_ref, w_ref, scale_ref, g_ref, b_ref, o_ref, buf_ref, *, alpha, n_prev):
    nb, t, d = x_ref.shape
    gw = w_ref.shape[-1]
    rows = POOL_HALO + t
    buf_ref[:, 0:POOL_HALO, :] = prev_ref[...]
    buf_ref[:, POOL_HALO:, :] = x_ref[...]
    x = x_ref[...].reshape(nb * t, d)
    pos = lax.broadcasted_iota(jnp.int32, (nb, t, 1), 1).reshape(nb * t, 1)
    diffs = []
    for g, w in enumerate(POOL_WINDOWS):
        sl = slice(g * gw, (g + 1) * gw)
        s = _window_sums(buf_ref[:, :, sl].reshape(nb * rows, gw), w)
        s = s.reshape(nb, rows, gw)[:, POOL_HALO:, :].reshape(nb * t, gw)
        cnt = jnp.minimum(pos + 1 + n_prev, w).astype(F32)
        diffs.append(s / cnt - x[:, sl])
    mixed = _pool_mix(diffs, w_ref, scale_ref[...])
    o_ref[...] = _layer_norm(alpha * x + mixed, g_ref[...], b_ref[...]).reshape(nb, t, d)


def _pool_sample(x, prev_padded, n_prev, pool_w, scale, ln_g, ln_b, alpha):
    b, t, d = x.shape
    nb = min(b, 16)
    vec = pl.BlockSpec((1, d), lambda i: (0, 0))
    return pl.pallas_call(
        functools.partial(_pool_sample_kernel, alpha=alpha, n_prev=n_prev),
        grid=(b // nb,),
        in_specs=[pl.BlockSpec((nb, t, d), lambda i: (i, 0, 0)),
                  pl.BlockSpec((nb, POOL_HALO, d), lambda i: (i, 0, 0)),
                  pl.BlockSpec(pool_w.shape, lambda i: (0, 0, 0)),
                  vec, vec, vec],
        out_specs=pl.BlockSpec((nb, t, d), lambda i: (i, 0, 0)),
        out_shape=jax.ShapeDtypeStruct(x.shape, F32),
        scratch_shapes=[pltpu.VMEM((nb, POOL_HALO + t, d), F32)],
        compiler_params=_cparams("parallel"),
        name="pool_sample",
    )(x, prev_padded, pool_w, scale, ln_g, ln_b)


def _attn_kernel(*refs, alpha, n_side):
    x_ref, k_ref, v_ref, wq_ref, wo_ref, g_ref, b_ref = refs[:7]
    o_ref = refs[7 + n_side]
    oh_ref, s_ref, p_ref = refs[-3:]
    _side_cast(refs[7:7 + n_side], refs[8 + n_side:8 + 2 * n_side])
    nb, tt, d = x_ref.shape
    hd = d // XA_HEADS
    scale = hd ** -0.5
    x = x_ref[...].reshape(nb * tt, d)
    q = _dot(x.astype(BF16), wq_ref[...])
    for h in range(XA_HEADS):
        sl = slice(h * hd, (h + 1) * hd)
        qh = q[:, sl].astype(BF16).reshape(nb, tt, hd)
        kh = k_ref[:, :, sl].astype(BF16)
        s_ref[h] = jnp.einsum("btd,bmd->btm", qh, kh, preferred_element_type=F32) * scale
    for h in range(XA_HEADS):
        s = s_ref[h]
        p = jnp.exp(s - jnp.max(s, axis=-1, keepdims=True))
        p_ref[h] = (p / jnp.sum(p, axis=-1, keepdims=True)).astype(BF16)
    for h in range(XA_HEADS):
        sl = slice(h * hd, (h + 1) * hd)
        vh = v_ref[:, :, sl].astype(BF16)
        oh = jnp.einsum("btm,bmd->btd", p_ref[h], vh, preferred_element_type=F32)
        oh_ref[:, sl] = oh.reshape(nb * tt, hd).astype(BF16)
    a = _dot(oh_ref[...], wo_ref[...])
    o_ref[...] = _layer_norm(alpha * x + a, g_ref[...], b_ref[...]).reshape(nb, tt, d)


def _mem_attend(x, k4, v4, layer, wq, wo, ln_g, ln_b, alpha, nb, tt, side=()):
    b, t, d = x.shape
    m = k4.shape[2]
    grid = (b // nb, t // tt)
    vec = pl.BlockSpec((1, d), lambda i, j: (0, 0))
    mat = pl.BlockSpec((d, d), lambda i, j: (0, 0))
    kv = pl.BlockSpec((None, nb, m, d), lambda i, j: (layer, i, 0, 0))
    side_specs, side_shapes = _side_cast_plan(side, grid)
    out = pl.pallas_call(
        functools.partial(_attn_kernel, alpha=alpha, n_side=len(side)),
        grid=grid,
        in_specs=[pl.BlockSpec((nb, tt, d), lambda i, j: (i, j, 0)), kv, kv, mat, mat, vec, vec] + side_specs,
        out_specs=[pl.BlockSpec((nb, tt, d), lambda i, j: (i, j, 0))] + side_specs,
        out_shape=[jax.ShapeDtypeStruct(x.shape, F32)] + side_shapes,
        scratch_shapes=[pltpu.VMEM((nb * tt, d), BF16), pltpu.VMEM((XA_HEADS, nb, tt, m), F32),
                        pltpu.VMEM((XA_HEADS, nb, tt, m), BF16)],
        compiler_params=_cparams("arbitrary", "arbitrary"),
        name=f"mem_attend_l{layer}",
    )(x, k4, v4, wq, wo, ln_g, ln_b, *side)
    return out if side else out[0]


def _attn_cache_kernel(x_ref, k_ref, v_ref, wq_ref, wo_ref, g_ref, b_ref, o_ref, q_scr, oh_scr, s_scr, p_scr,
                       *, alpha):
    nb, t, d = x_ref.shape
    heads = XA_HEADS
    hd = d // heads
    halves = hd // LANES
    assert halves == 2
    rows = k_ref.shape[1]
    qr = heads * t
    scale = hd ** -0.5
    x = x_ref[...].reshape(nb * t, d)
    q = _dot(x.astype(BF16), wq_ref[...])
    for c in range(halves):
        for h in range(heads):
            col = (h * halves + c) * LANES
            q_scr[c, :, h * t:(h + 1) * t, :] = q[:, col:col + LANES].reshape(nb, t, LANES)

    col_id = lax.broadcasted_iota(jnp.int32, (qr, rows), 1)
    row_head = lax.broadcasted_iota(jnp.int32, (qr, rows), 0) // t
    col_half = (col_id // heads) % halves
    col_head = col_id % heads
    keep = jnp.logical_and(col_half == 1, col_head == row_head)

    for b in range(nb):
        kb = k_ref[b].astype(BF16)
        g0 = lax.dot_general(q_scr[0, b].astype(BF16), kb, _NT, preferred_element_type=F32)
        g1 = lax.dot_general(q_scr[1, b].astype(BF16), kb, _NT, preferred_element_type=F32)
        s_scr[b] = jnp.where(col_half == 0, g0, g1)
    for b in range(nb):
        part = s_scr[b]
        s = (part + pltpu.roll(part, heads, 1)) * scale
        s = jnp.where(keep, s, -jnp.inf)
        p = jnp.exp(s - jnp.max(s, axis=-1, keepdims=True))
        p = p / jnp.sum(p, axis=-1, keepdims=True)
        p_scr[0, b] = pltpu.roll(p, rows - heads, 1).astype(BF16)
        p_scr[1, b] = p.astype(BF16)
    for b in range(nb):
        vb = v_ref[b].astype(BF16)
        for c in range(halves):
            oc = _dot(p_scr[c, b], vb)
            for h in range(heads):
                col = (h * halves + c) * LANES
                oh_scr[b * t:(b + 1) * t, col:col + LANES] = oc[h * t:(h + 1) * t, :]
    a = _dot(oh_scr[...].astype(BF16), wo_ref[...])
    o_ref[...] = _layer_norm(alpha * x + a, g_ref[...], b_ref[...]).reshape(nb, t, d)


def _mem_attend_cache(x, k8, v8, layer, wq, wo, ln_g, ln_b, alpha, nb):
    b, t, d = x.shape
    rows = k8.shape[2]
    vec = pl.BlockSpec((1, d), lambda i: (0, 0))
    mat = pl.BlockSpec((d, d), lambda i: (0, 0))
    kv = pl.BlockSpec((None, nb, rows, LANES), lambda i: (layer, i, 0, 0))
    return pl.pallas_call(
        functools.partial(_attn_cache_kernel, alpha=alpha),
        grid=(b // nb,),
        in_specs=[pl.BlockSpec((nb, t, d), lambda i: (i, 0, 0)), kv, kv, mat, mat, vec, vec],
        out_specs=pl.BlockSpec((nb, t, d), lambda i: (i, 0, 0)),
        out_shape=jax.ShapeDtypeStruct(x.shape, F32),
        scratch_shapes=[pltpu.VMEM((2, nb, XA_HEADS * t, LANES), F32), pltpu.VMEM((nb * t, d), F32),
                        pltpu.VMEM((nb, XA_HEADS * t, rows), F32), pltpu.VMEM((2, nb, XA_HEADS * t, rows), BF16)],
        compiler_params=_cparams("parallel"),
        name=f"mem_attend_cache_l{layer}",
    )(x, k8, v8, wq, wo, ln_g, ln_b)


FFN_SUB = 512


def _swiglu(x_ref, wg_ref, wu_ref, wd_ref, h_ref):
    xb = x_ref[...].astype(BF16)
    f = h_ref.shape[1]
    for s in range(f // FFN_SUB):
        sl = slice(s * FFN_SUB, (s + 1) * FFN_SUB)
        gate = _dot(xb, wg_ref[0, :, sl])
        up = _dot(xb, wu_ref[0, :, sl])
        h_ref[:, sl] = (gate * jax.nn.sigmoid(gate) * up).astype(BF16)
    return _dot(h_ref[...], wd_ref[0])


def _ffn_ln_kernel(*refs, alpha, n_side):
    x_ref, wg_ref, wu_ref, wd_ref, g_ref, b_ref = refs[:6]
    o_ref = refs[6 + n_side]
    h_ref = refs[-1]
    _side_cast(refs[6:6 + n_side], refs[7 + n_side:7 + 2 * n_side])
    y = _swiglu(x_ref, wg_ref, wu_ref, wd_ref, h_ref)
    o_ref[...] = _layer_norm(alpha * x_ref[...] + y, g_ref[...], b_ref[...])


def _ffn_ln(x2d, w_gu, w_d, ln_g, ln_b, alpha, tm, side=()):
    n, d = x2d.shape
    tm = min(tm, n)
    f = w_d.shape[1]
    grid = (n // tm,)
    vec = pl.BlockSpec((1, d), lambda i: (0, 0))
    once = pl.Buffered(1)
    side_specs, side_shapes = _side_cast_plan(side, grid)
    out = pl.pallas_call(
        functools.partial(_ffn_ln_kernel, alpha=alpha, n_side=len(side)),
        grid=grid,
        in_specs=[pl.BlockSpec((tm, d), lambda i: (i, 0)),
                  pl.BlockSpec((1, d, f), lambda i: (0, 0, 0), pipeline_mode=once),
                  pl.BlockSpec((1, d, f), lambda i: (0, 0, 1), pipeline_mode=once),
                  pl.BlockSpec((1, f, d), lambda i: (0, 0, 0), pipeline_mode=once),
                  vec, vec] + side_specs,
        out_specs=[pl.BlockSpec((tm, d), lambda i: (i, 0))] + side_specs,
        out_shape=[jax.ShapeDtypeStruct((n, d), F32)] + side_shapes,
        scratch_shapes=[pltpu.VMEM((tm, f), BF16)],
        compiler_params=_cparams("parallel"),
        name="ffn_dense",
    )(x2d, w_gu, w_gu, w_d, ln_g, ln_b, *side)
    return out if side else out[0]


def _expert_kernel(eid_ref, used_ref, x_ref, wg_ref, wu_ref, wd_ref, o_ref, h_ref):
    i = pl.program_id(0)

    @pl.when(used_ref[i] > 0)
    def _():
        o_ref[...] = _swiglu(x_ref, wg_ref, wu_ref, wd_ref, h_ref)

    @pl.when(used_ref[i] == 0)
    def _():
        o_ref[...] = jnp.zeros_like(o_ref)


def _expert_ffn(xs, tile_eid, tile_used, w_gu, w_d, tm):
    r, d = xs.shape
    f = w_d.shape[1]
    grid_spec = pltpu.PrefetchScalarGridSpec(
        num_scalar_prefetch=2,
        grid=(r // tm,),
        in_specs=[pl.BlockSpec((tm, d), lambda i, e, u: (i, 0)),
                  pl.BlockSpec((1, d, f), lambda i, e, u: (e[i], 0, 0)),
                  pl.BlockSpec((1, d, f), lambda i, e, u: (e[i], 0, 1)),
                  pl.BlockSpec((1, f, d), lambda i, e, u: (e[i], 0, 0))],
        out_specs=pl.BlockSpec((tm, d), lambda i, e, u: (i, 0)),
        scratch_shapes=[pltpu.VMEM((tm, f), BF16)],
    )
    return pl.pallas_call(
        _expert_kernel,
        grid_spec=grid_spec,
        out_shape=jax.ShapeDtypeStruct((r, d), F32),
        compiler_params=_cparams("arbitrary"),
        name="ffn_experts",
    )(tile_eid, tile_used, xs, w_gu, w_gu, w_d)


def _split3(x):
    hi = x.astype(BF16).astype(F32)
    r = x - hi
    mid = r.astype(BF16).astype(F32)
    lo = (r - mid).astype(BF16).astype(F32)
    return hi, mid, lo


def _hgrn_tile_kernel(*refs, alpha, layer, chunk, has_state):
    st_refs = refs[-HG_HEADS:]
    refs = refs[:-HG_HEADS]
    s0_ref = None
    if has_state:
        s0_ref, refs = refs[1], refs[:1] + refs[2:]
    (x_ref, w_in_ref, lbp_ref, ng_ref, w_o_ref, g_ref, b_ref, o_ref, sfin_ref,
     proj_ref, lf_ref, bcum_ref, qrel_ref, krel_ref, kend_ref, qdec_ref, vb_ref, dec_ref,
     u_ref, oacc_ref, og_ref) = refs
    t = pl.program_id(1)
    nb, tt, d = x_ref.shape
    heads = HG_HEADS
    dv = ng_ref.shape[-1]
    vdim = heads * dv
    kdim = (w_in_ref.shape[1] - 2 * vdim) // 2
    dk = kdim // heads
    m = nb * tt
    c = chunk
    n_chunks = m // c
    chunks_per_seq = tt // c
    mid = max(c // 2 - 1, 0)

    @pl.when(t == 0)
    def _():
        for h in range(heads):
            if has_state:
                for bi in range(nb):
                    st_refs[h][bi] = s0_ref[bi, h].T
            else:
                st_refs[h][...] = jnp.zeros_like(st_refs[h])

    x = x_ref[...].reshape(m, d)
    xb = x.astype(BF16)

    lbp = lbp_ref[...]
    e = jnp.exp(lbp - jnp.max(lbp, axis=0, keepdims=True))
    sm = e / jnp.sum(e, axis=0, keepdims=True)
    lb = jnp.sum(sm[1:layer + 1], axis=0, keepdims=True)

    f_cols = slice(kdim, 2 * kdim)
    proj_ref[:, f_cols] = _dot(xb, w_in_ref[:, f_cols])
    for j in range(n_chunks):
        rows = slice(j * c, (j + 1) * c)
        f = lb + (1.0 - lb) * jax.nn.sigmoid(proj_ref[rows, f_cols])
        proj_ref[rows, f_cols] = 1.0 - f
        for p, part in enumerate(_split3(jnp.log(f))):
            lf_ref[p, rows, :] = part.astype(lf_ref.dtype)
    proj_ref[:, 0:kdim] = _dot(xb, w_in_ref[:, 0:kdim])
    proj_ref[:, 2 * kdim:] = _dot(xb, w_in_ref[:, 2 * kdim:])

    ri = lax.broadcasted_iota(jnp.int32, (m, m), 0)
    ci = lax.broadcasted_iota(jnp.int32, (m, m), 1)
    causal = jnp.logical_and(ri // c == ci // c, ri >= ci)
    tri = causal.astype(BF16)
    bcum_ref[...] = (_dot(tri, lf_ref[0].astype(BF16)) + _dot(tri, lf_ref[1].astype(BF16))
                     + _dot(tri, lf_ref[2].astype(BF16)))

    sdt = qrel_ref.dtype
    vb_ref[...] = proj_ref[:, 2 * kdim:2 * kdim + vdim].astype(sdt)
    for j in range(n_chunks):
        rows = slice(j * c, (j + 1) * c)
        bcum = bcum_ref[rows, :]
        bmid = bcum[mid:mid + 1]
        blast = bcum[c - 1:c]
        q_dec = proj_ref[rows, 0:kdim] * jnp.exp(bcum)
        k_rel = proj_ref[rows, f_cols] * jnp.exp(bmid - bcum)
        qdec_ref[rows, :] = q_dec.astype(sdt)
        qrel_ref[rows, :] = (q_dec * jnp.exp(-bmid)).astype(sdt)
        krel_ref[rows, :] = k_rel.astype(sdt)
        kend_ref[rows, :] = (k_rel * jnp.exp(blast - bmid)).astype(sdt)
        dec_ref[j:j + 1, :] = jnp.exp(blast)

    for h in range(heads):
        sk = slice(h * dk, (h + 1) * dk)
        sv = slice(h * dv, (h + 1) * dv)
        a = lax.dot_general(qrel_ref[:, sk].astype(BF16), krel_ref[:, sk].astype(BF16), _NT,
                            preferred_element_type=F32)
        a = jnp.where(causal, a, 0.0).astype(BF16)
        oacc_ref[:, sv] = _dot(a, vb_ref[:, sv].astype(BF16))

    for j in range(n_chunks):
        rows = slice(j * c, (j + 1) * c)
        for h in range(heads):
            sk = slice(h * dk, (h + 1) * dk)
            sv = slice(h * dv, (h + 1) * dv)
            u_ref[j, h] = lax.dot_general(vb_ref[rows, sv].astype(BF16), kend_ref[rows, sk].astype(BF16), _TN,
                                          preferred_element_type=F32)

    for j in range(n_chunks):
        rows = slice(j * c, (j + 1) * c)
        bi = j // chunks_per_seq
        for h in range(heads):
            sk = slice(h * dk, (h + 1) * dk)
            sv = slice(h * dv, (h + 1) * dv)
            st = st_refs[h][bi]
            oacc_ref[rows, sv] += lax.dot_general(qdec_ref[rows, sk].astype(BF16), st.astype(BF16), _NT,
                                                  preferred_element_type=F32)
            st_refs[h][bi] = st * dec_ref[j:j + 1, sk] + u_ref[j, h]

    ng = ng_ref[...]
    for h in range(heads):
        sv = slice(h * dv, (h + 1) * dv)
        o = oacc_ref[:, sv]
        o = o * lax.rsqrt(jnp.mean(o * o, axis=-1, keepdims=True) + RMS_EPS) * ng
        gate = proj_ref[:, 2 * kdim + vdim + h * dv:2 * kdim + vdim + (h + 1) * dv]
        og_ref[:, sv] = (o * (gate * jax.nn.sigmoid(gate))).astype(BF16)
    out = _dot(og_ref[...], w_o_ref[...])
    o_ref[...] = _layer_norm(alpha * x + out, g_ref[...], b_ref[...]).reshape(nb, tt, d)

    @pl.when(t == pl.num_programs(1) - 1)
    def _():
        for bi in range(nb):
            for h in range(heads):
                sfin_ref[bi, h] = st_refs[h][bi].T


def _hgrn_tiled(x, s0, w_in, lb_param, norm_g, w_o, ln_g, ln_b, alpha, layer, chunk, nb, tt):
    b, t, d = x.shape
    heads = HG_HEADS
    dv = norm_g.shape[-1]
    vdim = heads * dv
    kdim = (w_in.shape[1] - 2 * vdim) // 2
    dk = kdim // heads
    m = nb * tt
    n_chunks = m // chunk
    has_state = s0 is not None
    const2 = lambda i, j: (0, 0)
    state_spec = pl.BlockSpec((nb, heads, dk, dv), lambda i, j: (i, 0, 0, 0))
    in_specs = [pl.BlockSpec((nb, tt, d), lambda i, j: (i, j, 0))]
    args = [x]
    if has_state:
        in_specs.append(state_spec)
        args.append(s0)
    in_specs += [pl.BlockSpec(w_in.shape, const2), pl.BlockSpec(lb_param.shape, const2),
                 pl.BlockSpec(norm_g.shape, const2), pl.BlockSpec(w_o.shape, const2),
                 pl.BlockSpec((1, d), const2), pl.BlockSpec((1, d), const2)]
    args += [w_in, lb_param, norm_g, w_o, ln_g, ln_b]
    sdt = BF16 if chunk % 16 == 0 else F32
    scratch = [pltpu.VMEM((m, w_in.shape[1]), F32),
               pltpu.VMEM((3, m, kdim), sdt),
               pltpu.VMEM((m, kdim), F32),
               pltpu.VMEM((m, kdim), sdt),
               pltpu.VMEM((m, kdim), sdt),
               pltpu.VMEM((m, kdim), sdt),
               pltpu.VMEM((m, kdim), sdt),
               pltpu.VMEM((m, vdim), sdt),
               pltpu.VMEM((n_chunks, kdim), F32),
               pltpu.VMEM((n_chunks, heads, dv, dk), F32),
               pltpu.VMEM((m, vdim), F32),
               pltpu.VMEM((m, vdim), BF16)]
    scratch += [pltpu.VMEM((nb, dv, dk), F32) for _ in range(heads)]
    return pl.pallas_call(
        functools.partial(_hgrn_tile_kernel, alpha=alpha, layer=layer, chunk=chunk, has_state=has_state),
        grid=(b // nb, t // tt),
        in_specs=in_specs,
        out_specs=[pl.BlockSpec((nb, tt, d), lambda i, j: (i, j, 0)), state_spec],
        out_shape=[jax.ShapeDtypeStruct(x.shape, F32),
                   jax.ShapeDtypeStruct((b, heads, dk, dv), F32)],
        scratch_shapes=scratch,
        compiler_params=_cparams("parallel", "arbitrary"),
        name="hgrn_state" if has_state else "hgrn_fresh",
    )(*args)


_I_E0, _I_E1, _I_G0, _I_G1, _I_R0, _I_R1 = range(6)


def _router_kernel(x_ref, w_ref, cnt0_ref, info_ref, cnt_ref, carry_ref):
    i = pl.program_id(0)
    tm = x_ref.shape[0]

    @pl.when(i == 0)
    def _():
        carry_ref[...] = cnt0_ref[...]

    x = x_ref[...]
    w = w_ref[...]
    x_hi = x.astype(BF16)
    x_lo = (x - x_hi.astype(F32)).astype(BF16)
    w_hi = w.astype(BF16)
    w_lo = (w - w_hi.astype(F32)).astype(BF16)
    logits = _dot(x_hi, w_hi) + (_dot(x_hi, w_lo) + _dot(x_lo, w_hi))
    lane = lax.broadcasted_iota(jnp.int32, (tm, LANES), 1)
    valid = lane < N_EXPERTS
    logits = jnp.where(valid, logits, -jnp.inf)
    p = jnp.exp(logits - jnp.max(logits, axis=-1, keepdims=True))
    p = p / jnp.sum(p, axis=-1, keepdims=True)
    p0 = jnp.where(valid, p, -1.0)
    v0 = jnp.max(p0, axis=-1, keepdims=True)
    e0 = jnp.min(jnp.where(p0 == v0, lane, LANES), axis=-1, keepdims=True)
    p1 = jnp.where(lane == e0, -1.0, p0)
    v1 = jnp.max(p1, axis=-1, keepdims=True)
    e1 = jnp.min(jnp.where(p1 == v1, lane, LANES), axis=-1, keepdims=True)
    den = v0 + v1
    sel = jnp.logical_or(lane == e0, lane == e1)
    ri = lax.broadcasted_iota(jnp.int32, (tm, tm), 0)
    ci = lax.broadcasted_iota(jnp.int32, (tm, tm), 1)
    before = (ri > ci).astype(BF16)
    rank = _dot(before, sel.astype(BF16)) + carry_ref[...]
    r0 = jnp.sum(jnp.where(lane == e0, rank, 0.0), axis=-1, keepdims=True)
    r1 = jnp.sum(jnp.where(lane == e1, rank, 0.0), axis=-1, keepdims=True)
    carry_ref[...] += jnp.sum(sel.astype(F32), axis=0, keepdims=True)
    info = jnp.zeros((tm, LANES), F32)
    for idx, val in ((_I_E0, e0.astype(F32)), (_I_E1, e1.astype(F32)), (_I_G0, v0 / den),
                     (_I_G1, v1 / den), (_I_R0, r0), (_I_R1, r1)):
        info = jnp.where(lane == idx, val, info)
    info_ref[...] = info
    cnt_ref[...] = carry_ref[...]


def _router(x2d, w_router_padded, counts_in, tm):
    n, d = x2d.shape
    return pl.pallas_call(
        _router_kernel,
        grid=(n // tm,),
        in_specs=[pl.BlockSpec((tm, d), lambda i: (i, 0)),
                  pl.BlockSpec((d, LANES), lambda i: (0, 0)),
                  pl.BlockSpec((1, LANES), lambda i: (0, 0))],
        out_specs=[pl.BlockSpec((tm, LANES), lambda i: (i, 0)),
                   pl.BlockSpec((1, LANES), lambda i: (0, 0))],
        out_shape=[jax.ShapeDtypeStruct((n, LANES), F32), jax.ShapeDtypeStruct((1, LANES), F32)],
        scratch_shapes=[pltpu.VMEM((1, LANES), F32)],
        compiler_params=_cparams("arbitrary"),
        name="moe_router",
    )(x2d, w_router_padded, counts_in)


def _row_copy(src, src_row, dst, dst_row, sem):
    return pltpu.make_async_copy(src.at[pl.ds(src_row, 1)], dst.at[pl.ds(dst_row, 1)], sem)


def _scatter_rows(pos_ref, x_ref, out_hbm, sem):
    tb = x_ref.shape[0]

    def issue(j, carry):
        for kk in range(TOP_K):
            _row_copy(x_ref, j, out_hbm, pos_ref[TOP_K * j + kk], sem).start()
        return carry

    lax.fori_loop(0, tb, issue, 0, unroll=4)
    for _ in range(TOP_K):
        pltpu.make_async_copy(x_ref, out_hbm.at[pl.ds(0, tb)], sem).wait()


def _dispatch_kernel(pos_ref, fill_ref, xa_ref, xb_ref, out_hbm, zero_ref, sem, zsem, *, steps_a):
    i = pl.program_id(0)

    @pl.when(i == 0)
    def _():
        zero_ref[...] = jnp.zeros_like(zero_ref)
        tz = zero_ref.shape[0]

        def fill(e):
            row = pl.multiple_of(jnp.maximum(fill_ref[e], 0), tz)
            return pltpu.make_async_copy(zero_ref, out_hbm.at[pl.ds(row, tz)], zsem)

        for e in range(fill_ref.shape[0]):
            pl.when(fill_ref[e] >= 0)(lambda e=e: fill(e).start())
        for e in range(fill_ref.shape[0]):
            pl.when(fill_ref[e] >= 0)(lambda e=e: fill(e).wait())

    pl.when(i < steps_a)(lambda: _scatter_rows(pos_ref, xa_ref, out_hbm, sem))
    pl.when(i >= steps_a)(lambda: _scatter_rows(pos_ref, xb_ref, out_hbm, sem))


def _dispatch(xa, xb, pos_flat, fill_rows, n_rows, tz):
    d = xa.shape[1]
    tb = min(1024, xa.shape[0], xb.shape[0])
    steps_a = xa.shape[0] // tb
    steps_b = xb.shape[0] // tb
    return pl.pallas_call(
        functools.partial(_dispatch_kernel, steps_a=steps_a),
        grid=(steps_a + steps_b,),
        in_specs=[pl.BlockSpec((TOP_K * tb,), lambda i: (i,), memory_space=pltpu.SMEM),
                  pl.BlockSpec(memory_space=pltpu.SMEM),
                  pl.BlockSpec((tb, d), lambda i: (jnp.minimum(i, steps_a - 1), 0)),
                  pl.BlockSpec((tb, d), lambda i: (jnp.maximum(i - steps_a, 0), 0))],
        out_specs=pl.BlockSpec(memory_space=pl.ANY),
        out_shape=jax.ShapeDtypeStruct((n_rows, d), xa.dtype),
        scratch_shapes=[pltpu.VMEM((tz, d), xa.dtype), pltpu.SemaphoreType.DMA(()),
                        pltpu.SemaphoreType.DMA(())],
        compiler_params=_cparams("arbitrary"),
        name="moe_dispatch",
    )(pos_flat, fill_rows, xa, xb)


def _combine_kernel(pos_ref, posn_ref, x_ref, info_ref, y_hbm, g_ref, b_ref, o_ref, ybuf, sem, *, alpha):
    i = pl.program_id(0)
    n = pl.num_programs(0)
    tm = x_ref.shape[0]

    def issue(p_ref, slot):
        def body(j, carry):
            for kk in range(TOP_K):
                _row_copy(y_hbm, p_ref[TOP_K * j + kk], ybuf.at[slot, kk], j, sem.at[slot]).start()
            return carry
        lax.fori_loop(0, tm, body, 0, unroll=8)

    @pl.when(i == 0)
    def _():
        issue(pos_ref, 0)

    @pl.when(i + 1 < n)
    def _():
        issue(posn_ref, (i + 1) % 2)

    slot = i % 2
    for kk in range(TOP_K):
        pltpu.make_async_copy(y_hbm.at[pl.ds(0, tm)], ybuf.at[slot, kk], sem.at[slot]).wait()
    info = info_ref[...]
    moe = info[:, _I_G0:_I_G0 + 1] * ybuf[slot, 0] + info[:, _I_G1:_I_G1 + 1] * ybuf[slot, 1]
    o_ref[...] = _layer_norm(alpha * x_ref[...] + moe, g_ref[...], b_ref[...])


def _combine_ln(x2d, info, pos_flat, y, ln_g, ln_b, alpha, tm):
    n, d = x2d.shape
    nt = n // tm
    vec = pl.BlockSpec((1, d), lambda i: (0, 0))
    return pl.pallas_call(
        functools.partial(_combine_kernel, alpha=alpha),
        grid=(nt,),
        in_specs=[pl.BlockSpec((TOP_K * tm,), lambda i: (i,), memory_space=pltpu.SMEM),
                  pl.BlockSpec((TOP_K * tm,), lambda i: (jnp.minimum(i + 1, nt - 1),), memory_space=pltpu.SMEM),
                  pl.BlockSpec((tm, d), lambda i: (i, 0)),
                  pl.BlockSpec((tm, LANES), lambda i: (i, 0)),
                  pl.BlockSpec(memory_space=pl.ANY),
                  vec, vec],
        out_specs=pl.BlockSpec((tm, d), lambda i: (i, 0)),
        out_shape=jax.ShapeDtypeStruct((n, d), F32),
        scratch_shapes=[pltpu.VMEM((2, TOP_K, tm, d), F32), pltpu.SemaphoreType.DMA((2,))],
        compiler_params=_cparams("arbitrary"),
        name="moe_combine",
    )(pos_flat, pos_flat, x2d, info, y, ln_g, ln_b)


def _moe_ln(xs2d, w_router, w_gu, w_d, ln_g, ln_b, alpha, tm_expert):
    d = xs2d[0].shape[1]
    n_total = sum(x.shape[0] for x in xs2d)
    w_r = jnp.pad(w_router, ((0, 0), (0, LANES - N_EXPERTS)))
    counts = jnp.zeros((1, LANES), F32)
    infos = []
    for x in xs2d:
        info, counts = _router(x, w_r, counts, min(x.shape[0], 1024))
        infos.append(info)
    cnt = counts[0, :N_EXPERTS].astype(jnp.int32)
    padded = (cnt + tm_expert - 1) // tm_expert * tm_expert
    ends = jnp.cumsum(padded)
    offs = ends - padded
    n_rows = -(-(TOP_K * n_total) // tm_expert) * tm_expert + N_EXPERTS * tm_expert
    tile_start = jnp.arange(n_rows // tm_expert, dtype=jnp.int32) * tm_expert
    tile_eid = jnp.minimum(jnp.sum(tile_start[:, None] >= ends[None, :], axis=1), N_EXPERTS - 1).astype(jnp.int32)
    tile_used = (tile_start < offs[tile_eid] + cnt[tile_eid]).astype(jnp.int32)
    tail = ends[-1] + jnp.arange(N_EXPERTS, dtype=jnp.int32) * tm_expert
    fill_rows = jnp.concatenate([jnp.where(padded > 0, ends - tm_expert, -1),
                                 jnp.where(tail < n_rows, tail, -1)]).astype(jnp.int32)
    poss = []
    for info in infos:
        eid = info[:, _I_E0:_I_E1 + 1].astype(jnp.int32)
        rank = info[:, _I_R0:_I_R1 + 1].astype(jnp.int32)
        poss.append((offs[eid] + rank).reshape(-1))
    xa, xb = xs2d
    buf = _dispatch(xa, xb, jnp.concatenate(poss), fill_rows, n_rows, tm_expert)
    y = _expert_ffn(buf, tile_eid, tile_used, w_gu, w_d, tm_expert)
    return [_combine_ln(x, info, pos, y, ln_g, ln_b, alpha, min(x.shape[0], 1024))
            for x, info, pos in zip(xs2d, infos, poss)]


def kernel(x_prompt, x_sample, mem_prompt, state_pool, state_hgrn, cache_mem_k, cache_mem_v, pool_w, pool_scale, hg_w_in, hg_lb, hg_norm_g, hg_w_o, xa_w_q, xa_w_kv, xa_w_o, ffn_w_gu, ffn_w_d, moe_w_router, moe_w_gu, moe_w_d, ln_g, ln_b):
    b, t, d = x_prompt.shape
    sb, st, _ = x_sample.shape
    depth = ln_g.shape[0]
    mem_len = mem_prompt.shape[1]
    alpha = (2 * depth) ** 0.25
    n_prev = state_pool.shape[2]

    pool_w_b = pool_w.astype(BF16)
    hg_w_in_b = hg_w_in.astype(BF16)
    hg_w_o_b = hg_w_o.astype(BF16)
    xa_w_q_b = xa_w_q.astype(BF16)
    xa_w_kv_b = xa_w_kv.astype(BF16)
    xa_w_o_b = xa_w_o.astype(BF16)

    def vec(a):
        return a.reshape(1, -1)

    mk, mv = _mem_kv(mem_prompt.reshape(b * mem_len, d), xa_w_kv_b)
    mk4 = mk.reshape(depth, b, mem_len, d)
    mv4 = mv.reshape(depth, b, mem_len, d)

    def stored_order(c):
        halves = c.shape[4] // LANES
        c = c.reshape(depth, sb, mem_len, XA_HEADS, halves, LANES).transpose(0, 1, 2, 4, 3, 5)
        return c.reshape(depth, sb, mem_len * halves * XA_HEADS, LANES)

    ck8 = stored_order(cache_mem_k)
    cv8 = stored_order(cache_mem_v)

    tt_attn = min(t, 1024)
    nb_attn = min(sb, 8)

    def attend(xp, xs, layer, side=()):
        args = (xa_w_q_b[layer], xa_w_o_b[layer], vec(ln_g[layer, 1]), vec(ln_b[layer, 1]), alpha)
        return (_mem_attend(xp, mk4, mv4, layer, *args, 1, tt_attn, side),
                _mem_attend_cache(xs, ck8, cv8, layer, *args, nb_attn))

    def cast_beside(weights, grid):
        views = tuple(w.reshape(-1, w.shape[-1]) for w in weights)
        if _side_cast_plan(views, grid) is None:
            return (), tuple(w.astype(BF16) for w in weights)
        return views, None

    pool_args = (pool_w_b[0], vec(pool_scale[0]), vec(ln_g[0, 0]), vec(ln_b[0, 0]), alpha)
    xp = _pool_prompt(x_prompt, *pool_args)
    prev = jnp.pad(state_pool[0], ((0, 0), (POOL_HALO - n_prev, 0), (0, 0)))
    xs = _pool_sample(x_sample, prev, n_prev, *pool_args)
    ffn_w = (ffn_w_gu, ffn_w_d)
    side, ffn_w_b = cast_beside(ffn_w, (b, t // tt_attn))
    xp, xs = attend(xp, xs, 0, side)
    if side:
        xp, ffn_w_b = xp[0], tuple(c.reshape(w.shape) for c, w in zip(xp[1:], ffn_w))
    ffn_args = (*ffn_w_b, vec(ln_g[0, 2]), vec(ln_b[0, 2]), alpha)
    moe_w = (moe_w_gu, moe_w_d)
    tm_ffn = min(256, b * t)
    side, moe_w_b = cast_beside(moe_w, (b * t // tm_ffn,))
    xp = _ffn_ln(xp.reshape(b * t, d), *ffn_args, tm_ffn, side)
    if side:
        xp, moe_w_b = xp[0], tuple(c.reshape(w.shape) for c, w in zip(xp[1:], moe_w))
    xp = xp.reshape(b, t, d)
    xs = _ffn_ln(xs.reshape(sb * st, d), *ffn_args, 1024).reshape(sb, st, d)

    hg_args = (hg_w_in_b[0], hg_lb, vec(hg_norm_g[0]), hg_w_o_b[0], vec(ln_g[1, 0]), vec(ln_b[1, 0]), alpha, 1)
    xp, hg_p = _hgrn_tiled(xp, None, *hg_args, 2 * HG_CHUNK, 1, min(t, 256))
    xs, hg_s = _hgrn_tiled(xs, state_hgrn[0], *hg_args, math.gcd(st, HG_CHUNK), min(sb, 8), st)
    xp, xs = attend(xp, xs, 1)
    yp, ys = _moe_ln([xp.reshape(b * t, d), xs.reshape(sb * st, d)], moe_w_router[0], moe_w_b[0][0],
                     moe_w_b[1][0], vec(ln_g[1, 2]), vec(ln_b[1, 2]), alpha, 512)

    heads_hd = (depth, b, mem_len, XA_HEADS, d // XA_HEADS)
    pool_p = x_prompt[:, t - n_prev:, :][None]
    pool_s = jnp.concatenate([state_pool[0], x_sample], axis=1)[:, st:, :][None]
    return (yp.reshape(b, t, d), ys.reshape(sb, st, d), pool_p, hg_p[None],
            mk.reshape(heads_hd), mv.reshape(heads_hd), pool_s, hg_s[None])
```

```python
import functools
import math

import jax
import jax.numpy as jnp
from jax import lax
from jax.experimental import pallas as pl
from jax.experimental.pallas import tpu as pltpu

F32 = jnp.float32
BF16 = jnp.bfloat16

POOL_WINDOWS = (2, 4, 8, 16)
POOL_HALO = 16
HG_HEADS = 8
HG_CHUNK = 32
RMS_EPS = 1e-6
XA_HEADS = 4
N_EXPERTS = 8
TOP_K = 2
LN_EPS = 1e-5
LANES = 128
VMEM_LIMIT = 62 * 1024 * 1024

_NT = (((1,), (1,)), ((), ()))
_TN = (((0,), (0,)), ((), ()))


def _cparams(*sem):
    return pltpu.CompilerParams(dimension_semantics=sem, vmem_limit_bytes=VMEM_LIMIT)


def _layer_norm(z, g, b):
    mu = jnp.mean(z, axis=-1, keepdims=True)
    zc = z - mu
    var = jnp.mean(zc * zc, axis=-1, keepdims=True)
    return zc * lax.rsqrt(var + LN_EPS) * g + b


def _dot(a, b):
    return jnp.dot(a, b, preferred_element_type=F32)


def _side_cast_plan(arrays, grid):
    steps = math.prod(grid)

    def block_index(*idx):
        lin = 0
        for g, i in zip(grid, idx):
            lin = lin * g + i
        return lin, 0

    specs, shapes = [], []
    for a in arrays:
        rows, cols = a.shape
        if rows % steps or (rows // steps) % 16:
            return None
        specs.append(pl.BlockSpec((rows // steps, cols), block_index))
        shapes.append(jax.ShapeDtypeStruct(a.shape, BF16))
    return specs, shapes


def _side_cast(in_refs, out_refs):
    for src, dst in zip(in_refs, out_refs):
        dst[...] = src[...].astype(dst.dtype)


def _kv_kernel(mem_ref, w_ref, k_ref, v_ref):
    kv = _dot(mem_ref[...].astype(BF16), w_ref[0])
    n = k_ref.shape[-1]
    k_ref[0] = kv[:, :n]
    v_ref[0] = kv[:, n:]


def _mem_kv(mem2d, w_kv):
    rows, d = mem2d.shape
    depth, _, e2 = w_kv.shape
    e = e2 // 2
    tm = min(rows, 512)
    out = jax.ShapeDtypeStruct((depth, rows, e), F32)
    return pl.pallas_call(
        _kv_kernel,
        grid=(depth, rows // tm),
        in_specs=[pl.BlockSpec((tm, d), lambda l, i: (i, 0)),
                  pl.BlockSpec((1, d, e2), lambda l, i: (l, 0, 0))],
        out_specs=[pl.BlockSpec((1, tm, e), lambda l, i: (l, i, 0)),
                   pl.BlockSpec((1, tm, e), lambda l, i: (l, i, 0))],
        out_shape=[out, out],
        compiler_params=_cparams("parallel", "parallel"),
        name="mem_kv",
    )(mem2d, w_kv)


def _window_sums(a, w):
    s, k = a, 1
    while k < w:
        s = s + pltpu.roll(s, k, 0)
        k *= 2
    return s


def _pool_mix(pooled_minus_tok, w_ref, scale):
    parts = [_dot(pooled_minus_tok[g].astype(BF16), w_ref[g]) for g in range(len(POOL_WINDOWS))]
    return jnp.concatenate(parts, axis=1) * scale


def _pool_prompt_kernel(x_ref, halo_ref, w_ref, scale_ref, g_ref, b_ref, o_ref, buf_ref, *, alpha):
    t = pl.program_id(1)
    tt = x_ref.shape[1]
    gw = w_ref.shape[-1]
    x = x_ref[0]
    buf_ref[0:POOL_HALO, :] = jnp.where(t > 0, halo_ref[0], 0.0)
    buf_ref[POOL_HALO:, :] = x
    pos = t * tt + lax.broadcasted_iota(jnp.int32, (tt, 1), 0)
    diffs = []
    for g, w in enumerate(POOL_WINDOWS):
        sl = slice(g * gw, (g + 1) * gw)
        s = _window_sums(buf_ref[:, sl], w)[POOL_HALO:, :]
        cnt = jnp.minimum(pos + 1, w).astype(F32)
        diffs.append(s / cnt - x[:, sl])
    mixed = _pool_mix(diffs, w_ref, scale_ref[...])
    o_ref[0] = _layer_norm(alpha * x + mixed, g_ref[...], b_ref[...])


def _pool_prompt(x, pool_w, scale, ln_g, ln_b, alpha):
    b, t, d = x.shape
    tt = min(t, 1024)
    hb = tt // POOL_HALO
    vec = pl.BlockSpec((1, d), lambda i, j: (0, 0))
    return pl.pallas_call(
        functools.partial(_pool_prompt_kernel, alpha=alpha),
        grid=(b, t // tt),
        in_specs=[pl.BlockSpec((1, tt, d), lambda i, j: (i, j, 0)),
                  pl.BlockSpec((1, POOL_HALO, d), lambda i, j: (i, jnp.maximum(j * hb - 1, 0), 0)),
                  pl.BlockSpec(pool_w.shape, lambda i, j: (0, 0, 0)),
                  vec, vec, vec],
        out_specs=pl.BlockSpec((1, tt, d), lambda i, j: (i, j, 0)),
        out_shape=jax.ShapeDtypeStruct(x.shape, F32),
        scratch_shapes=[pltpu.VMEM((tt + POOL_HALO, d), F32)],
        compiler_params=_cparams("parallel", "parallel"),
        name="pool_prompt",
    )(x, x, pool_w, scale, ln_g, ln_b)


def _pool_sample_kernel(x_ref, prev_ref, w_ref, scale_ref, g_ref, b_ref, o_ref, buf_ref, *, alpha, n_prev):
    nb, t, d = x_ref.shape
    gw = w_ref.shape[-1]
    rows = POOL_HALO + t
    buf_ref[:, 0:POOL_HALO, :] = prev_ref[...]
    buf_ref[:, POOL_HALO:, :] = x_ref[...]
    x = x_ref[...].reshape(nb * t, d)
    pos = lax.broadcasted_iota(jnp.int32, (nb, t, 1), 1).reshape(nb * t, 1)
    diffs = []
    for g, w in enumerate(POOL_WINDOWS):
        sl = slice(g * gw, (g + 1) * gw)
        s = _window_sums(buf_ref[:, :, sl].reshape(nb * rows, gw), w)
        s = s.reshape(nb, rows, gw)[:, POOL_HALO:, :].reshape(nb * t, gw)
        cnt = jnp.minimum(pos + 1 + n_prev, w).astype(F32)
        diffs.append(s / cnt - x[:, sl])
    mixed = _pool_mix(diffs, w_ref, scale_ref[...])
    o_ref[...] = _layer_norm(alpha * x + mixed, g_ref[...], b_ref[...]).reshape(nb, t, d)


def _pool_sample(x, prev_padded, n_prev, pool_w, scale, ln_g, ln_b, alpha):
    b, t, d = x.shape
    nb = min(b, 16)
    vec = pl.BlockSpec((1, d), lambda i: (0, 0))
    return pl.pallas_call(
        functools.partial(_pool_sample_kernel, alpha=alpha, n_prev=n_prev),
        grid=(b // nb,),
        in_specs=[pl.BlockSpec((nb, t, d), lambda i: (i, 0, 0)),
                  pl.BlockSpec((nb, POOL_HALO, d), lambda i: (i, 0, 0)),
                  pl.BlockSpec(pool_w.shape, lambda i: (0, 0, 0)),
                  vec, vec, vec],
        out_specs=pl.BlockSpec((nb, t, d), lambda i: (i, 0, 0)),
        out_shape=jax.ShapeDtypeStruct(x.shape, F32),
        scratch_shapes=[pltpu.VMEM((nb, POOL_HALO + t, d), F32)],
        compiler_params=_cparams("parallel"),
        name="pool_sample",
    )(x, prev_padded, pool_w, scale, ln_g, ln_b)


def _attn_kernel(*refs, alpha, n_side):
    x_ref, k_ref, v_ref, wq_ref, wo_ref, g_ref, b_ref = refs[:7]
    o_ref = refs[7 + n_side]
    oh_ref, s_ref, p_ref = refs[-3:]
    _side_cast(refs[7:7 + n_side], refs[8 + n_side:8 + 2 * n_side])
    tt, d = x_ref.shape
    hd = d // XA_HEADS
    scale = hd ** -0.5
    heads = [slice(h * hd, (h + 1) * hd) for h in range(XA_HEADS)]
    n_parts = 2 if tt % 32 == 0 else 1
    parts = [slice(r * (tt // n_parts), (r + 1) * (tt // n_parts)) for r in range(n_parts)]
    xs = [x_ref[rows, :] for rows in parts]
    for rows, x in zip(parts, xs):
        oh_ref[rows, :] = _dot(x.astype(BF16), wq_ref[...]).astype(BF16)
    for rows in parts:
        for h, sl in enumerate(heads):
            s_ref[h, rows, :] = lax.dot_general(oh_ref[rows, sl], k_ref[:, sl].astype(BF16), _NT,
                                                preferred_element_type=F32) * scale
    for rows in parts:
        for h in range(XA_HEADS):
            s = s_ref[h, rows, :]
            p = jnp.exp(s - jnp.max(s, axis=-1, keepdims=True))
            p_ref[h, rows, :] = (p / jnp.sum(p, axis=-1, keepdims=True)).astype(BF16)
    for rows in parts:
        for h, sl in enumerate(heads):
            oh_ref[rows, sl] = _dot(p_ref[h, rows, :], v_ref[:, sl].astype(BF16)).astype(BF16)
    for rows, x in zip(parts, xs):
        a = _dot(oh_ref[rows, :], wo_ref[...])
        o_ref[rows, :] = _layer_norm(alpha * x + a, g_ref[...], b_ref[...])


def _mem_attend(x, k4, v4, layer, wq, wo, ln_g, ln_b, alpha, tt, side=()):
    b, t, d = x.shape
    m = k4.shape[2]
    grid = (b, t // tt)
    vec = pl.BlockSpec((1, d), lambda i, j: (0, 0))
    mat = pl.BlockSpec((d, d), lambda i, j: (0, 0))
    kv = pl.BlockSpec((None, None, m, d), lambda i, j: (layer, i, 0, 0))
    side_specs, side_shapes = _side_cast_plan(side, grid)
    out = pl.pallas_call(
        functools.partial(_attn_kernel, alpha=alpha, n_side=len(side)),
        grid=grid,
        in_specs=[pl.BlockSpec((None, tt, d), lambda i, j: (i, j, 0)), kv, kv, mat, mat, vec, vec] + side_specs,
        out_specs=[pl.BlockSpec((None, tt, d), lambda i, j: (i, j, 0))] + side_specs,
        out_shape=[jax.ShapeDtypeStruct(x.shape, F32)] + side_shapes,
        scratch_shapes=[pltpu.VMEM((tt, d), BF16), pltpu.VMEM((XA_HEADS, tt, m), F32),
                        pltpu.VMEM((XA_HEADS, tt, m), BF16)],
        compiler_params=_cparams("arbitrary", "arbitrary"),
        name=f"mem_attend_l{layer}",
    )(x, k4, v4, wq, wo, ln_g, ln_b, *side)
    return out if side else out[0]


def _attn_cache_kernel(x_ref, k_ref, v_ref, wq_ref, wo_ref, g_ref, b_ref, o_ref, q_scr, oh_scr, s_scr, p_scr,
                       *, alpha):
    nb, t, d = x_ref.shape
    heads = XA_HEADS
    hd = d // heads
    halves = hd // LANES
    assert halves == 2
    rows = k_ref.shape[1]
    qr = heads * t
    scale = hd ** -0.5
    x = x_ref[...].reshape(nb * t, d)
    q = _dot(x.astype(BF16), wq_ref[...])
    for c in range(halves):
        for h in range(heads):
            col = (h * halves + c) * LANES
            q_scr[c, :, h * t:(h + 1) * t, :] = q[:, col:col + LANES].reshape(nb, t, LANES)

    col_id = lax.broadcasted_iota(jnp.int32, (qr, rows), 1)
    row_head = lax.broadcasted_iota(jnp.int32, (qr, rows), 0) // t
    col_half = (col_id // heads) % halves
    col_head = col_id % heads
    keep = jnp.logical_and(col_half == 1, col_head == row_head)

    for b in range(nb):
        kb = k_ref[b].astype(BF16)
        g0 = lax.dot_general(q_scr[0, b].astype(BF16), kb, _NT, preferred_element_type=F32)
        g1 = lax.dot_general(q_scr[1, b].astype(BF16), kb, _NT, preferred_element_type=F32)
        s_scr[b] = jnp.where(col_half == 0, g0, g1)
    for b in range(nb):
        part = s_scr[b]
        s = (part + pltpu.roll(part, heads, 1)) * scale
        s = jnp.where(keep, s, -jnp.inf)
        p = jnp.exp(s - jnp.max(s, axis=-1, keepdims=True))
        p = p / jnp.sum(p, axis=-1, keepdims=True)
        p_scr[0, b] = pltpu.roll(p, rows - heads, 1).astype(BF16)
        p_scr[1, b] = p.astype(BF16)
    for b in range(nb):
        vb = v_ref[b].astype(BF16)
        for c in range(halves):
            oc = _dot(p_scr[c, b], vb)
            for h in range(heads):
                col = (h * halves + c) * LANES
                oh_scr[b * t:(b + 1) * t, col:col + LANES] = oc[h * t:(h + 1) * t, :]
    a = _dot(oh_scr[...].astype(BF16), wo_ref[...])
    o_ref[...] = _layer_norm(alpha * x + a, g_ref[...], b_ref[...]).reshape(nb, t, d)


def _mem_attend_cache(x, k8, v8, layer, wq, wo, ln_g, ln_b, alpha, nb):
    b, t, d = x.shape
    rows = k8.shape[2]
    vec = pl.BlockSpec((1, d), lambda i: (0, 0))
    mat = pl.BlockSpec((d, d), lambda i: (0, 0))
    kv = pl.BlockSpec((None, nb, rows, LANES), lambda i: (layer, i, 0, 0))
    return pl.pallas_call(
        functools.partial(_attn_cache_kernel, alpha=alpha),
        grid=(b // nb,),
        in_specs=[pl.BlockSpec((nb, t, d), lambda i: (i, 0, 0)), kv, kv, mat, mat, vec, vec],
        out_specs=pl.BlockSpec((nb, t, d), lambda i: (i, 0, 0)),
        out_shape=jax.ShapeDtypeStruct(x.shape, F32),
        scratch_shapes=[pltpu.VMEM((2, nb, XA_HEADS * t, LANES), F32), pltpu.VMEM((nb * t, d), F32),
                        pltpu.VMEM((nb, XA_HEADS * t, rows), F32), pltpu.VMEM((2, nb, XA_HEADS * t, rows), BF16)],
        compiler_params=_cparams("parallel"),
        name=f"mem_attend_cache_l{layer}",
    )(x, k8, v8, wq, wo, ln_g, ln_b)


FFN_SUB = 512


def _swiglu(x_ref, wg_ref, wu_ref, wd_ref, h_ref):
    xb = x_ref[...].astype(BF16)
    f = h_ref.shape[1]
    for s in range(f // FFN_SUB):
        sl = slice(s * FFN_SUB, (s + 1) * FFN_SUB)
        gate = _dot(xb, wg_ref[0, :, sl])
        up = _dot(xb, wu_ref[0, :, sl])
        h_ref[:, sl] = (gate * jax.nn.sigmoid(gate) * up).astype(BF16)
    return _dot(h_ref[...], wd_ref[0])


def _ffn_ln_kernel(*refs, alpha, n_side):
    x_ref, wg_ref, wu_ref, wd_ref, g_ref, b_ref = refs[:6]
    o_ref = refs[6 + n_side]
    h_ref = refs[-1]
    _side_cast(refs[6:6 + n_side], refs[7 + n_side:7 + 2 * n_side])
    y = _swiglu(x_ref, wg_ref, wu_ref, wd_ref, h_ref)
    o_ref[...] = _layer_norm(alpha * x_ref[...] + y, g_ref[...], b_ref[...])


def _ffn_ln(x2d, w_gu, w_d, ln_g, ln_b, alpha, tm, side=()):
    n, d = x2d.shape
    tm = min(tm, n)
    f = w_d.shape[1]
    grid = (n // tm,)
    vec = pl.BlockSpec((1, d), lambda i: (0, 0))
    once = pl.Buffered(1)
    side_specs, side_shapes = _side_cast_plan(side, grid)
    out = pl.pallas_call(
        functools.partial(_ffn_ln_kernel, alpha=alpha, n_side=len(side)),
        grid=grid,
        in_specs=[pl.BlockSpec((tm, d), lambda i: (i, 0)),
                  pl.BlockSpec((1, d, f), lambda i: (0, 0, 0), pipeline_mode=once),
                  pl.BlockSpec((1, d, f), lambda i: (0, 0, 1), pipeline_mode=once),
                  pl.BlockSpec((1, f, d), lambda i: (0, 0, 0), pipeline_mode=once),
                  vec, vec] + side_specs,
        out_specs=[pl.BlockSpec((tm, d), lambda i: (i, 0))] + side_specs,
        out_shape=[jax.ShapeDtypeStruct((n, d), F32)] + side_shapes,
        scratch_shapes=[pltpu.VMEM((tm, f), BF16)],
        compiler_params=_cparams("parallel"),
        name="ffn_dense",
    )(x2d, w_gu, w_gu, w_d, ln_g, ln_b, *side)
    return out if side else out[0]


def _expert_kernel(eid_ref, used_ref, x_ref, wg_ref, wu_ref, wd_ref, o_ref, h_ref):
    i = pl.program_id(0)

    @pl.when(used_ref[i] > 0)
    def _():
        o_ref[...] = _swiglu(x_ref, wg_ref, wu_ref, wd_ref, h_ref)

    @pl.when(used_ref[i] == 0)
    def _():
        o_ref[...] = jnp.zeros_like(o_ref)


def _expert_ffn(xs, tile_eid, tile_used, w_gu, w_d, tm):
    r, d = xs.shape
    f = w_d.shape[1]
    grid_spec = pltpu.PrefetchScalarGridSpec(
        num_scalar_prefetch=2,
        grid=(r // tm,),
        in_specs=[pl.BlockSpec((tm, d), lambda i, e, u: (i, 0)),
                  pl.BlockSpec((1, d, f), lambda i, e, u: (e[i], 0, 0)),
                  pl.BlockSpec((1, d, f), lambda i, e, u: (e[i], 0, 1)),
                  pl.BlockSpec((1, f, d), lambda i, e, u: (e[i], 0, 0))],
        out_specs=pl.BlockSpec((tm, d), lambda i, e, u: (i, 0)),
        scratch_shapes=[pltpu.VMEM((tm, f), BF16)],
    )
    return pl.pallas_call(
        _expert_kernel,
        grid_spec=grid_spec,
        out_shape=jax.ShapeDtypeStruct((r, d), F32),
        compiler_params=_cparams("arbitrary"),
        name="ffn_experts",
    )(tile_eid, tile_used, xs, w_gu, w_gu, w_d)


def _split3(x):
    hi = x.astype(BF16).astype(F32)
    r = x - hi
    mid = r.astype(BF16).astype(F32)
    lo = (r - mid).astype(BF16).astype(F32)
    return hi, mid, lo


def _hgrn_tile_kernel(*refs, alpha, layer, chunk, has_state):
    st_refs = refs[-HG_HEADS:]
    refs = refs[:-HG_HEADS]
    s0_ref = None
    if has_state:
        s0_ref, refs = refs[1], refs[:1] + refs[2:]
    (x_ref, w_in_ref, lbp_ref, ng_ref, w_o_ref, g_ref, b_ref, o_ref, sfin_ref,
     proj_ref, lf_ref, bcum_ref, qrel_ref, krel_ref, kend_ref, qdec_ref, vb_ref, dec_ref,
     u_ref, oacc_ref, og_ref) = refs
    t = pl.program_id(1)
    nb, tt, d = x_ref.shape
    heads = HG_HEADS
    dv = ng_ref.shape[-1]
    vdim = heads * dv
    kdim = (w_in_ref.shape[1] - 2 * vdim) // 2
    dk = kdim // heads
    m = nb * tt
    c = chunk
    n_chunks = m // c
    chunks_per_seq = tt // c
    mid = max(c // 2 - 1, 0)

    @pl.when(t == 0)
    def _():
        for h in range(heads):
            if has_state:
                for bi in range(nb):
                    st_refs[h][bi] = s0_ref[bi, h]
            else:
                st_refs[h][...] = jnp.zeros_like(st_refs[h])

    x = x_ref[...].reshape(m, d)
    xb = x.astype(BF16)

    lbp = lbp_ref[...]
    e = jnp.exp(lbp - jnp.max(lbp, axis=0, keepdims=True))
    sm = e / jnp.sum(e, axis=0, keepdims=True)
    lb = jnp.sum(sm[1:layer + 1], axis=0, keepdims=True)

    f_cols = slice(kdim, 2 * kdim)
    proj_ref[:, f_cols] = _dot(xb, w_in_ref[:, f_cols])
    for j in range(n_chunks):
        rows = slice(j * c, (j + 1) * c)
        f = lb + (1.0 - lb) * jax.nn.sigmoid(proj_ref[rows, f_cols])
        proj_ref[rows, f_cols] = 1.0 - f
        for p, part in enumerate(_split3(jnp.log(f))):
            lf_ref[p, rows, :] = part.astype(lf_ref.dtype)
    proj_ref[:, 0:kdim] = _dot(xb, w_in_ref[:, 0:kdim])
    proj_ref[:, 2 * kdim:] = _dot(xb, w_in_ref[:, 2 * kdim:])

    ri = lax.broadcasted_iota(jnp.int32, (m, m), 0)
    ci = lax.broadcasted_iota(jnp.int32, (m, m), 1)
    causal = jnp.logical_and(ri // c == ci // c, ri >= ci)
    tri = causal.astype(BF16)
    bcum_ref[...] = (_dot(tri, lf_ref[0].astype(BF16)) + _dot(tri, lf_ref[1].astype(BF16))
                     + _dot(tri, lf_ref[2].astype(BF16)))

    sdt = qrel_ref.dtype
    vb_ref[...] = proj_ref[:, 2 * kdim:2 * kdim + vdim].astype(sdt)
    for j in range(n_chunks):
        rows = slice(j * c, (j + 1) * c)
        bcum = bcum_ref[rows, :]
        bmid = bcum[mid:mid + 1]
        blast = bcum[c - 1:c]
        q_dec = proj_ref[rows, 0:kdim] * jnp.exp(bcum)
        k_rel = proj_ref[rows, f_cols] * jnp.exp(bmid - bcum)
        qdec_ref[rows, :] = q_dec.astype(sdt)
        qrel_ref[rows, :] = (q_dec * jnp.exp(-bmid)).astype(sdt)
        krel_ref[rows, :] = k_rel.astype(sdt)
        kend_ref[rows, :] = (k_rel * jnp.exp(blast - bmid)).astype(sdt)
        dec_ref[j:j + 1, :] = jnp.exp(blast)
    if n_chunks < dec_ref.shape[0]:
        dec_ref[n_chunks:, :] = jnp.zeros((dec_ref.shape[0] - n_chunks, kdim), F32)
    dec_cols = dec_ref[...].T

    for h in range(heads):
        sk = slice(h * dk, (h + 1) * dk)
        sv = slice(h * dv, (h + 1) * dv)
        a = lax.dot_general(qrel_ref[:, sk].astype(BF16), krel_ref[:, sk].astype(BF16), _NT,
                            preferred_element_type=F32)
        a = jnp.where(causal, a, 0.0).astype(BF16)
        oacc_ref[:, sv] = _dot(a, vb_ref[:, sv].astype(BF16))

    for j in range(n_chunks):
        rows = slice(j * c, (j + 1) * c)
        for h in range(heads):
            sk = slice(h * dk, (h + 1) * dk)
            sv = slice(h * dv, (h + 1) * dv)
            u_ref[j, h] = lax.dot_general(kend_ref[rows, sk].astype(BF16), vb_ref[rows, sv].astype(BF16), _TN,
                                          preferred_element_type=F32)

    for j in range(n_chunks):
        rows = slice(j * c, (j + 1) * c)
        bi = j // chunks_per_seq
        for h in range(heads):
            sk = slice(h * dk, (h + 1) * dk)
            sv = slice(h * dv, (h + 1) * dv)
            st = st_refs[h][bi]
            oacc_ref[rows, sv] += _dot(qdec_ref[rows, sk].astype(BF16), st.astype(BF16))
            st_refs[h][bi] = st * dec_cols[sk, j:j + 1] + u_ref[j, h]

    ng = ng_ref[...]
    for h in range(heads):
        sv = slice(h * dv, (h + 1) * dv)
        o = oacc_ref[:, sv]
        o = o * lax.rsqrt(jnp.mean(o * o, axis=-1, keepdims=True) + RMS_EPS) * ng
        gate = proj_ref[:, 2 * kdim + vdim + h * dv:2 * kdim + vdim + (h + 1) * dv]
        og_ref[:, sv] = (o * (gate * jax.nn.sigmoid(gate))).astype(BF16)
    out = _dot(og_ref[...], w_o_ref[...])
    o_ref[...] = _layer_norm(alpha * x + out, g_ref[...], b_ref[...]).reshape(nb, tt, d)

    @pl.when(t == pl.num_programs(1) - 1)
    def _():
        for bi in range(nb):
            for h in range(heads):
                sfin_ref[bi, h] = st_refs[h][bi]


def _hgrn_tiled(x, s0, w_in, lb_param, norm_g, w_o, ln_g, ln_b, alpha, layer, chunk, nb, tt):
    b, t, d = x.shape
    heads = HG_HEADS
    dv = norm_g.shape[-1]
    vdim = heads * dv
    kdim = (w_in.shape[1] - 2 * vdim) // 2
    dk = kdim // heads
    m = nb * tt
    n_chunks = m // chunk
    has_state = s0 is not None
    const2 = lambda i, j: (0, 0)
    state_spec = pl.BlockSpec((nb, heads, dk, dv), lambda i, j: (i, 0, 0, 0))
    in_specs = [pl.BlockSpec((nb, tt, d), lambda i, j: (i, j, 0))]
    args = [x]
    if has_state:
        in_specs.append(state_spec)
        args.append(s0)
    in_specs += [pl.BlockSpec(w_in.shape, const2), pl.BlockSpec(lb_param.shape, const2),
                 pl.BlockSpec(norm_g.shape, const2), pl.BlockSpec(w_o.shape, const2),
                 pl.BlockSpec((1, d), const2), pl.BlockSpec((1, d), const2)]
    args += [w_in, lb_param, norm_g, w_o, ln_g, ln_b]
    sdt = BF16 if chunk % 16 == 0 else F32
    scratch = [pltpu.VMEM((m, w_in.shape[1]), F32),
               pltpu.VMEM((3, m, kdim), sdt),
               pltpu.VMEM((m, kdim), F32),
               pltpu.VMEM((m, kdim), sdt),
               pltpu.VMEM((m, kdim), sdt),
               pltpu.VMEM((m, kdim), sdt),
               pltpu.VMEM((m, kdim), sdt),
               pltpu.VMEM((m, vdim), sdt),
               pltpu.VMEM((max(n_chunks, 8), kdim), F32),
               pltpu.VMEM((n_chunks, heads, dk, dv), F32),
               pltpu.VMEM((m, vdim), F32),
               pltpu.VMEM((m, vdim), BF16)]
    scratch += [pltpu.VMEM((nb, dk, dv), F32) for _ in range(heads)]
    return pl.pallas_call(
        functools.partial(_hgrn_tile_kernel, alpha=alpha, layer=layer, chunk=chunk, has_state=has_state),
        grid=(b // nb, t // tt),
        in_specs=in_specs,
        out_specs=[pl.BlockSpec((nb, tt, d), lambda i, j: (i, j, 0)), state_spec],
        out_shape=[jax.ShapeDtypeStruct(x.shape, F32),
                   jax.ShapeDtypeStruct((b, heads, dk, dv), F32)],
        scratch_shapes=scratch,
        compiler_params=_cparams("parallel", "arbitrary"),
        name="hgrn_state" if has_state else "hgrn_fresh",
    )(*args)


_I_E0, _I_E1, _I_G0, _I_G1, _I_R0, _I_R1 = range(6)


def _router_kernel(x_ref, w_ref, cnt0_ref, info_ref, cnt_ref, carry_ref):
    i = pl.program_id(0)
    tm = x_ref.shape[0]

    @pl.when(i == 0)
    def _():
        carry_ref[...] = cnt0_ref[...]

    x = x_ref[...]
    w = w_ref[...]
    x_hi = x.astype(BF16)
    x_lo = (x - x_hi.astype(F32)).astype(BF16)
    w_hi = w.astype(BF16)
    w_lo = (w - w_hi.astype(F32)).astype(BF16)
    hi_both = _dot(x_hi, jnp.concatenate([w_hi, w_lo], axis=1))
    logits = hi_both[:, :LANES] + (hi_both[:, LANES:] + _dot(x_lo, w_hi))
    lane = lax.broadcasted_iota(jnp.int32, (tm, LANES), 1)
    valid = lane < N_EXPERTS
    logits = jnp.where(valid, logits, -jnp.inf)
    p = jnp.exp(logits - jnp.max(logits, axis=-1, keepdims=True))
    p = p / jnp.sum(p, axis=-1, keepdims=True)
    p0 = jnp.where(valid, p, -1.0)
    v0 = jnp.max(p0, axis=-1, keepdims=True)
    e0 = jnp.min(jnp.where(p0 == v0, lane, LANES), axis=-1, keepdims=True)
    p1 = jnp.where(lane == e0, -1.0, p0)
    v1 = jnp.max(p1, axis=-1, keepdims=True)
    e1 = jnp.min(jnp.where(p1 == v1, lane, LANES), axis=-1, keepdims=True)
    den = v0 + v1
    sel = jnp.logical_or(lane == e0, lane == e1)
    ri = lax.broadcasted_iota(jnp.int32, (tm, tm), 0)
    ci = lax.broadcasted_iota(jnp.int32, (tm, tm), 1)
    before = (ri > ci).astype(BF16)
    rank = _dot(before, sel.astype(BF16)) + carry_ref[...]
    r0 = jnp.sum(jnp.where(lane == e0, rank, 0.0), axis=-1, keepdims=True)
    r1 = jnp.sum(jnp.where(lane == e1, rank, 0.0), axis=-1, keepdims=True)
    carry_ref[...] += jnp.sum(sel.astype(F32), axis=0, keepdims=True)
    info = jnp.zeros((tm, LANES), F32)
    for idx, val in ((_I_E0, e0.astype(F32)), (_I_E1, e1.astype(F32)), (_I_G0, v0 / den),
                     (_I_G1, v1 / den), (_I_R0, r0), (_I_R1, r1)):
        info = jnp.where(lane == idx, val, info)
    info_ref[...] = info
    cnt_ref[...] = carry_ref[...]


def _router(x2d, w_router_padded, counts_in, tm):
    n, d = x2d.shape
    return pl.pallas_call(
        _router_kernel,
        grid=(n // tm,),
        in_specs=[pl.BlockSpec((tm, d), lambda i: (i, 0)),
                  pl.BlockSpec((d, LANES), lambda i: (0, 0)),
                  pl.BlockSpec((1, LANES), lambda i: (0, 0))],
        out_specs=[pl.BlockSpec((tm, LANES), lambda i: (i, 0)),
                   pl.BlockSpec((1, LANES), lambda i: (0, 0))],
        out_shape=[jax.ShapeDtypeStruct((n, LANES), F32), jax.ShapeDtypeStruct((1, LANES), F32)],
        scratch_shapes=[pltpu.VMEM((1, LANES), F32)],
        compiler_params=_cparams("arbitrary"),
        name="moe_router",
    )(x2d, w_router_padded, counts_in)


def _row_copy(src, src_row, dst, dst_row, sem):
    return pltpu.make_async_copy(src.at[pl.ds(src_row, 1)], dst.at[pl.ds(dst_row, 1)], sem)


def _scatter_rows(pos_ref, x_ref, out_hbm, sem):
    tb = x_ref.shape[0]

    def issue(j, carry):
        for kk in range(TOP_K):
            _row_copy(x_ref, j, out_hbm, pos_ref[TOP_K * j + kk], sem).start()
        return carry

    lax.fori_loop(0, tb, issue, 0, unroll=4)
    for _ in range(TOP_K):
        pltpu.make_async_copy(x_ref, out_hbm.at[pl.ds(0, tb)], sem).wait()


def _dispatch_kernel(pos_ref, fill_ref, xa_ref, xb_ref, out_hbm, zero_ref, sem, zsem, *, steps_a):
    i = pl.program_id(0)

    @pl.when(i == 0)
    def _():
        zero_ref[...] = jnp.zeros_like(zero_ref)
        tz = zero_ref.shape[0]

        def fill(e):
            row = pl.multiple_of(jnp.maximum(fill_ref[e], 0), tz)
            return pltpu.make_async_copy(zero_ref, out_hbm.at[pl.ds(row, tz)], zsem)

        for e in range(fill_ref.shape[0]):
            pl.when(fill_ref[e] >= 0)(lambda e=e: fill(e).start())
        for e in range(fill_ref.shape[0]):
            pl.when(fill_ref[e] >= 0)(lambda e=e: fill(e).wait())

    pl.when(i < steps_a)(lambda: _scatter_rows(pos_ref, xa_ref, out_hbm, sem))
    pl.when(i >= steps_a)(lambda: _scatter_rows(pos_ref, xb_ref, out_hbm, sem))


def _dispatch(xa, xb, pos_flat, fill_rows, n_rows, tz):
    d = xa.shape[1]
    tb = min(1024, xa.shape[0], xb.shape[0])
    steps_a = xa.shape[0] // tb
    steps_b = xb.shape[0] // tb
    return pl.pallas_call(
        functools.partial(_dispatch_kernel, steps_a=steps_a),
        grid=(steps_a + steps_b,),
        in_specs=[pl.BlockSpec((TOP_K * tb,), lambda i: (i,), memory_space=pltpu.SMEM),
                  pl.BlockSpec(memory_space=pltpu.SMEM),
                  pl.BlockSpec((tb, d), lambda i: (jnp.minimum(i, steps_a - 1), 0)),
                  pl.BlockSpec((tb, d), lambda i: (jnp.maximum(i - steps_a, 0), 0))],
        out_specs=pl.BlockSpec(memory_space=pl.ANY),
        out_shape=jax.ShapeDtypeStruct((n_rows, d), xa.dtype),
        scratch_shapes=[pltpu.VMEM((tz, d), xa.dtype), pltpu.SemaphoreType.DMA(()),
                        pltpu.SemaphoreType.DMA(())],
        compiler_params=_cparams("arbitrary"),
        name="moe_dispatch",
    )(pos_flat, fill_rows, xa, xb)


def _combine_kernel(pos_ref, posn_ref, x_ref, info_ref, y_hbm, g_ref, b_ref, o_ref, ybuf, sem, *, alpha):
    i = pl.program_id(0)
    n = pl.num_programs(0)
    tm = x_ref.shape[0]

    def issue(p_ref, slot):
        def body(j, carry):
            for kk in range(TOP_K):
                _row_copy(y_hbm, p_ref[TOP_K * j + kk], ybuf.at[slot, kk], j, sem.at[slot]).start()
            return carry
        lax.fori_loop(0, tm, body, 0, unroll=8)

    @pl.when(i == 0)
    def _():
        issue(pos_ref, 0)

    @pl.when(i + 1 < n)
    def _():
        issue(posn_ref, (i + 1) % 2)

    slot = i % 2
    for kk in range(TOP_K):
        pltpu.make_async_copy(y_hbm.at[pl.ds(0, tm)], ybuf.at[slot, kk], sem.at[slot]).wait()
    info = info_ref[...]
    moe = info[:, _I_G0:_I_G0 + 1] * ybuf[slot, 0] + info[:, _I_G1:_I_G1 + 1] * ybuf[slot, 1]
    o_ref[...] = _layer_norm(alpha * x_ref[...] + moe, g_ref[...], b_ref[...])


def _combine_ln(x2d, info, pos_flat, y, ln_g, ln_b, alpha, tm):
    n, d = x2d.shape
    nt = n // tm
    vec = pl.BlockSpec((1, d), lambda i: (0, 0))
    return pl.pallas_call(
        functools.partial(_combine_kernel, alpha=alpha),
        grid=(nt,),
        in_specs=[pl.BlockSpec((TOP_K * tm,), lambda i: (i,), memory_space=pltpu.SMEM),
                  pl.BlockSpec((TOP_K * tm,), lambda i: (jnp.minimum(i + 1, nt - 1),), memory_space=pltpu.SMEM),
                  pl.BlockSpec((tm, d), lambda i: (i, 0)),
                  pl.BlockSpec((tm, LANES), lambda i: (i, 0)),
                  pl.BlockSpec(memory_space=pl.ANY),
                  vec, vec],
        out_specs=pl.BlockSpec((tm, d), lambda i: (i, 0)),
        out_shape=jax.ShapeDtypeStruct((n, d), F32),
        scratch_shapes=[pltpu.VMEM((2, TOP_K, tm, d), F32), pltpu.SemaphoreType.DMA((2,))],
        compiler_params=_cparams("arbitrary"),
        name="moe_combine",
    )(pos_flat, pos_flat, x2d, info, y, ln_g, ln_b)


def _moe_ln(xs2d, w_router, w_gu, w_d, ln_g, ln_b, alpha, tm_expert):
    d = xs2d[0].shape[1]
    n_total = sum(x.shape[0] for x in xs2d)
    w_r = jnp.pad(w_router, ((0, 0), (0, LANES - N_EXPERTS)))
    counts = jnp.zeros((1, LANES), F32)
    infos = []
    for x in xs2d:
        info, counts = _router(x, w_r, counts, min(x.shape[0], 1024))
        infos.append(info)
    cnt = counts[0, :N_EXPERTS].astype(jnp.int32)
    padded = (cnt + tm_expert - 1) // tm_expert * tm_expert
    ends = jnp.cumsum(padded)
    offs = ends - padded
    n_rows = -(-(TOP_K * n_total) // tm_expert) * tm_expert + N_EXPERTS * tm_expert
    tile_start = jnp.arange(n_rows // tm_expert, dtype=jnp.int32) * tm_expert
    tile_eid = jnp.minimum(jnp.sum(tile_start[:, None] >= ends[None, :], axis=1), N_EXPERTS - 1).astype(jnp.int32)
    tile_used = (tile_start < offs[tile_eid] + cnt[tile_eid]).astype(jnp.int32)
    tail = ends[-1] + jnp.arange(N_EXPERTS, dtype=jnp.int32) * tm_expert
    fill_rows = jnp.concatenate([jnp.where(padded > 0, ends - tm_expert, -1),
                                 jnp.where(tail < n_rows, tail, -1)]).astype(jnp.int32)
    poss = []
    for info in infos:
        eid = info[:, _I_E0:_I_E1 + 1].astype(jnp.int32)
        rank = info[:, _I_R0:_I_R1 + 1].astype(jnp.int32)
        poss.append((offs[eid] + rank).reshape(-1))
    xa, xb = xs2d
    buf = _dispatch(xa, xb, jnp.concatenate(poss), fill_rows, n_rows, tm_expert)
    y = _expert_ffn(buf, tile_eid, tile_used, w_gu, w_d, tm_expert)
    return [_combine_ln(x, info, pos, y, ln_g, ln_b, alpha, min(x.shape[0], 1024))
            for x, info, pos in zip(xs2d, infos, poss)]


def kernel(x_prompt, x_sample, mem_prompt, state_pool, state_hgrn, cache_mem_k, cache_mem_v, pool_w, pool_scale, hg_w_in, hg_lb, hg_norm_g, hg_w_o, xa_w_q, xa_w_kv, xa_w_o, ffn_w_gu, ffn_w_d, moe_w_router, moe_w_gu, moe_w_d, ln_g, ln_b):
    b, t, d = x_prompt.shape
    sb, st, _ = x_sample.shape
    depth = ln_g.shape[0]
    mem_len = mem_prompt.shape[1]
    alpha = (2 * depth) ** 0.25
    n_prev = state_pool.shape[2]

    pool_w_b = pool_w.astype(BF16)
    hg_w_in_b = hg_w_in.astype(BF16)
    hg_w_o_b = hg_w_o.astype(BF16)
    xa_w_q_b = xa_w_q.astype(BF16)
    xa_w_kv_b = xa_w_kv.astype(BF16)
    xa_w_o_b = xa_w_o.astype(BF16)

    def vec(a):
        return a.reshape(1, -1)

    mk, mv = _mem_kv(mem_prompt.reshape(b * mem_len, d), xa_w_kv_b)
    mk4 = mk.reshape(depth, b, mem_len, d)
    mv4 = mv.reshape(depth, b, mem_len, d)

    def stored_order(c):
        halves = c.shape[4] // LANES
        c = c.reshape(depth, sb, mem_len, XA_HEADS, halves, LANES).transpose(0, 1, 2, 4, 3, 5)
        return c.reshape(depth, sb, mem_len * halves * XA_HEADS, LANES)

    ck8 = stored_order(cache_mem_k)
    cv8 = stored_order(cache_mem_v)

    tt_attn = min(t, 1024)
    nb_attn = min(sb, 8)

    def attend(xp, xs, layer, side=()):
        args = (xa_w_q_b[layer], xa_w_o_b[layer], vec(ln_g[layer, 1]), vec(ln_b[layer, 1]), alpha)
        return (_mem_attend(xp, mk4, mv4, layer, *args, tt_attn, side),
                _mem_attend_cache(xs, ck8, cv8, layer, *args, nb_attn))

    def cast_beside(weights, grid):
        views = tuple(w.reshape(-1, w.shape[-1]) for w in weights)
        if _side_cast_plan(views, grid) is None:
            return (), tuple(w.astype(BF16) for w in weights)
        return views, None

    pool_args = (pool_w_b[0], vec(pool_scale[0]), vec(ln_g[0, 0]), vec(ln_b[0, 0]), alpha)
    xp = _pool_prompt(x_prompt, *pool_args)
    prev = jnp.pad(state_pool[0], ((0, 0), (POOL_HALO - n_prev, 0), (0, 0)))
    xs = _pool_sample(x_sample, prev, n_prev, *pool_args)
    ffn_w = (ffn_w_gu, ffn_w_d)
    side, ffn_w_b = cast_beside(ffn_w, (b, t // tt_attn))
    xp, xs = attend(xp, xs, 0, side)
    if side:
        xp, ffn_w_b = xp[0], tuple(c.reshape(w.shape) for c, w in zip(xp[1:], ffn_w))
    ffn_args = (*ffn_w_b, vec(ln_g[0, 2]), vec(ln_b[0, 2]), alpha)
    moe_w = (moe_w_gu, moe_w_d)
    tm_ffn = min(256, b * t)
    side, moe_w_b = cast_beside(moe_w, (b * t // tm_ffn,))
    xp = _ffn_ln(xp.reshape(b * t, d), *ffn_args, tm_ffn, side)
    if side:
        xp, moe_w_b = xp[0], tuple(c.reshape(w.shape) for c, w in zip(xp[1:], moe_w))
    xp = xp.reshape(b, t, d)
    xs = _ffn_ln(xs.reshape(sb * st, d), *ffn_args, 1024).reshape(sb, st, d)

    hg_args = (hg_w_in_b[0], hg_lb, vec(hg_norm_g[0]), hg_w_o_b[0], vec(ln_g[1, 0]), vec(ln_b[1, 0]), alpha, 1)
    xp, hg_p = _hgrn_tiled(xp, None, *hg_args, 2 * HG_CHUNK, 1, min(t, 256))
    xs, hg_s = _hgrn_tiled(xs, state_hgrn[0], *hg_args, math.gcd(st, HG_CHUNK), min(sb, 8), st)
    xp, xs = attend(xp, xs, 1)
    yp, ys = _moe_ln([xp.reshape(b * t, d), xs.reshape(sb * st, d)], moe_w_router[0], moe_w_b[0][0],
                     moe_w_b[1][0], vec(ln_g[1, 2]), vec(ln_b[1, 2]), alpha, 512)

    heads_hd = (depth, b, mem_len, XA_HEADS, d // XA_HEADS)
    pool_p = x_prompt[:, t - n_prev:, :][None]
    pool_s = jnp.concatenate([state_pool[0], x_sample], axis=1)[:, st:, :][None]
    return (yp.reshape(b, t, d), ys.reshape(sb, st, d), pool_p, hg_p[None],
            mk.reshape(heads_hd), mv.reshape(heads_hd), pool_s, hg_s[None])
```

```python
import functools
import math

import jax
import jax.numpy as jnp
from jax import lax
from jax.experimental import pallas as pl
from jax.experimental.pallas import tpu as pltpu

F32 = jnp.float32
BF16 = jnp.bfloat16

POOL_WINDOWS = (2, 4, 8, 16)
POOL_HALO = 16
HG_HEADS = 8
HG_CHUNK = 32
RMS_EPS = 1e-6
XA_HEADS = 4
N_EXPERTS = 8
TOP_K = 2
LN_EPS = 1e-5
LANES = 128
VMEM_LIMIT = 62 * 1024 * 1024

TILE_KV = 512
TILE_POOL = 1024
SEQS_POOL = 16
TILE_ATTN = 1024
SEQS_ATTN = 8
TILE_FFN = 1024
TILE_FFN_CASTING = 256
TILE_EXPERT = 512
TILE_HGRN = 256
SEQS_HGRN = 8
TILE_ROUTE = 1024

_NT = (((1,), (1,)), ((), ()))
_TN = (((0,), (0,)), ((), ()))


def _cparams(*sem):
    return pltpu.CompilerParams(dimension_semantics=sem, vmem_limit_bytes=VMEM_LIMIT)


def _layer_norm(z, g, b):
    mu = jnp.mean(z, axis=-1, keepdims=True)
    zc = z - mu
    var = jnp.mean(zc * zc, axis=-1, keepdims=True)
    return zc * lax.rsqrt(var + LN_EPS) * g + b


def _dot(a, b):
    return jnp.dot(a, b, preferred_element_type=F32)


def _side_cast_plan(arrays, grid):
    steps = math.prod(grid)

    def block_index(*idx):
        lin = 0
        for g, i in zip(grid, idx):
            lin = lin * g + i
        return lin, 0

    specs, shapes = [], []
    for a in arrays:
        rows, cols = a.shape
        if rows % steps or (rows // steps) % 16:
            return None
        specs.append(pl.BlockSpec((rows // steps, cols), block_index))
        shapes.append(jax.ShapeDtypeStruct(a.shape, BF16))
    return specs, shapes


def _side_cast(in_refs, out_refs):
    for src, dst in zip(in_refs, out_refs):
        dst[...] = src[...].astype(dst.dtype)


def _kv_kernel(mem_ref, w_ref, k_ref, v_ref):
    kv = _dot(mem_ref[...].astype(BF16), w_ref[0])
    n = k_ref.shape[-1]
    k_ref[0] = kv[:, :n]
    v_ref[0] = kv[:, n:]


def _mem_kv(mem2d, w_kv):
    rows, d = mem2d.shape
    depth, _, e2 = w_kv.shape
    e = e2 // 2
    tm = min(rows, TILE_KV)
    out = jax.ShapeDtypeStruct((depth, rows, e), F32)
    return pl.pallas_call(
        _kv_kernel,
        grid=(depth, rows // tm),
        in_specs=[pl.BlockSpec((tm, d), lambda l, i: (i, 0)),
                  pl.BlockSpec((1, d, e2), lambda l, i: (l, 0, 0))],
        out_specs=[pl.BlockSpec((1, tm, e), lambda l, i: (l, i, 0)),
                   pl.BlockSpec((1, tm, e), lambda l, i: (l, i, 0))],
        out_shape=[out, out],
        compiler_params=_cparams("parallel", "parallel"),
        name="mem_kv",
    )(mem2d, w_kv)


def _window_sums(a, w):
    s, k = a, 1
    while k < w:
        s = s + pltpu.roll(s, k, 0)
        k *= 2
    return s


def _pool_mix(pooled_minus_tok, w_ref, scale):
    parts = [_dot(pooled_minus_tok[g].astype(BF16), w_ref[g]) for g in range(len(POOL_WINDOWS))]
    return jnp.concatenate(parts, axis=1) * scale


def _pool_prompt_kernel(x_ref, halo_ref, w_ref, scale_ref, g_ref, b_ref, o_ref, buf_ref, *, alpha):
    t = pl.program_id(1)
    tt = x_ref.shape[1]
    gw = w_ref.shape[-1]
    x = x_ref[0]
    buf_ref[0:POOL_HALO, :] = jnp.where(t > 0, halo_ref[0], 0.0)
    buf_ref[POOL_HALO:, :] = x
    pos = t * tt + lax.broadcasted_iota(jnp.int32, (tt, 1), 0)
    diffs = []
    for g, w in enumerate(POOL_WINDOWS):
        sl = slice(g * gw, (g + 1) * gw)
        s = _window_sums(buf_ref[:, sl], w)[POOL_HALO:, :]
        cnt = jnp.minimum(pos + 1, w).astype(F32)
        diffs.append(s / cnt - x[:, sl])
    mixed = _pool_mix(diffs, w_ref, scale_ref[...])
    o_ref[0] = _layer_norm(alpha * x + mixed, g_ref[...], b_ref[...])


def _pool_prompt(x, pool_w, scale, ln_g, ln_b, alpha):
    b, t, d = x.shape
    tt = min(t, TILE_POOL)
    hb = tt // POOL_HALO
    vec = pl.BlockSpec((1, d), lambda i, j: (0, 0))
    return pl.pallas_call(
        functools.partial(_pool_prompt_kernel, alpha=alpha),
        grid=(b, t // tt),
        in_specs=[pl.BlockSpec((1, tt, d), lambda i, j: (i, j, 0)),
                  pl.BlockSpec((1, POOL_HALO, d), lambda i, j: (i, jnp.maximum(j * hb - 1, 0), 0)),
                  pl.BlockSpec(pool_w.shape, lambda i, j: (0, 0, 0)),
                  vec, vec, vec],
        out_specs=pl.BlockSpec((1, tt, d), lambda i, j: (i, j, 0)),
        out_shape=jax.ShapeDtypeStruct(x.shape, F32),
        scratch_shapes=[pltpu.VMEM((tt + POOL_HALO, d), F32)],
        compiler_params=_cparams("parallel", "parallel"),
        name="pool_prompt",
    )(x, x, pool_w, scale, ln_g, ln_b)


def _pool_sample_kernel(x_ref, prev_ref, w_ref, scale_ref, g_ref, b_ref, o_ref, buf_ref, *, alpha, n_prev):
    nb, t, d = x_ref.shape
    gw = w_ref.shape[-1]
    rows = POOL_HALO + t
    buf_ref[:, 0:POOL_HALO, :] = prev_ref[...]
    buf_ref[:, POOL_HALO:, :] = x_ref[...]
    x = x_ref[...].reshape(nb * t, d)
    pos = lax.broadcasted_iota(jnp.int32, (nb, t, 1), 1).reshape(nb * t, 1)
    diffs = []
    for g, w in enumerate(POOL_WINDOWS):
        sl = slice(g * gw, (g + 1) * gw)
        s = _window_sums(buf_ref[:, :, sl].reshape(nb * rows, gw), w)
        s = s.reshape(nb, rows, gw)[:, POOL_HALO:, :].reshape(nb * t, gw)
        cnt = jnp.minimum(pos + 1 + n_prev, w).astype(F32)
        diffs.append(s / cnt - x[:, sl])
    mixed = _pool_mix(diffs, w_ref, scale_ref[...])
    o_ref[...] = _layer_norm(alpha * x + mixed, g_ref[...], b_ref[...]).reshape(nb, t, d)


def _pool_sample(x, prev_padded, n_prev, pool_w, scale, ln_g, ln_b, alpha):
    b, t, d = x.shape
    nb = min(b, SEQS_POOL)
    vec = pl.BlockSpec((1, d), lambda i: (0, 0))
    return pl.pallas_call(
        functools.partial(_pool_sample_kernel, alpha=alpha, n_prev=n_prev),
        grid=(b // nb,),
        in_specs=[pl.BlockSpec((nb, t, d), lambda i: (i, 0, 0)),
                  pl.BlockSpec((nb, POOL_HALO, d), lambda i: (i, 0, 0)),
                  pl.BlockSpec(pool_w.shape, lambda i: (0, 0, 0)),
                  vec, vec, vec],
        out_specs=pl.BlockSpec((nb, t, d), lambda i: (i, 0, 0)),
        out_shape=jax.ShapeDtypeStruct(x.shape, F32),
        scratch_shapes=[pltpu.VMEM((nb, POOL_HALO + t, d), F32)],
        compiler_params=_cparams("parallel"),
        name="pool_sample",
    )(x, prev_padded, pool_w, scale, ln_g, ln_b)


def _attn_kernel(*refs, alpha, n_side):
    o_ref = refs[7 + n_side]
    _side_cast(refs[7:7 + n_side], refs[8 + n_side:8 + 2 * n_side])
    _attn_body(*refs[:7], o_ref, *refs[-3:], alpha)


def _attn_body(x_ref, k_ref, v_ref, wq_ref, wo_ref, g_ref, b_ref, o_ref, oh_ref, s_ref, p_ref, alpha):
    tt, d = x_ref.shape
    hd = d // XA_HEADS
    scale = hd ** -0.5
    heads = [slice(h * hd, (h + 1) * hd) for h in range(XA_HEADS)]
    n_parts = 2 if tt % 32 == 0 else 1
    parts = [slice(r * (tt // n_parts), (r + 1) * (tt // n_parts)) for r in range(n_parts)]
    xs = [x_ref[rows, :] for rows in parts]
    for rows, x in zip(parts, xs):
        oh_ref[rows, :] = _dot(x.astype(BF16), wq_ref[...]).astype(BF16)
    for rows in parts:
        for h, sl in enumerate(heads):
            s_ref[h, rows, :] = lax.dot_general(oh_ref[rows, sl], k_ref[:, sl].astype(BF16), _NT,
                                                preferred_element_type=F32) * scale
    for rows in parts:
        for h in range(XA_HEADS):
            s = s_ref[h, rows, :]
            p = jnp.exp(s - jnp.max(s, axis=-1, keepdims=True))
            p_ref[h, rows, :] = (p / jnp.sum(p, axis=-1, keepdims=True)).astype(BF16)
    for rows in parts:
        for h, sl in enumerate(heads):
            oh_ref[rows, sl] = _dot(p_ref[h, rows, :], v_ref[:, sl].astype(BF16)).astype(BF16)
    for rows, x in zip(parts, xs):
        a = _dot(oh_ref[rows, :], wo_ref[...])
        o_ref[rows, :] = _layer_norm(alpha * x + a, g_ref[...], b_ref[...])


def _mem_attend(x, k4, v4, layer, wq, wo, ln_g, ln_b, alpha, tt, side=()):
    b, t, d = x.shape
    m = k4.shape[2]
    grid = (b, t // tt)
    vec = pl.BlockSpec((1, d), lambda i, j: (0, 0))
    mat = pl.BlockSpec((d, d), lambda i, j: (0, 0))
    kv = pl.BlockSpec((None, None, m, d), lambda i, j: (layer, i, 0, 0))
    side_specs, side_shapes = _side_cast_plan(side, grid)
    out = pl.pallas_call(
        functools.partial(_attn_kernel, alpha=alpha, n_side=len(side)),
        grid=grid,
        in_specs=[pl.BlockSpec((None, tt, d), lambda i, j: (i, j, 0)), kv, kv, mat, mat, vec, vec] + side_specs,
        out_specs=[pl.BlockSpec((None, tt, d), lambda i, j: (i, j, 0))] + side_specs,
        out_shape=[jax.ShapeDtypeStruct(x.shape, F32)] + side_shapes,
        scratch_shapes=[pltpu.VMEM((tt, d), BF16), pltpu.VMEM((XA_HEADS, tt, m), F32),
                        pltpu.VMEM((XA_HEADS, tt, m), BF16)],
        compiler_params=_cparams("arbitrary", "arbitrary"),
        name=f"mem_attend_l{layer}",
    )(x, k4, v4, wq, wo, ln_g, ln_b, *side)
    return out if side else out[0]


def _attn_cache_kernel(*refs, alpha):
    _attn_cache_body(*refs, alpha)


def _attn_cache_body(x_ref, k_ref, v_ref, wq_ref, wo_ref, g_ref, b_ref, o_ref, q_scr, oh_scr, s_scr, p_scr, alpha):
    nb, t, d = x_ref.shape
    heads = XA_HEADS
    hd = d // heads
    halves = hd // LANES
    assert halves == 2
    rows = k_ref.shape[1]
    qr = heads * t
    scale = hd ** -0.5
    x = x_ref[...].reshape(nb * t, d)
    q = _dot(x.astype(BF16), wq_ref[...])
    for c in range(halves):
        for h in range(heads):
            col = (h * halves + c) * LANES
            q_scr[c, :, h * t:(h + 1) * t, :] = q[:, col:col + LANES].reshape(nb, t, LANES)

    col_id = lax.broadcasted_iota(jnp.int32, (qr, rows), 1)
    row_head = lax.broadcasted_iota(jnp.int32, (qr, rows), 0) // t
    col_half = (col_id // heads) % halves
    col_head = col_id % heads
    keep = jnp.logical_and(col_half == 1, col_head == row_head)

    for b in range(nb):
        kb = k_ref[b].astype(BF16)
        g0 = lax.dot_general(q_scr[0, b].astype(BF16), kb, _NT, preferred_element_type=F32)
        g1 = lax.dot_general(q_scr[1, b].astype(BF16), kb, _NT, preferred_element_type=F32)
        s_scr[b] = jnp.where(col_half == 0, g0, g1)
    for b in range(nb):
        part = s_scr[b]
        s = (part + pltpu.roll(part, heads, 1)) * scale
        s = jnp.where(keep, s, -jnp.inf)
        p = jnp.exp(s - jnp.max(s, axis=-1, keepdims=True))
        p = p / jnp.sum(p, axis=-1, keepdims=True)
        p_scr[0, b] = pltpu.roll(p, rows - heads, 1).astype(BF16)
        p_scr[1, b] = p.astype(BF16)
    for b in range(nb):
        vb = v_ref[b].astype(BF16)
        for c in range(halves):
            oc = _dot(p_scr[c, b], vb)
            for h in range(heads):
                col = (h * halves + c) * LANES
                oh_scr[b * t:(b + 1) * t, col:col + LANES] = oc[h * t:(h + 1) * t, :]
    a = _dot(oh_scr[...].astype(BF16), wo_ref[...])
    o_ref[...] = _layer_norm(alpha * x + a, g_ref[...], b_ref[...]).reshape(nb, t, d)


def _mem_attend_cache(x, k8, v8, layer, wq, wo, ln_g, ln_b, alpha, nb):
    b, t, d = x.shape
    rows = k8.shape[2]
    vec = pl.BlockSpec((1, d), lambda i: (0, 0))
    mat = pl.BlockSpec((d, d), lambda i: (0, 0))
    kv = pl.BlockSpec((None, nb, rows, LANES), lambda i: (layer, i, 0, 0))
    return pl.pallas_call(
        functools.partial(_attn_cache_kernel, alpha=alpha),
        grid=(b // nb,),
        in_specs=[pl.BlockSpec((nb, t, d), lambda i: (i, 0, 0)), kv, kv, mat, mat, vec, vec],
        out_specs=pl.BlockSpec((nb, t, d), lambda i: (i, 0, 0)),
        out_shape=jax.ShapeDtypeStruct(x.shape, F32),
        scratch_shapes=[pltpu.VMEM((2, nb, XA_HEADS * t, LANES), F32), pltpu.VMEM((nb * t, d), F32),
                        pltpu.VMEM((nb, XA_HEADS * t, rows), F32), pltpu.VMEM((2, nb, XA_HEADS * t, rows), BF16)],
        compiler_params=_cparams("parallel"),
        name=f"mem_attend_cache_l{layer}",
    )(x, k8, v8, wq, wo, ln_g, ln_b)


FFN_SUB = 512


def _swiglu(x_ref, wg_ref, wu_ref, wd_ref, h_ref):
    xb = x_ref[...].astype(BF16)
    f = h_ref.shape[1]
    for s in range(f // FFN_SUB):
        sl = slice(s * FFN_SUB, (s + 1) * FFN_SUB)
        gate = _dot(xb, wg_ref[0, :, sl])
        up = _dot(xb, wu_ref[0, :, sl])
        h_ref[:, sl] = (gate * jax.nn.sigmoid(gate) * up).astype(BF16)
    return _dot(h_ref[...], wd_ref[0])


def _ffn_ln_kernel(*refs, alpha, n_side):
    x_ref, wg_ref, wu_ref, wd_ref, g_ref, b_ref = refs[:6]
    o_ref = refs[6 + n_side]
    h_ref = refs[-1]
    y = _swiglu(x_ref, wg_ref, wu_ref, wd_ref, h_ref)
    o_ref[...] = _layer_norm(alpha * x_ref[...] + y, g_ref[...], b_ref[...])
    _side_cast(refs[6:6 + n_side], refs[7 + n_side:7 + 2 * n_side])


def _ffn_ln(x2d, w_gu, w_d, ln_g, ln_b, alpha, tm, side=()):
    n, d = x2d.shape
    tm = min(tm, n)
    f = w_d.shape[1]
    grid = (n // tm,)
    vec = pl.BlockSpec((1, d), lambda i: (0, 0))
    once = pl.Buffered(1)
    side_specs, side_shapes = _side_cast_plan(side, grid)
    out = pl.pallas_call(
        functools.partial(_ffn_ln_kernel, alpha=alpha, n_side=len(side)),
        grid=grid,
        in_specs=[pl.BlockSpec((tm, d), lambda i: (i, 0)),
                  pl.BlockSpec((1, d, f), lambda i: (0, 0, 0), pipeline_mode=once),
                  pl.BlockSpec((1, d, f), lambda i: (0, 0, 1), pipeline_mode=once),
                  pl.BlockSpec((1, f, d), lambda i: (0, 0, 0), pipeline_mode=once),
                  vec, vec] + side_specs,
        out_specs=[pl.BlockSpec((tm, d), lambda i: (i, 0))] + side_specs,
        out_shape=[jax.ShapeDtypeStruct((n, d), F32)] + side_shapes,
        scratch_shapes=[pltpu.VMEM((tm, f), BF16)],
        compiler_params=_cparams("parallel"),
        name="ffn_dense",
    )(x2d, w_gu, w_gu, w_d, ln_g, ln_b, *side)
    return out if side else out[0]


def _expert_kernel(eid_ref, used_ref, x_ref, wg_ref, wu_ref, wd_ref, o_ref, h_ref):
    i = pl.program_id(0)

    @pl.when(used_ref[i] > 0)
    def _():
        o_ref[...] = _swiglu(x_ref, wg_ref, wu_ref, wd_ref, h_ref)

    @pl.when(used_ref[i] == 0)
    def _():
        o_ref[...] = jnp.zeros_like(o_ref)


def _expert_ffn(xs, tile_eid, tile_used, w_gu, w_d, tm):
    r, d = xs.shape
    f = w_d.shape[1]
    grid_spec = pltpu.PrefetchScalarGridSpec(
        num_scalar_prefetch=2,
        grid=(r // tm,),
        in_specs=[pl.BlockSpec((tm, d), lambda i, e, u: (i, 0)),
                  pl.BlockSpec((1, d, f), lambda i, e, u: (e[i], 0, 0)),
                  pl.BlockSpec((1, d, f), lambda i, e, u: (e[i], 0, 1)),
                  pl.BlockSpec((1, f, d), lambda i, e, u: (e[i], 0, 0))],
        out_specs=pl.BlockSpec((tm, d), lambda i, e, u: (i, 0)),
        scratch_shapes=[pltpu.VMEM((tm, f), BF16)],
    )
    return pl.pallas_call(
        _expert_kernel,
        grid_spec=grid_spec,
        out_shape=jax.ShapeDtypeStruct((r, d), F32),
        compiler_params=_cparams("arbitrary"),
        name="ffn_experts",
    )(tile_eid, tile_used, xs, w_gu, w_gu, w_d)


def _split3(x):
    hi = x.astype(BF16).astype(F32)
    r = x - hi
    mid = r.astype(BF16).astype(F32)
    lo = (r - mid).astype(BF16).astype(F32)
    return hi, mid, lo


def _hgrn_tile_kernel(*refs, alpha, layer, chunk, has_state):
    st_refs = refs[-HG_HEADS:]
    refs = refs[:-HG_HEADS]
    s0_ref = None
    if has_state:
        s0_ref, refs = refs[1], refs[:1] + refs[2:]
    (x_ref, w_in_ref, lbp_ref, ng_ref, w_o_ref, g_ref, b_ref, o_ref, sfin_ref,
     proj_ref, lf_ref, bcum_ref, qrel_ref, krel_ref, kend_ref, qdec_ref, vb_ref, dec_ref,
     u_ref, oacc_ref, og_ref) = refs
    t = pl.program_id(1)
    nb, tt, d = x_ref.shape
    heads = HG_HEADS
    dv = ng_ref.shape[-1]
    vdim = heads * dv
    kdim = (w_in_ref.shape[1] - 2 * vdim) // 2
    dk = kdim // heads
    m = nb * tt
    c = chunk
    n_chunks = m // c
    chunks_per_seq = tt // c
    mid = max(c // 2 - 1, 0)

    @pl.when(t == 0)
    def _():
        for h in range(heads):
            if has_state:
                for bi in range(nb):
                    st_refs[h][bi] = s0_ref[bi, h]
            else:
                st_refs[h][...] = jnp.zeros_like(st_refs[h])

    x = x_ref[...].reshape(m, d)
    xb = x.astype(BF16)

    lbp = lbp_ref[...]
    e = jnp.exp(lbp - jnp.max(lbp, axis=0, keepdims=True))
    sm = e / jnp.sum(e, axis=0, keepdims=True)
    lb = jnp.sum(sm[1:layer + 1], axis=0, keepdims=True)

    f_cols = slice(kdim, 2 * kdim)
    proj_ref[:, f_cols] = _dot(xb, w_in_ref[:, f_cols])
    for j in range(n_chunks):
        rows = slice(j * c, (j + 1) * c)
        f = lb + (1.0 - lb) * jax.nn.sigmoid(proj_ref[rows, f_cols])
        proj_ref[rows, f_cols] = 1.0 - f
        for p, part in enumerate(_split3(jnp.log(f))):
            lf_ref[p, rows, :] = part.astype(lf_ref.dtype)
    proj_ref[:, 0:kdim] = _dot(xb, w_in_ref[:, 0:kdim])
    proj_ref[:, 2 * kdim:] = _dot(xb, w_in_ref[:, 2 * kdim:])

    ri = lax.broadcasted_iota(jnp.int32, (m, m), 0)
    ci = lax.broadcasted_iota(jnp.int32, (m, m), 1)
    causal = jnp.logical_and(ri // c == ci // c, ri >= ci)
    tri = causal.astype(BF16)
    bcum_ref[...] = (_dot(tri, lf_ref[0].astype(BF16)) + _dot(tri, lf_ref[1].astype(BF16))
                     + _dot(tri, lf_ref[2].astype(BF16)))

    sdt = qrel_ref.dtype
    vb_ref[...] = proj_ref[:, 2 * kdim:2 * kdim + vdim].astype(sdt)
    for j in range(n_chunks):
        rows = slice(j * c, (j + 1) * c)
        bcum = bcum_ref[rows, :]
        bmid = bcum[mid:mid + 1]
        blast = bcum[c - 1:c]
        q_dec = proj_ref[rows, 0:kdim] * jnp.exp(bcum)
        k_rel = proj_ref[rows, f_cols] * jnp.exp(bmid - bcum)
        qdec_ref[rows, :] = q_dec.astype(sdt)
        qrel_ref[rows, :] = (q_dec * jnp.exp(-bmid)).astype(sdt)
        krel_ref[rows, :] = k_rel.astype(sdt)
        kend_ref[rows, :] = (k_rel * jnp.exp(blast - bmid)).astype(sdt)
        dec_ref[j:j + 1, :] = jnp.exp(blast)
    if n_chunks < dec_ref.shape[0]:
        dec_ref[n_chunks:, :] = jnp.zeros((dec_ref.shape[0] - n_chunks, kdim), F32)
    dec_cols = dec_ref[...].T

    for h in range(heads):
        sk = slice(h * dk, (h + 1) * dk)
        sv = slice(h * dv, (h + 1) * dv)
        a = lax.dot_general(qrel_ref[:, sk].astype(BF16), krel_ref[:, sk].astype(BF16), _NT,
                            preferred_element_type=F32)
        a = jnp.where(causal, a, 0.0).astype(BF16)
        oacc_ref[:, sv] = _dot(a, vb_ref[:, sv].astype(BF16))

    for j in range(n_chunks):
        rows = slice(j * c, (j + 1) * c)
        for h in range(heads):
            sk = slice(h * dk, (h + 1) * dk)
            sv = slice(h * dv, (h + 1) * dv)
            u_ref[j, h] = lax.dot_general(kend_ref[rows, sk].astype(BF16), vb_ref[rows, sv].astype(BF16), _TN,
                                          preferred_element_type=F32)

    for j in range(n_chunks):
        rows = slice(j * c, (j + 1) * c)
        bi = j // chunks_per_seq
        for h in range(heads):
            sk = slice(h * dk, (h + 1) * dk)
            sv = slice(h * dv, (h + 1) * dv)
            st = st_refs[h][bi]
            oacc_ref[rows, sv] += _dot(qdec_ref[rows, sk].astype(BF16), st.astype(BF16))
            st_refs[h][bi] = st * dec_cols[sk, j:j + 1] + u_ref[j, h]

    ng = ng_ref[...]
    for h in range(heads):
        sv = slice(h * dv, (h + 1) * dv)
        o = oacc_ref[:, sv]
        o = o * lax.rsqrt(jnp.mean(o * o, axis=-1, keepdims=True) + RMS_EPS) * ng
        gate = proj_ref[:, 2 * kdim + vdim + h * dv:2 * kdim + vdim + (h + 1) * dv]
        og_ref[:, sv] = (o * (gate * jax.nn.sigmoid(gate))).astype(BF16)
    out = _dot(og_ref[...], w_o_ref[...])
    o_ref[...] = _layer_norm(alpha * x + out, g_ref[...], b_ref[...]).reshape(nb, tt, d)

    @pl.when(t == pl.num_programs(1) - 1)
    def _():
        for bi in range(nb):
            for h in range(heads):
                sfin_ref[bi, h] = st_refs[h][bi]


def _hgrn_tiled(x, s0, w_in, lb_param, norm_g, w_o, ln_g, ln_b, alpha, layer, chunk, nb, tt):
    b, t, d = x.shape
    heads = HG_HEADS
    dv = norm_g.shape[-1]
    vdim = heads * dv
    kdim = (w_in.shape[1] - 2 * vdim) // 2
    dk = kdim // heads
    m = nb * tt
    n_chunks = m // chunk
    has_state = s0 is not None
    const2 = lambda i, j: (0, 0)
    state_spec = pl.BlockSpec((nb, heads, dk, dv), lambda i, j: (i, 0, 0, 0))
    in_specs = [pl.BlockSpec((nb, tt, d), lambda i, j: (i, j, 0))]
    args = [x]
    if has_state:
        in_specs.append(state_spec)
        args.append(s0)
    in_specs += [pl.BlockSpec(w_in.shape, const2), pl.BlockSpec(lb_param.shape, const2),
                 pl.BlockSpec(norm_g.shape, const2), pl.BlockSpec(w_o.shape, const2),
                 pl.BlockSpec((1, d), const2), pl.BlockSpec((1, d), const2)]
    args += [w_in, lb_param, norm_g, w_o, ln_g, ln_b]
    sdt = BF16 if chunk % 16 == 0 else F32
    scratch = [pltpu.VMEM((m, w_in.shape[1]), F32),
               pltpu.VMEM((3, m, kdim), sdt),
               pltpu.VMEM((m, kdim), F32),
               pltpu.VMEM((m, kdim), sdt),
               pltpu.VMEM((m, kdim), sdt),
               pltpu.VMEM((m, kdim), sdt),
               pltpu.VMEM((m, kdim), sdt),
               pltpu.VMEM((m, vdim), sdt),
               pltpu.VMEM((max(n_chunks, 8), kdim), F32),
               pltpu.VMEM((n_chunks, heads, dk, dv), F32),
               pltpu.VMEM((m, vdim), F32),
               pltpu.VMEM((m, vdim), BF16)]
    scratch += [pltpu.VMEM((nb, dk, dv), F32) for _ in range(heads)]
    return pl.pallas_call(
        functools.partial(_hgrn_tile_kernel, alpha=alpha, layer=layer, chunk=chunk, has_state=has_state),
        grid=(b // nb, t // tt),
        in_specs=in_specs,
        out_specs=[pl.BlockSpec((nb, tt, d), lambda i, j: (i, j, 0)), state_spec],
        out_shape=[jax.ShapeDtypeStruct(x.shape, F32),
                   jax.ShapeDtypeStruct((b, heads, dk, dv), F32)],
        scratch_shapes=scratch,
        compiler_params=_cparams("parallel", "arbitrary"),
        name="hgrn_state" if has_state else "hgrn_fresh",
    )(*args)


_I_E0, _I_E1, _I_G0, _I_G1, _I_R0, _I_R1 = range(6)


def _router_kernel(x_ref, w_ref, cnt0_ref, info_ref, cnt_ref, carry_ref):
    i = pl.program_id(0)
    tm = x_ref.shape[0]

    @pl.when(i == 0)
    def _():
        carry_ref[...] = cnt0_ref[...]

    x = x_ref[...]
    w = w_ref[...]
    x_hi = x.astype(BF16)
    x_lo = (x - x_hi.astype(F32)).astype(BF16)
    w_hi = w.astype(BF16)
    w_lo = (w - w_hi.astype(F32)).astype(BF16)
    hi_both = _dot(x_hi, jnp.concatenate([w_hi, w_lo], axis=1))
    logits = hi_both[:, :LANES] + (hi_both[:, LANES:] + _dot(x_lo, w_hi))
    lane = lax.broadcasted_iota(jnp.int32, (tm, LANES), 1)
    valid = lane < N_EXPERTS
    logits = jnp.where(valid, logits, -jnp.inf)
    p = jnp.exp(logits - jnp.max(logits, axis=-1, keepdims=True))
    p = p / jnp.sum(p, axis=-1, keepdims=True)
    p0 = jnp.where(valid, p, -1.0)
    v0 = jnp.max(p0, axis=-1, keepdims=True)
    e0 = jnp.min(jnp.where(p0 == v0, lane, LANES), axis=-1, keepdims=True)
    p1 = jnp.where(lane == e0, -1.0, p0)
    v1 = jnp.max(p1, axis=-1, keepdims=True)
    e1 = jnp.min(jnp.where(p1 == v1, lane, LANES), axis=-1, keepdims=True)
    den = v0 + v1
    sel = jnp.logical_or(lane == e0, lane == e1)
    ri = lax.broadcasted_iota(jnp.int32, (tm, tm), 0)
    ci = lax.broadcasted_iota(jnp.int32, (tm, tm), 1)
    before = (ri > ci).astype(BF16)
    rank = _dot(before, sel.astype(BF16)) + carry_ref[...]
    r0 = jnp.sum(jnp.where(lane == e0, rank, 0.0), axis=-1, keepdims=True)
    r1 = jnp.sum(jnp.where(lane == e1, rank, 0.0), axis=-1, keepdims=True)
    carry_ref[...] += jnp.sum(sel.astype(F32), axis=0, keepdims=True)
    info = jnp.zeros((tm, LANES), F32)
    for idx, val in ((_I_E0, e0.astype(F32)), (_I_E1, e1.astype(F32)), (_I_G0, v0 / den),
                     (_I_G1, v1 / den), (_I_R0, r0), (_I_R1, r1)):
        info = jnp.where(lane == idx, val, info)
    info_ref[...] = info
    cnt_ref[...] = carry_ref[...]


def _router(x2d, w_router_padded, counts_in, tm):
    n, d = x2d.shape
    return pl.pallas_call(
        _router_kernel,
        grid=(n // tm,),
        in_specs=[pl.BlockSpec((tm, d), lambda i: (i, 0)),
                  pl.BlockSpec((d, LANES), lambda i: (0, 0)),
                  pl.BlockSpec((1, LANES), lambda i: (0, 0))],
        out_specs=[pl.BlockSpec((tm, LANES), lambda i: (i, 0)),
                   pl.BlockSpec((1, LANES), lambda i: (0, 0))],
        out_shape=[jax.ShapeDtypeStruct((n, LANES), F32), jax.ShapeDtypeStruct((1, LANES), F32)],
        scratch_shapes=[pltpu.VMEM((1, LANES), F32)],
        compiler_params=_cparams("arbitrary"),
        name="moe_router",
    )(x2d, w_router_padded, counts_in)


def _row_copy(src, src_row, dst, dst_row, sem):
    return pltpu.make_async_copy(src.at[pl.ds(src_row, 1)], dst.at[pl.ds(dst_row, 1)], sem)


def _scatter_rows(pos_ref, x_ref, out_hbm, sem):
    tb = x_ref.shape[0]

    def issue(j, carry):
        for kk in range(TOP_K):
            _row_copy(x_ref, j, out_hbm, pos_ref[TOP_K * j + kk], sem).start()
        return carry

    lax.fori_loop(0, tb, issue, 0, unroll=4)
    for _ in range(TOP_K):
        pltpu.make_async_copy(x_ref, out_hbm.at[pl.ds(0, tb)], sem).wait()


def _dispatch_kernel(pos_ref, fill_ref, xa_ref, xb_ref, out_hbm, zero_ref, sem, zsem, *, steps_a):
    i = pl.program_id(0)

    @pl.when(i == 0)
    def _():
        zero_ref[...] = jnp.zeros_like(zero_ref)
        tz = zero_ref.shape[0]

        def fill(e):
            row = pl.multiple_of(jnp.maximum(fill_ref[e], 0), tz)
            return pltpu.make_async_copy(zero_ref, out_hbm.at[pl.ds(row, tz)], zsem)

        for e in range(fill_ref.shape[0]):
            pl.when(fill_ref[e] >= 0)(lambda e=e: fill(e).start())
        for e in range(fill_ref.shape[0]):
            pl.when(fill_ref[e] >= 0)(lambda e=e: fill(e).wait())

    pl.when(i < steps_a)(lambda: _scatter_rows(pos_ref, xa_ref, out_hbm, sem))
    pl.when(i >= steps_a)(lambda: _scatter_rows(pos_ref, xb_ref, out_hbm, sem))


def _dispatch(xa, xb, pos_flat, fill_rows, n_rows, tz):
    d = xa.shape[1]
    tb = min(TILE_ROUTE, xa.shape[0], xb.shape[0])
    steps_a = xa.shape[0] // tb
    steps_b = xb.shape[0] // tb
    return pl.pallas_call(
        functools.partial(_dispatch_kernel, steps_a=steps_a),
        grid=(steps_a + steps_b,),
        in_specs=[pl.BlockSpec((TOP_K * tb,), lambda i: (i,), memory_space=pltpu.SMEM),
                  pl.BlockSpec(memory_space=pltpu.SMEM),
                  pl.BlockSpec((tb, d), lambda i: (jnp.minimum(i, steps_a - 1), 0)),
                  pl.BlockSpec((tb, d), lambda i: (jnp.maximum(i - steps_a, 0), 0))],
        out_specs=pl.BlockSpec(memory_space=pl.ANY),
        out_shape=jax.ShapeDtypeStruct((n_rows, d), xa.dtype),
        scratch_shapes=[pltpu.VMEM((tz, d), xa.dtype), pltpu.SemaphoreType.DMA(()),
                        pltpu.SemaphoreType.DMA(())],
        compiler_params=_cparams("arbitrary"),
        name="moe_dispatch",
    )(pos_flat, fill_rows, xa, xb)


def _combine_kernel(pos_ref, posn_ref, x_ref, info_ref, y_hbm, g_ref, b_ref, o_ref, ybuf, sem, *, alpha):
    i = pl.program_id(0)
    n = pl.num_programs(0)
    tm = x_ref.shape[0]

    def issue(p_ref, slot):
        def body(j, carry):
            for kk in range(TOP_K):
                _row_copy(y_hbm, p_ref[TOP_K * j + kk], ybuf.at[slot, kk], j, sem.at[slot]).start()
            return carry
        lax.fori_loop(0, tm, body, 0, unroll=8)

    @pl.when(i == 0)
    def _():
        issue(pos_ref, 0)

    @pl.when(i + 1 < n)
    def _():
        issue(posn_ref, (i + 1) % 2)

    slot = i % 2
    for kk in range(TOP_K):
        pltpu.make_async_copy(y_hbm.at[pl.ds(0, tm)], ybuf.at[slot, kk], sem.at[slot]).wait()
    info = info_ref[...]
    moe = info[:, _I_G0:_I_G0 + 1] * ybuf[slot, 0] + info[:, _I_G1:_I_G1 + 1] * ybuf[slot, 1]
    o_ref[...] = _layer_norm(alpha * x_ref[...] + moe, g_ref[...], b_ref[...])


def _combine_ln(x2d, info, pos_flat, y, ln_g, ln_b, alpha, tm):
    n, d = x2d.shape
    nt = n // tm
    vec = pl.BlockSpec((1, d), lambda i: (0, 0))
    return pl.pallas_call(
        functools.partial(_combine_kernel, alpha=alpha),
        grid=(nt,),
        in_specs=[pl.BlockSpec((TOP_K * tm,), lambda i: (i,), memory_space=pltpu.SMEM),
                  pl.BlockSpec((TOP_K * tm,), lambda i: (jnp.minimum(i + 1, nt - 1),), memory_space=pltpu.SMEM),
                  pl.BlockSpec((tm, d), lambda i: (i, 0)),
                  pl.BlockSpec((tm, LANES), lambda i: (i, 0)),
                  pl.BlockSpec(memory_space=pl.ANY),
                  vec, vec],
        out_specs=pl.BlockSpec((tm, d), lambda i: (i, 0)),
        out_shape=jax.ShapeDtypeStruct((n, d), F32),
        scratch_shapes=[pltpu.VMEM((2, TOP_K, tm, d), F32), pltpu.SemaphoreType.DMA((2,))],
        compiler_params=_cparams("arbitrary"),
        name="moe_combine",
    )(pos_flat, pos_flat, x2d, info, y, ln_g, ln_b)


def _moe_ln(xs2d, w_router, w_gu, w_d, ln_g, ln_b, alpha, tm_expert):
    d = xs2d[0].shape[1]
    n_total = sum(x.shape[0] for x in xs2d)
    w_r = jnp.pad(w_router, ((0, 0), (0, LANES - N_EXPERTS)))
    counts = jnp.zeros((1, LANES), F32)
    infos = []
    for x in xs2d:
        info, counts = _router(x, w_r, counts, min(x.shape[0], TILE_ROUTE))
        infos.append(info)
    cnt = counts[0, :N_EXPERTS].astype(jnp.int32)
    padded = (cnt + tm_expert - 1) // tm_expert * tm_expert
    ends = jnp.cumsum(padded)
    offs = ends - padded
    n_rows = -(-(TOP_K * n_total) // tm_expert) * tm_expert + N_EXPERTS * tm_expert
    tile_start = jnp.arange(n_rows // tm_expert, dtype=jnp.int32) * tm_expert
    tile_eid = jnp.minimum(jnp.sum(tile_start[:, None] >= ends[None, :], axis=1), N_EXPERTS - 1).astype(jnp.int32)
    tile_used = (tile_start < offs[tile_eid] + cnt[tile_eid]).astype(jnp.int32)
    tail = ends[-1] + jnp.arange(N_EXPERTS, dtype=jnp.int32) * tm_expert
    fill_rows = jnp.concatenate([jnp.where(padded > 0, ends - tm_expert, -1),
                                 jnp.where(tail < n_rows, tail, -1)]).astype(jnp.int32)
    poss = []
    for info in infos:
        eid = info[:, _I_E0:_I_E1 + 1].astype(jnp.int32)
        rank = info[:, _I_R0:_I_R1 + 1].astype(jnp.int32)
        poss.append((offs[eid] + rank).reshape(-1))
    xa, xb = xs2d
    buf = _dispatch(xa, xb, jnp.concatenate(poss), fill_rows, n_rows, tm_expert)
    y = _expert_ffn(buf, tile_eid, tile_used, w_gu, w_d, tm_expert)
    return [_combine_ln(x, info, pos, y, ln_g, ln_b, alpha, min(x.shape[0], TILE_ROUTE))
            for x, info, pos in zip(xs2d, infos, poss)]


def kernel(x_prompt, x_sample, mem_prompt, state_pool, state_hgrn, cache_mem_k, cache_mem_v, pool_w, pool_scale, hg_w_in, hg_lb, hg_norm_g, hg_w_o, xa_w_q, xa_w_kv, xa_w_o, ffn_w_gu, ffn_w_d, moe_w_router, moe_w_gu, moe_w_d, ln_g, ln_b):
    b, t, d = x_prompt.shape
    sb, st, _ = x_sample.shape
    depth = ln_g.shape[0]
    mem_len = mem_prompt.shape[1]
    alpha = (2 * depth) ** 0.25
    n_prev = state_pool.shape[2]

    pool_w_b = pool_w.astype(BF16)
    hg_w_in_b = hg_w_in.astype(BF16)
    hg_w_o_b = hg_w_o.astype(BF16)
    xa_w_q_b = xa_w_q.astype(BF16)
    xa_w_kv_b = xa_w_kv.astype(BF16)
    xa_w_o_b = xa_w_o.astype(BF16)

    def vec(a):
        return a.reshape(1, -1)

    mk, mv = _mem_kv(mem_prompt.reshape(b * mem_len, d), xa_w_kv_b)
    mk4 = mk.reshape(depth, b, mem_len, d)
    mv4 = mv.reshape(depth, b, mem_len, d)

    def stored_order(c):
        halves = c.shape[4] // LANES
        c = c.reshape(depth, sb, mem_len, XA_HEADS, halves, LANES).transpose(0, 1, 2, 4, 3, 5)
        return c.reshape(depth, sb, mem_len * halves * XA_HEADS, LANES)

    ck8 = stored_order(cache_mem_k)
    cv8 = stored_order(cache_mem_v)

    tt_attn = min(t, TILE_ATTN)
    nb_attn = min(sb, SEQS_ATTN)

    def attend(xp, xs, layer, side=()):
        args = (xa_w_q_b[layer], xa_w_o_b[layer], vec(ln_g[layer, 1]), vec(ln_b[layer, 1]), alpha)
        outp = _mem_attend(xp, mk4, mv4, layer, *args, tt_attn, side)
        outs = _mem_attend_cache(xs, ck8, cv8, layer, *args, nb_attn)
        return (outp[0], outs, outp[1:]) if side else (outp, outs, [])

    def cast_beside(weights, grid):
        views = tuple(w.reshape(-1, w.shape[-1]) for w in weights)
        if _side_cast_plan(views, grid) is None:
            return (), tuple(w.astype(BF16) for w in weights)
        return views, None

    pool_args = (pool_w_b[0], vec(pool_scale[0]), vec(ln_g[0, 0]), vec(ln_b[0, 0]), alpha)
    xp = _pool_prompt(x_prompt, *pool_args)
    prev = jnp.pad(state_pool[0], ((0, 0), (POOL_HALO - n_prev, 0), (0, 0)))
    xs = _pool_sample(x_sample, prev, n_prev, *pool_args)
    ffn_w = (ffn_w_gu, ffn_w_d)
    side, ffn_w_b = cast_beside(ffn_w, (b, t // tt_attn))
    xp, xs, casts = attend(xp, xs, 0, side)
    if side:
        ffn_w_b = tuple(c.reshape(w.shape) for c, w in zip(casts, ffn_w))
    ffn_args = (*ffn_w_b, vec(ln_g[0, 2]), vec(ln_b[0, 2]), alpha)
    moe_w = (moe_w_gu, moe_w_d)
    tm_ffn = min(TILE_FFN_CASTING, b * t)
    side, moe_w_b = cast_beside(moe_w, (b * t // tm_ffn,))
    xp = _ffn_ln(xp.reshape(b * t, d), *ffn_args, tm_ffn, side)
    if side:
        xp, moe_w_b = xp[0], tuple(c.reshape(w.shape) for c, w in zip(xp[1:], moe_w))
    xp = xp.reshape(b, t, d)
    xs = _ffn_ln(xs.reshape(sb * st, d), *ffn_args, TILE_FFN).reshape(sb, st, d)

    hg_args = (hg_w_in_b[0], hg_lb, vec(hg_norm_g[0]), hg_w_o_b[0], vec(ln_g[1, 0]), vec(ln_b[1, 0]), alpha, 1)
    xp, hg_p = _hgrn_tiled(xp, None, *hg_args, 2 * HG_CHUNK, 1, min(t, TILE_HGRN))
    xs, hg_s = _hgrn_tiled(xs, state_hgrn[0], *hg_args, math.gcd(st, HG_CHUNK), min(sb, SEQS_HGRN), st)
    xp, xs, _ = attend(xp, xs, 1)
    yp, ys = _moe_ln([xp.reshape(b * t, d), xs.reshape(sb * st, d)], moe_w_router[0], moe_w_b[0][0],
                     moe_w_b[1][0], vec(ln_g[1, 2]), vec(ln_b[1, 2]), alpha, TILE_EXPERT)

    heads_hd = (depth, b, mem_len, XA_HEADS, d // XA_HEADS)
    pool_p = x_prompt[:, t - n_prev:, :][None]
    pool_s = jnp.concatenate([state_pool[0], x_sample], axis=1)[:, st:, :][None]
    return (yp.reshape(b, t, d), ys.reshape(sb, st, d), pool_p, hg_p[None],
            mk.reshape(heads_hd), mv.reshape(heads_hd), pool_s, hg_s[None])
```

```python
import functools
import math

import jax
import jax.numpy as jnp
from jax import lax
from jax.experimental import pallas as pl
from jax.experimental.pallas import tpu as pltpu

F32 = jnp.float32
BF16 = jnp.bfloat16

POOL_WINDOWS = (2, 4, 8, 16)
POOL_HALO = 16
HG_HEADS = 8
HG_CHUNK = 32
RMS_EPS = 1e-6
XA_HEADS = 4
N_EXPERTS = 8
TOP_K = 2
LN_EPS = 1e-5
LANES = 128
VMEM_LIMIT = 62 * 1024 * 1024

TILE_KV = 512
TILE_POOL = 1024
SEQS_POOL = 16
TILE_ATTN = 1024
SEQS_ATTN = 8
TILE_FFN = 1024
TILE_FFN_CASTING = 256
TILE_EXPERT = 512
TILE_HGRN = 256
SEQS_HGRN = 8
TILE_ROUTE = 1024

_NT = (((1,), (1,)), ((), ()))
_TN = (((0,), (0,)), ((), ()))


def _cparams(*sem):
    return pltpu.CompilerParams(dimension_semantics=sem, vmem_limit_bytes=VMEM_LIMIT)


def _layer_norm(z, g, b):
    mu = jnp.mean(z, axis=-1, keepdims=True)
    zc = z - mu
    var = jnp.mean(zc * zc, axis=-1, keepdims=True)
    return zc * lax.rsqrt(var + LN_EPS) * g + b


def _dot(a, b):
    return jnp.dot(a, b, preferred_element_type=F32)


def _side_cast_plan(arrays, grid):
    steps = math.prod(grid)

    def block_index(*idx):
        lin = 0
        for g, i in zip(grid, idx):
            lin = lin * g + i
        return lin, 0

    specs, shapes = [], []
    for a in arrays:
        rows, cols = a.shape
        if rows % steps or (rows // steps) % 16:
            return None
        specs.append(pl.BlockSpec((rows // steps, cols), block_index))
        shapes.append(jax.ShapeDtypeStruct(a.shape, BF16))
    return specs, shapes


def _side_cast(in_refs, out_refs):
    for src, dst in zip(in_refs, out_refs):
        dst[...] = src[...].astype(dst.dtype)


def _kv_kernel(mem_ref, w_ref, k_ref, v_ref):
    kv = _dot(mem_ref[...].astype(BF16), w_ref[0])
    n = k_ref.shape[-1]
    k_ref[0] = kv[:, :n]
    v_ref[0] = kv[:, n:]


def _mem_kv(mem2d, w_kv):
    rows, d = mem2d.shape
    depth, _, e2 = w_kv.shape
    e = e2 // 2
    tm = min(rows, TILE_KV)
    out = jax.ShapeDtypeStruct((depth, rows, e), F32)
    return pl.pallas_call(
        _kv_kernel,
        grid=(depth, rows // tm),
        in_specs=[pl.BlockSpec((tm, d), lambda l, i: (i, 0)),
                  pl.BlockSpec((1, d, e2), lambda l, i: (l, 0, 0))],
        out_specs=[pl.BlockSpec((1, tm, e), lambda l, i: (l, i, 0)),
                   pl.BlockSpec((1, tm, e), lambda l, i: (l, i, 0))],
        out_shape=[out, out],
        compiler_params=_cparams("parallel", "parallel"),
        name="mem_kv",
    )(mem2d, w_kv)


def _window_sums(a, w):
    s, k = a, 1
    while k < w:
        s = s + pltpu.roll(s, k, 0)
        k *= 2
    return s


def _pool_mix(pooled_minus_tok, w_ref, scale):
    parts = [_dot(pooled_minus_tok[g].astype(BF16), w_ref[g]) for g in range(len(POOL_WINDOWS))]
    return jnp.concatenate(parts, axis=1) * scale


def _pool_prompt_kernel(*refs, alpha, n_side):
    x_ref, halo_ref, w_ref, scale_ref, g_ref, b_ref = refs[:6]
    o_ref = refs[6 + n_side]
    buf_ref = refs[-1]
    _side_cast(refs[6:6 + n_side], refs[7 + n_side:7 + 2 * n_side])
    t = pl.program_id(1)
    tt = x_ref.shape[1]
    gw = w_ref.shape[-1]
    x = x_ref[0]
    buf_ref[0:POOL_HALO, :] = jnp.where(t > 0, halo_ref[0], 0.0)
    buf_ref[POOL_HALO:, :] = x
    pos = t * tt + lax.broadcasted_iota(jnp.int32, (tt, 1), 0)
    diffs = []
    for g, w in enumerate(POOL_WINDOWS):
        sl = slice(g * gw, (g + 1) * gw)
        s = _window_sums(buf_ref[:, sl], w)[POOL_HALO:, :]
        cnt = jnp.minimum(pos + 1, w).astype(F32)
        diffs.append(s / cnt - x[:, sl])
    mixed = _pool_mix(diffs, w_ref, scale_ref[...])
    o_ref[0] = _layer_norm(alpha * x + mixed, g_ref[...], b_ref[...])


def _pool_prompt_grid(x):
    b, t, _ = x.shape
    return b, t // min(t, TILE_POOL)


def _pool_prompt(x, pool_w, scale, ln_g, ln_b, alpha, side=()):
    b, t, d = x.shape
    grid = _pool_prompt_grid(x)
    tt = t // grid[1]
    hb = tt // POOL_HALO
    vec = pl.BlockSpec((1, d), lambda i, j: (0, 0))
    side_specs, side_shapes = _side_cast_plan(side, grid)
    out = pl.pallas_call(
        functools.partial(_pool_prompt_kernel, alpha=alpha, n_side=len(side)),
        grid=grid,
        in_specs=[pl.BlockSpec((1, tt, d), lambda i, j: (i, j, 0)),
                  pl.BlockSpec((1, POOL_HALO, d), lambda i, j: (i, jnp.maximum(j * hb - 1, 0), 0)),
                  pl.BlockSpec(pool_w.shape, lambda i, j: (0, 0, 0)),
                  vec, vec, vec] + side_specs,
        out_specs=[pl.BlockSpec((1, tt, d), lambda i, j: (i, j, 0))] + side_specs,
        out_shape=[jax.ShapeDtypeStruct(x.shape, F32)] + side_shapes,
        scratch_shapes=[pltpu.VMEM((tt + POOL_HALO, d), F32)],
        compiler_params=_cparams("arbitrary", "arbitrary"),
        name="pool_prompt",
    )(x, x, pool_w, scale, ln_g, ln_b, *side)
    return out if side else out[0]


def _pool_sample_kernel(x_ref, prev_ref, w_ref, scale_ref, g_ref, b_ref, o_ref, buf_ref, *, alpha, n_prev):
    nb, t, d = x_ref.shape
    gw = w_ref.shape[-1]
    rows = POOL_HALO + t
    buf_ref[:, 0:POOL_HALO, :] = prev_ref[...]
    buf_ref[:, POOL_HALO:, :] = x_ref[...]
    x = x_ref[...].reshape(nb * t, d)
    pos = lax.broadcasted_iota(jnp.int32, (nb, t, 1), 1).reshape(nb * t, 1)
    diffs = []
    for g, w in enumerate(POOL_WINDOWS):
        sl = slice(g * gw, (g + 1) * gw)
        s = _window_sums(buf_ref[:, :, sl].reshape(nb * rows, gw), w)
        s = s.reshape(nb, rows, gw)[:, POOL_HALO:, :].reshape(nb * t, gw)
        cnt = jnp.minimum(pos + 1 + n_prev, w).astype(F32)
        diffs.append(s / cnt - x[:, sl])
    mixed = _pool_mix(diffs, w_ref, scale_ref[...])
    o_ref[...] = _layer_norm(alpha * x + mixed, g_ref[...], b_ref[...]).reshape(nb, t, d)


def _pool_sample(x, prev_padded, n_prev, pool_w, scale, ln_g, ln_b, alpha):
    b, t, d = x.shape
    nb = min(b, SEQS_POOL)
    vec = pl.BlockSpec((1, d), lambda i: (0, 0))
    return pl.pallas_call(
        functools.partial(_pool_sample_kernel, alpha=alpha, n_prev=n_prev),
        grid=(b // nb,),
        in_specs=[pl.BlockSpec((nb, t, d), lambda i: (i, 0, 0)),
                  pl.BlockSpec((nb, POOL_HALO, d), lambda i: (i, 0, 0)),
                  pl.BlockSpec(pool_w.shape, lambda i: (0, 0, 0)),
                  vec, vec, vec],
        out_specs=pl.BlockSpec((nb, t, d), lambda i: (i, 0, 0)),
        out_shape=jax.ShapeDtypeStruct(x.shape, F32),
        scratch_shapes=[pltpu.VMEM((nb, POOL_HALO + t, d), F32)],
        compiler_params=_cparams("parallel"),
        name="pool_sample",
    )(x, prev_padded, pool_w, scale, ln_g, ln_b)


def _attn_kernel(*refs, alpha, n_side):
    o_ref = refs[7 + n_side]
    _side_cast(refs[7:7 + n_side], refs[8 + n_side:8 + 2 * n_side])
    _attn_body(*refs[:7], o_ref, *refs[-3:], alpha)


def _attn_body(x_ref, k_ref, v_ref, wq_ref, wo_ref, g_ref, b_ref, o_ref, oh_ref, s_ref, p_ref, alpha):
    tt, d = x_ref.shape
    hd = d // XA_HEADS
    scale = hd ** -0.5
    heads = [slice(h * hd, (h + 1) * hd) for h in range(XA_HEADS)]
    n_parts = 2 if tt % 32 == 0 else 1
    parts = [slice(r * (tt // n_parts), (r + 1) * (tt // n_parts)) for r in range(n_parts)]
    xs = [x_ref[rows, :] for rows in parts]
    for rows, x in zip(parts, xs):
        oh_ref[rows, :] = _dot(x.astype(BF16), wq_ref[...]).astype(BF16)
    for rows in parts:
        for h, sl in enumerate(heads):
            s_ref[h, rows, :] = lax.dot_general(oh_ref[rows, sl], k_ref[:, sl].astype(BF16), _NT,
                                                preferred_element_type=F32) * scale
    for rows in parts:
        for h in range(XA_HEADS):
            s = s_ref[h, rows, :]
            p = jnp.exp(s - jnp.max(s, axis=-1, keepdims=True))
            p_ref[h, rows, :] = (p / jnp.sum(p, axis=-1, keepdims=True)).astype(BF16)
    for rows in parts:
        for h, sl in enumerate(heads):
            oh_ref[rows, sl] = _dot(p_ref[h, rows, :], v_ref[:, sl].astype(BF16)).astype(BF16)
    for rows, x in zip(parts, xs):
        a = _dot(oh_ref[rows, :], wo_ref[...])
        o_ref[rows, :] = _layer_norm(alpha * x + a, g_ref[...], b_ref[...])


def _mem_attend(x, k4, v4, layer, wq, wo, ln_g, ln_b, alpha, tt, side=()):
    b, t, d = x.shape
    m = k4.shape[2]
    grid = (b, t // tt)
    vec = pl.BlockSpec((1, d), lambda i, j: (0, 0))
    mat = pl.BlockSpec((d, d), lambda i, j: (0, 0))
    kv = pl.BlockSpec((None, None, m, d), lambda i, j: (layer, i, 0, 0))
    side_specs, side_shapes = _side_cast_plan(side, grid)
    out = pl.pallas_call(
        functools.partial(_attn_kernel, alpha=alpha, n_side=len(side)),
        grid=grid,
        in_specs=[pl.BlockSpec((None, tt, d), lambda i, j: (i, j, 0)), kv, kv, mat, mat, vec, vec] + side_specs,
        out_specs=[pl.BlockSpec((None, tt, d), lambda i, j: (i, j, 0))] + side_specs,
        out_shape=[jax.ShapeDtypeStruct(x.shape, F32)] + side_shapes,
        scratch_shapes=[pltpu.VMEM((tt, d), BF16), pltpu.VMEM((XA_HEADS, tt, m), F32),
                        pltpu.VMEM((XA_HEADS, tt, m), BF16)],
        compiler_params=_cparams("arbitrary", "arbitrary"),
        name=f"mem_attend_l{layer}",
    )(x, k4, v4, wq, wo, ln_g, ln_b, *side)
    return out if side else out[0]


def _attn_cache_kernel(*refs, alpha):
    _attn_cache_body(*refs, alpha)


def _attn_cache_body(x_ref, k_ref, v_ref, wq_ref, wo_ref, g_ref, b_ref, o_ref, q_scr, oh_scr, s_scr, p_scr, alpha):
    nb, t, d = x_ref.shape
    heads = XA_HEADS
    hd = d // heads
    halves = hd // LANES
    assert halves == 2
    rows = k_ref.shape[1]
    qr = heads * t
    scale = hd ** -0.5
    x = x_ref[...].reshape(nb * t, d)
    q = _dot(x.astype(BF16), wq_ref[...])
    for c in range(halves):
        for h in range(heads):
            col = (h * halves + c) * LANES
            q_scr[c, :, h * t:(h + 1) * t, :] = q[:, col:col + LANES].reshape(nb, t, LANES)

    col_id = lax.broadcasted_iota(jnp.int32, (qr, rows), 1)
    row_head = lax.broadcasted_iota(jnp.int32, (qr, rows), 0) // t
    col_half = (col_id // heads) % halves
    col_head = col_id % heads
    keep = jnp.logical_and(col_half == 1, col_head == row_head)

    for b in range(nb):
        kb = k_ref[b].astype(BF16)
        g0 = lax.dot_general(q_scr[0, b].astype(BF16), kb, _NT, preferred_element_type=F32)
        g1 = lax.dot_general(q_scr[1, b].astype(BF16), kb, _NT, preferred_element_type=F32)
        s_scr[b] = jnp.where(col_half == 0, g0, g1)
    for b in range(nb):
        part = s_scr[b]
        s = (part + pltpu.roll(part, heads, 1)) * scale
        s = jnp.where(keep, s, -jnp.inf)
        p = jnp.exp(s - jnp.max(s, axis=-1, keepdims=True))
        p = p / jnp.sum(p, axis=-1, keepdims=True)
        p_scr[0, b] = pltpu.roll(p, rows - heads, 1).astype(BF16)
        p_scr[1, b] = p.astype(BF16)
    for b in range(nb):
        vb = v_ref[b].astype(BF16)
        for c in range(halves):
            oc = _dot(p_scr[c, b], vb)
            for h in range(heads):
                col = (h * halves + c) * LANES
                oh_scr[b * t:(b + 1) * t, col:col + LANES] = oc[h * t:(h + 1) * t, :]
    a = _dot(oh_scr[...].astype(BF16), wo_ref[...])
    o_ref[...] = _layer_norm(alpha * x + a, g_ref[...], b_ref[...]).reshape(nb, t, d)


def _mem_attend_cache(x, k8, v8, layer, wq, wo, ln_g, ln_b, alpha, nb):
    b, t, d = x.shape
    rows = k8.shape[2]
    vec = pl.BlockSpec((1, d), lambda i: (0, 0))
    mat = pl.BlockSpec((d, d), lambda i: (0, 0))
    kv = pl.BlockSpec((None, nb, rows, LANES), lambda i: (layer, i, 0, 0))
    return pl.pallas_call(
        functools.partial(_attn_cache_kernel, alpha=alpha),
        grid=(b // nb,),
        in_specs=[pl.BlockSpec((nb, t, d), lambda i: (i, 0, 0)), kv, kv, mat, mat, vec, vec],
        out_specs=pl.BlockSpec((nb, t, d), lambda i: (i, 0, 0)),
        out_shape=jax.ShapeDtypeStruct(x.shape, F32),
        scratch_shapes=[pltpu.VMEM((2, nb, XA_HEADS * t, LANES), F32), pltpu.VMEM((nb * t, d), F32),
                        pltpu.VMEM((nb, XA_HEADS * t, rows), F32), pltpu.VMEM((2, nb, XA_HEADS * t, rows), BF16)],
        compiler_params=_cparams("parallel"),
        name=f"mem_attend_cache_l{layer}",
    )(x, k8, v8, wq, wo, ln_g, ln_b)


FFN_SUB = 512


def _swiglu(x_ref, wg_ref, wu_ref, wd_ref, h_ref):
    xb = x_ref[...].astype(BF16)
    f = h_ref.shape[1]
    for s in range(f // FFN_SUB):
        sl = slice(s * FFN_SUB, (s + 1) * FFN_SUB)
        gate = _dot(xb, wg_ref[0, :, sl])
        up = _dot(xb, wu_ref[0, :, sl])
        h_ref[:, sl] = (gate * jax.nn.sigmoid(gate) * up).astype(BF16)
    return _dot(h_ref[...], wd_ref[0])


def _ffn_ln_kernel(*refs, alpha, n_side):
    x_ref, wg_ref, wu_ref, wd_ref, g_ref, b_ref = refs[:6]
    o_ref = refs[6 + n_side]
    h_ref = refs[-1]
    y = _swiglu(x_ref, wg_ref, wu_ref, wd_ref, h_ref)
    o_ref[...] = _layer_norm(alpha * x_ref[...] + y, g_ref[...], b_ref[...])
    _side_cast(refs[6:6 + n_side], refs[7 + n_side:7 + 2 * n_side])


def _ffn_ln(x2d, w_gu, w_d, ln_g, ln_b, alpha, tm, side=()):
    n, d = x2d.shape
    tm = min(tm, n)
    f = w_d.shape[1]
    grid = (n // tm,)
    vec = pl.BlockSpec((1, d), lambda i: (0, 0))
    once = pl.Buffered(1)
    side_specs, side_shapes = _side_cast_plan(side, grid)
    out = pl.pallas_call(
        functools.partial(_ffn_ln_kernel, alpha=alpha, n_side=len(side)),
        grid=grid,
        in_specs=[pl.BlockSpec((tm, d), lambda i: (i, 0)),
                  pl.BlockSpec((1, d, f), lambda i: (0, 0, 0), pipeline_mode=once),
                  pl.BlockSpec((1, d, f), lambda i: (0, 0, 1), pipeline_mode=once),
                  pl.BlockSpec((1, f, d), lambda i: (0, 0, 0), pipeline_mode=once),
                  vec, vec] + side_specs,
        out_specs=[pl.BlockSpec((tm, d), lambda i: (i, 0))] + side_specs,
        out_shape=[jax.ShapeDtypeStruct((n, d), F32)] + side_shapes,
        scratch_shapes=[pltpu.VMEM((tm, f), BF16)],
        compiler_params=_cparams("parallel"),
        name="ffn_dense",
    )(x2d, w_gu, w_gu, w_d, ln_g, ln_b, *side)
    return out if side else out[0]


def _expert_kernel(eid_ref, used_ref, x_ref, wg_ref, wu_ref, wd_ref, o_ref, h_ref):
    i = pl.program_id(0)

    @pl.when(used_ref[i] > 0)
    def _():
        o_ref[...] = _swiglu(x_ref, wg_ref, wu_ref, wd_ref, h_ref)

    @pl.when(used_ref[i] == 0)
    def _():
        o_ref[...] = jnp.zeros_like(o_ref)


def _expert_ffn(xs, tile_eid, tile_used, w_gu, w_d, tm):
    r, d = xs.shape
    f = w_d.shape[1]
    grid_spec = pltpu.PrefetchScalarGridSpec(
        num_scalar_prefetch=2,
        grid=(r // tm,),
        in_specs=[pl.BlockSpec((tm, d), lambda i, e, u: (i, 0)),
                  pl.BlockSpec((1, d, f), lambda i, e, u: (e[i], 0, 0)),
                  pl.BlockSpec((1, d, f), lambda i, e, u: (e[i], 0, 1)),
                  pl.BlockSpec((1, f, d), lambda i, e, u: (e[i], 0, 0))],
        out_specs=pl.BlockSpec((tm, d), lambda i, e, u: (i, 0)),
        scratch_shapes=[pltpu.VMEM((tm, f), BF16)],
    )
    return pl.pallas_call(
        _expert_kernel,
        grid_spec=grid_spec,
        out_shape=jax.ShapeDtypeStruct((r, d), F32),
        compiler_params=_cparams("arbitrary"),
        name="ffn_experts",
    )(tile_eid, tile_used, xs, w_gu, w_gu, w_d)


def _split3(x):
    hi = x.astype(BF16).astype(F32)
    r = x - hi
    mid = r.astype(BF16).astype(F32)
    lo = (r - mid).astype(BF16).astype(F32)
    return hi, mid, lo


def _hgrn_tile_kernel(*refs, alpha, layer, chunk, has_state):
    st_refs = refs[-HG_HEADS:]
    refs = refs[:-HG_HEADS]
    s0_ref = None
    if has_state:
        s0_ref, refs = refs[1], refs[:1] + refs[2:]
    (x_ref, w_in_ref, lbp_ref, ng_ref, w_o_ref, g_ref, b_ref, o_ref, sfin_ref,
     proj_ref, lf_ref, bcum_ref, qrel_ref, krel_ref, kend_ref, qdec_ref, vb_ref, dec_ref,
     u_ref, oacc_ref, og_ref) = refs
    t = pl.program_id(1)
    nb, tt, d = x_ref.shape
    heads = HG_HEADS
    dv = ng_ref.shape[-1]
    vdim = heads * dv
    kdim = (w_in_ref.shape[1] - 2 * vdim) // 2
    dk = kdim // heads
    m = nb * tt
    c = chunk
    n_chunks = m // c
    chunks_per_seq = tt // c
    mid = max(c // 2 - 1, 0)

    @pl.when(t == 0)
    def _():
        for h in range(heads):
            if has_state:
                for bi in range(nb):
                    st_refs[h][bi] = s0_ref[bi, h]
            else:
                st_refs[h][...] = jnp.zeros_like(st_refs[h])

    x = x_ref[...].reshape(m, d)
    xb = x.astype(BF16)

    lbp = lbp_ref[...]
    e = jnp.exp(lbp - jnp.max(lbp, axis=0, keepdims=True))
    sm = e / jnp.sum(e, axis=0, keepdims=True)
    lb = jnp.sum(sm[1:layer + 1], axis=0, keepdims=True)

    f_cols = slice(kdim, 2 * kdim)
    proj_ref[:, f_cols] = _dot(xb, w_in_ref[:, f_cols])
    for j in range(n_chunks):
        rows = slice(j * c, (j + 1) * c)
        f = lb + (1.0 - lb) * jax.nn.sigmoid(proj_ref[rows, f_cols])
        proj_ref[rows, f_cols] = 1.0 - f
        for p, part in enumerate(_split3(jnp.log(f))):
            lf_ref[p, rows, :] = part.astype(lf_ref.dtype)
    proj_ref[:, 0:kdim] = _dot(xb, w_in_ref[:, 0:kdim])
    proj_ref[:, 2 * kdim:] = _dot(xb, w_in_ref[:, 2 * kdim:])

    ri = lax.broadcasted_iota(jnp.int32, (m, m), 0)
    ci = lax.broadcasted_iota(jnp.int32, (m, m), 1)
    causal = jnp.logical_and(ri // c == ci // c, ri >= ci)
    tri = causal.astype(BF16)
    bcum_ref[...] = (_dot(tri, lf_ref[0].astype(BF16)) + _dot(tri, lf_ref[1].astype(BF16))
                     + _dot(tri, lf_ref[2].astype(BF16)))

    sdt = qrel_ref.dtype
    vb_ref[...] = proj_ref[:, 2 * kdim:2 * kdim + vdim].astype(sdt)
    for j in range(n_chunks):
        rows = slice(j * c, (j + 1) * c)
        bcum = bcum_ref[rows, :]
        bmid = bcum[mid:mid + 1]
        blast = bcum[c - 1:c]
        q_dec = proj_ref[rows, 0:kdim] * jnp.exp(bcum)
        k_rel = proj_ref[rows, f_cols] * jnp.exp(bmid - bcum)
        qdec_ref[rows, :] = q_dec.astype(sdt)
        qrel_ref[rows, :] = (q_dec * jnp.exp(-bmid)).astype(sdt)
        krel_ref[rows, :] = k_rel.astype(sdt)
        kend_ref[rows, :] = (k_rel * jnp.exp(blast - bmid)).astype(sdt)
        dec_ref[j:j + 1, :] = jnp.exp(blast)
    if n_chunks < dec_ref.shape[0]:
        dec_ref[n_chunks:, :] = jnp.zeros((dec_ref.shape[0] - n_chunks, kdim), F32)
    dec_cols = dec_ref[...].T

    for h in range(heads):
        sk = slice(h * dk, (h + 1) * dk)
        sv = slice(h * dv, (h + 1) * dv)
        a = lax.dot_general(qrel_ref[:, sk].astype(BF16), krel_ref[:, sk].astype(BF16), _NT,
                            preferred_element_type=F32)
        a = jnp.where(causal, a, 0.0).astype(BF16)
        oacc_ref[:, sv] = _dot(a, vb_ref[:, sv].astype(BF16))

    for j in range(n_chunks):
        rows = slice(j * c, (j + 1) * c)
        for h in range(heads):
            sk = slice(h * dk, (h + 1) * dk)
            sv = slice(h * dv, (h + 1) * dv)
            u_ref[j, h] = lax.dot_general(kend_ref[rows, sk].astype(BF16), vb_ref[rows, sv].astype(BF16), _TN,
                                          preferred_element_type=F32)

    for j in range(n_chunks):
        rows = slice(j * c, (j + 1) * c)
        bi = j // chunks_per_seq
        for h in range(heads):
            sk = slice(h * dk, (h + 1) * dk)
            sv = slice(h * dv, (h + 1) * dv)
            st = st_refs[h][bi]
            oacc_ref[rows, sv] += _dot(qdec_ref[rows, sk].astype(BF16), st.astype(BF16))
            st_refs[h][bi] = st * dec_cols[sk, j:j + 1] + u_ref[j, h]

    ng = ng_ref[...]
    for h in range(heads):
        sv = slice(h * dv, (h + 1) * dv)
        o = oacc_ref[:, sv]
        o = o * lax.rsqrt(jnp.mean(o * o, axis=-1, keepdims=True) + RMS_EPS) * ng
        gate = proj_ref[:, 2 * kdim + vdim + h * dv:2 * kdim + vdim + (h + 1) * dv]
        og_ref[:, sv] = (o * (gate * jax.nn.sigmoid(gate))).astype(BF16)
    out = _dot(og_ref[...], w_o_ref[...])
    o_ref[...] = _layer_norm(alpha * x + out, g_ref[...], b_ref[...]).reshape(nb, tt, d)

    @pl.when(t == pl.num_programs(1) - 1)
    def _():
        for bi in range(nb):
            for h in range(heads):
                sfin_ref[bi, h] = st_refs[h][bi]


def _hgrn_tiled(x, s0, w_in, lb_param, norm_g, w_o, ln_g, ln_b, alpha, layer, chunk, nb, tt):
    b, t, d = x.shape
    heads = HG_HEADS
    dv = norm_g.shape[-1]
    vdim = heads * dv
    kdim = (w_in.shape[1] - 2 * vdim) // 2
    dk = kdim // heads
    m = nb * tt
    n_chunks = m // chunk
    has_state = s0 is not None
    const2 = lambda i, j: (0, 0)
    state_spec = pl.BlockSpec((nb, heads, dk, dv), lambda i, j: (i, 0, 0, 0))
    in_specs = [pl.BlockSpec((nb, tt, d), lambda i, j: (i, j, 0))]
    args = [x]
    if has_state:
        in_specs.append(state_spec)
        args.append(s0)
    in_specs += [pl.BlockSpec(w_in.shape, const2), pl.BlockSpec(lb_param.shape, const2),
                 pl.BlockSpec(norm_g.shape, const2), pl.BlockSpec(w_o.shape, const2),
                 pl.BlockSpec((1, d), const2), pl.BlockSpec((1, d), const2)]
    args += [w_in, lb_param, norm_g, w_o, ln_g, ln_b]
    sdt = BF16 if chunk % 16 == 0 else F32
    scratch = [pltpu.VMEM((m, w_in.shape[1]), F32),
               pltpu.VMEM((3, m, kdim), sdt),
               pltpu.VMEM((m, kdim), F32),
               pltpu.VMEM((m, kdim), sdt),
               pltpu.VMEM((m, kdim), sdt),
               pltpu.VMEM((m, kdim), sdt),
               pltpu.VMEM((m, kdim), sdt),
               pltpu.VMEM((m, vdim), sdt),
               pltpu.VMEM((max(n_chunks, 8), kdim), F32),
               pltpu.VMEM((n_chunks, heads, dk, dv), F32),
               pltpu.VMEM((m, vdim), F32),
               pltpu.VMEM((m, vdim), BF16)]
    scratch += [pltpu.VMEM((nb, dk, dv), F32) for _ in range(heads)]
    return pl.pallas_call(
        functools.partial(_hgrn_tile_kernel, alpha=alpha, layer=layer, chunk=chunk, has_state=has_state),
        grid=(b // nb, t // tt),
        in_specs=in_specs,
        out_specs=[pl.BlockSpec((nb, tt, d), lambda i, j: (i, j, 0)), state_spec],
        out_shape=[jax.ShapeDtypeStruct(x.shape, F32),
                   jax.ShapeDtypeStruct((b, heads, dk, dv), F32)],
        scratch_shapes=scratch,
        compiler_params=_cparams("parallel", "arbitrary"),
        name="hgrn_state" if has_state else "hgrn_fresh",
    )(*args)


_I_E0, _I_E1, _I_G0, _I_G1, _I_R0, _I_R1 = range(6)
ROUTE_ROWS = 8


def _router_kernel(x_ref, w_ref, cnt0_ref, info_ref, route_ref, cnt_ref, carry_ref):
    i = pl.program_id(0)
    tm = x_ref.shape[0]

    @pl.when(i == 0)
    def _():
        carry_ref[...] = cnt0_ref[...]

    x = x_ref[...]
    w = w_ref[...]
    x_hi = x.astype(BF16)
    x_lo = (x - x_hi.astype(F32)).astype(BF16)
    w_hi = w.astype(BF16)
    w_lo = (w - w_hi.astype(F32)).astype(BF16)
    hi_both = _dot(x_hi, jnp.concatenate([w_hi, w_lo], axis=1))
    logits = hi_both[:, :LANES] + (hi_both[:, LANES:] + _dot(x_lo, w_hi))
    lane = lax.broadcasted_iota(jnp.int32, (tm, LANES), 1)
    valid = lane < N_EXPERTS
    logits = jnp.where(valid, logits, -jnp.inf)
    p = jnp.exp(logits - jnp.max(logits, axis=-1, keepdims=True))
    p = p / jnp.sum(p, axis=-1, keepdims=True)
    p0 = jnp.where(valid, p, -1.0)
    v0 = jnp.max(p0, axis=-1, keepdims=True)
    e0 = jnp.min(jnp.where(p0 == v0, lane, LANES), axis=-1, keepdims=True)
    p1 = jnp.where(lane == e0, -1.0, p0)
    v1 = jnp.max(p1, axis=-1, keepdims=True)
    e1 = jnp.min(jnp.where(p1 == v1, lane, LANES), axis=-1, keepdims=True)
    den = v0 + v1
    sel = jnp.logical_or(lane == e0, lane == e1)
    ri = lax.broadcasted_iota(jnp.int32, (tm, tm), 0)
    ci = lax.broadcasted_iota(jnp.int32, (tm, tm), 1)
    before = (ri > ci).astype(BF16)
    rank = _dot(before, sel.astype(BF16)) + carry_ref[...]
    r0 = jnp.sum(jnp.where(lane == e0, rank, 0.0), axis=-1, keepdims=True)
    r1 = jnp.sum(jnp.where(lane == e1, rank, 0.0), axis=-1, keepdims=True)
    carry_ref[...] += jnp.sum(sel.astype(F32), axis=0, keepdims=True)
    info = jnp.zeros((tm, LANES), F32)
    for idx, val in ((_I_E0, e0.astype(F32)), (_I_E1, e1.astype(F32)), (_I_G0, v0 / den),
                     (_I_G1, v1 / den), (_I_R0, r0), (_I_R1, r1)):
        info = jnp.where(lane == idx, val, info)
    info_ref[...] = info
    route_ref[...] = info.T[0:route_ref.shape[0], :]
    cnt_ref[...] = carry_ref[...]


def _router(x2d, w_router_padded, counts_in, tm):
    n, d = x2d.shape
    return pl.pallas_call(
        _router_kernel,
        grid=(n // tm,),
        in_specs=[pl.BlockSpec((tm, d), lambda i: (i, 0)),
                  pl.BlockSpec((d, LANES), lambda i: (0, 0)),
                  pl.BlockSpec((1, LANES), lambda i: (0, 0))],
        out_specs=[pl.BlockSpec((tm, LANES), lambda i: (i, 0)),
                   pl.BlockSpec((ROUTE_ROWS, tm), lambda i: (0, i)),
                   pl.BlockSpec((1, LANES), lambda i: (0, 0))],
        out_shape=[jax.ShapeDtypeStruct((n, LANES), F32), jax.ShapeDtypeStruct((ROUTE_ROWS, n), F32),
                   jax.ShapeDtypeStruct((1, LANES), F32)],
        scratch_shapes=[pltpu.VMEM((1, LANES), F32)],
        compiler_params=_cparams("arbitrary"),
        name="moe_router",
    )(x2d, w_router_padded, counts_in)


def _positions_kernel(offs_ref, route_ref, pos_ref):
    route = route_ref[...].astype(jnp.int32)
    expert = route[_I_E0:_I_E1 + 1]
    start = jnp.zeros_like(expert)
    for e in range(N_EXPERTS):
        start = jnp.where(expert == e, offs_ref[e], start)
    pos_ref[...] = start + route[_I_R0:_I_R1 + 1]


def _positions(route, offs, tn):
    n = route.shape[1]
    grid_spec = pltpu.PrefetchScalarGridSpec(
        num_scalar_prefetch=1,
        grid=(n // tn,),
        in_specs=[pl.BlockSpec((ROUTE_ROWS, tn), lambda i, o: (0, i))],
        out_specs=pl.BlockSpec((TOP_K, tn), lambda i, o: (0, i)),
    )
    return pl.pallas_call(
        _positions_kernel,
        grid_spec=grid_spec,
        out_shape=jax.ShapeDtypeStruct((TOP_K, n), jnp.int32),
        compiler_params=_cparams("parallel"),
        name="moe_positions",
    )(offs, route)


def _row_copy(src, src_row, dst, dst_row, sem):
    return pltpu.make_async_copy(src.at[pl.ds(src_row, 1)], dst.at[pl.ds(dst_row, 1)], sem)


def _scatter_rows(pos_refs, x_ref, out_hbm, sem):
    tb = x_ref.shape[0]

    def issue(j, carry):
        for pos_ref in pos_refs:
            _row_copy(x_ref, j, out_hbm, pos_ref[j], sem).start()
        return carry

    lax.fori_loop(0, tb, issue, 0, unroll=4)
    for _ in range(TOP_K):
        pltpu.make_async_copy(x_ref, out_hbm.at[pl.ds(0, tb)], sem).wait()


def _dispatch_kernel(pos0_ref, pos1_ref, fill_ref, xa_ref, xb_ref, out_hbm, zero_ref, sem, zsem, *, steps_a):
    i = pl.program_id(0)
    pos_refs = (pos0_ref, pos1_ref)

    @pl.when(i == 0)
    def _():
        zero_ref[...] = jnp.zeros_like(zero_ref)
        tz = zero_ref.shape[0]

        def fill(e):
            row = pl.multiple_of(jnp.maximum(fill_ref[e], 0), tz)
            return pltpu.make_async_copy(zero_ref, out_hbm.at[pl.ds(row, tz)], zsem)

        for e in range(fill_ref.shape[0]):
            pl.when(fill_ref[e] >= 0)(lambda e=e: fill(e).start())
        for e in range(fill_ref.shape[0]):
            pl.when(fill_ref[e] >= 0)(lambda e=e: fill(e).wait())

    pl.when(i < steps_a)(lambda: _scatter_rows(pos_refs, xa_ref, out_hbm, sem))
    pl.when(i >= steps_a)(lambda: _scatter_rows(pos_refs, xb_ref, out_hbm, sem))


def _dispatch(xa, xb, pos, fill_rows, n_rows, tz):
    d = xa.shape[1]
    tb = min(TILE_ROUTE, xa.shape[0], xb.shape[0])
    steps_a = xa.shape[0] // tb
    steps_b = xb.shape[0] // tb
    pos_spec = pl.BlockSpec((tb,), lambda i: (i,), memory_space=pltpu.SMEM)
    return pl.pallas_call(
        functools.partial(_dispatch_kernel, steps_a=steps_a),
        grid=(steps_a + steps_b,),
        in_specs=[pos_spec, pos_spec,
                  pl.BlockSpec(memory_space=pltpu.SMEM),
                  pl.BlockSpec((tb, d), lambda i: (jnp.minimum(i, steps_a - 1), 0)),
                  pl.BlockSpec((tb, d), lambda i: (jnp.maximum(i - steps_a, 0), 0))],
        out_specs=pl.BlockSpec(memory_space=pl.ANY),
        out_shape=jax.ShapeDtypeStruct((n_rows, d), xa.dtype),
        scratch_shapes=[pltpu.VMEM((tz, d), xa.dtype), pltpu.SemaphoreType.DMA(()),
                        pltpu.SemaphoreType.DMA(())],
        compiler_params=_cparams("arbitrary"),
        name="moe_dispatch",
    )(pos[0], pos[1], fill_rows, xa, xb)


def _combine_kernel(pos0_ref, pos1_ref, pos0n_ref, pos1n_ref, x_ref, info_ref, y_hbm, g_ref, b_ref, o_ref,
                    ybuf, sem, *, alpha):
    i = pl.program_id(0)
    n = pl.num_programs(0)
    tm = x_ref.shape[0]

    def issue(pos_refs, slot):
        def body(j, carry):
            for kk, pos_ref in enumerate(pos_refs):
                _row_copy(y_hbm, pos_ref[j], ybuf.at[slot, kk], j, sem.at[slot]).start()
            return carry
        lax.fori_loop(0, tm, body, 0, unroll=8)

    @pl.when(i == 0)
    def _():
        issue((pos0_ref, pos1_ref), 0)

    @pl.when(i + 1 < n)
    def _():
        issue((pos0n_ref, pos1n_ref), (i + 1) % 2)

    slot = i % 2
    for kk in range(TOP_K):
        pltpu.make_async_copy(y_hbm.at[pl.ds(0, tm)], ybuf.at[slot, kk], sem.at[slot]).wait()
    info = info_ref[...]
    moe = info[:, _I_G0:_I_G0 + 1] * ybuf[slot, 0] + info[:, _I_G1:_I_G1 + 1] * ybuf[slot, 1]
    o_ref[...] = _layer_norm(alpha * x_ref[...] + moe, g_ref[...], b_ref[...])


def _combine_ln(x2d, info, pos, y, ln_g, ln_b, alpha, tm):
    n, d = x2d.shape
    nt = n // tm
    vec = pl.BlockSpec((1, d), lambda i: (0, 0))
    pos_now = pl.BlockSpec((tm,), lambda i: (i,), memory_space=pltpu.SMEM)
    pos_next = pl.BlockSpec((tm,), lambda i: (jnp.minimum(i + 1, nt - 1),), memory_space=pltpu.SMEM)
    return pl.pallas_call(
        functools.partial(_combine_kernel, alpha=alpha),
        grid=(nt,),
        in_specs=[pos_now, pos_now, pos_next, pos_next,
                  pl.BlockSpec((tm, d), lambda i: (i, 0)),
                  pl.BlockSpec((tm, LANES), lambda i: (i, 0)),
                  pl.BlockSpec(memory_space=pl.ANY),
                  vec, vec],
        out_specs=pl.BlockSpec((tm, d), lambda i: (i, 0)),
        out_shape=jax.ShapeDtypeStruct((n, d), F32),
        scratch_shapes=[pltpu.VMEM((2, TOP_K, tm, d), F32), pltpu.SemaphoreType.DMA((2,))],
        compiler_params=_cparams("arbitrary"),
        name="moe_combine",
    )(pos[0], pos[1], pos[0], pos[1], x2d, info, y, ln_g, ln_b)


def _moe_ln(xs2d, w_router, w_gu, w_d, ln_g, ln_b, alpha, tm_expert):
    d = xs2d[0].shape[1]
    n_total = sum(x.shape[0] for x in xs2d)
    w_r = jnp.pad(w_router, ((0, 0), (0, LANES - N_EXPERTS)))
    counts = jnp.zeros((1, LANES), F32)
    infos, routes = [], []
    for x in xs2d:
        info, route, counts = _router(x, w_r, counts, min(x.shape[0], TILE_ROUTE))
        infos.append(info)
        routes.append(route)
    cnt = counts[0, :N_EXPERTS].astype(jnp.int32)
    padded = (cnt + tm_expert - 1) // tm_expert * tm_expert
    ends = jnp.cumsum(padded)
    offs = ends - padded
    n_rows = -(-(TOP_K * n_total) // tm_expert) * tm_expert + N_EXPERTS * tm_expert
    tile_start = jnp.arange(n_rows // tm_expert, dtype=jnp.int32) * tm_expert
    tile_eid = jnp.minimum(jnp.sum(tile_start[:, None] >= ends[None, :], axis=1), N_EXPERTS - 1).astype(jnp.int32)
    tile_used = (tile_start < offs[tile_eid] + cnt[tile_eid]).astype(jnp.int32)
    tail = ends[-1] + jnp.arange(N_EXPERTS, dtype=jnp.int32) * tm_expert
    fill_rows = jnp.concatenate([jnp.where(padded > 0, ends - tm_expert, -1),
                                 jnp.where(tail < n_rows, tail, -1)]).astype(jnp.int32)
    poss = [_positions(route, offs.astype(jnp.int32), min(route.shape[1], TILE_ROUTE)) for route in routes]
    xa, xb = xs2d
    buf = _dispatch(xa, xb, jnp.concatenate(poss, axis=1), fill_rows, n_rows, tm_expert)
    y = _expert_ffn(buf, tile_eid, tile_used, w_gu, w_d, tm_expert)
    return [_combine_ln(x, info, pos, y, ln_g, ln_b, alpha, min(x.shape[0], TILE_ROUTE))
            for x, info, pos in zip(xs2d, infos, poss)]


def kernel(x_prompt, x_sample, mem_prompt, state_pool, state_hgrn, cache_mem_k, cache_mem_v, pool_w, pool_scale, hg_w_in, hg_lb, hg_norm_g, hg_w_o, xa_w_q, xa_w_kv, xa_w_o, ffn_w_gu, ffn_w_d, moe_w_router, moe_w_gu, moe_w_d, ln_g, ln_b):
    b, t, d = x_prompt.shape
    sb, st, _ = x_sample.shape
    depth = ln_g.shape[0]
    mem_len = mem_prompt.shape[1]
    alpha = (2 * depth) ** 0.25
    n_prev = state_pool.shape[2]

    pool_w_b = pool_w.astype(BF16)
    xa_w_kv_b = xa_w_kv.astype(BF16)

    def vec(a):
        return a.reshape(1, -1)

    mk, mv = _mem_kv(mem_prompt.reshape(b * mem_len, d), xa_w_kv_b)
    mk4 = mk.reshape(depth, b, mem_len, d)
    mv4 = mv.reshape(depth, b, mem_len, d)

    def stored_order(c):
        halves = c.shape[4] // LANES
        c = c.reshape(depth, sb, mem_len, XA_HEADS, halves, LANES).transpose(0, 1, 2, 4, 3, 5)
        return c.reshape(depth, sb, mem_len * halves * XA_HEADS, LANES)

    ck8 = stored_order(cache_mem_k)
    cv8 = stored_order(cache_mem_v)

    tt_attn = min(t, TILE_ATTN)
    nb_attn = min(sb, SEQS_ATTN)

    def attend(xp, xs, layer, side=()):
        args = (xa_w_q_b[layer], xa_w_o_b[layer], vec(ln_g[layer, 1]), vec(ln_b[layer, 1]), alpha)
        outp = _mem_attend(xp, mk4, mv4, layer, *args, tt_attn, side)
        outs = _mem_attend_cache(xs, ck8, cv8, layer, *args, nb_attn)
        return (outp[0], outs, outp[1:]) if side else (outp, outs, [])

    def cast_beside(weights, grid):
        views = tuple(w.reshape(-1, w.shape[-1]) for w in weights)
        if _side_cast_plan(views, grid) is None:
            return (), tuple(w.astype(BF16) for w in weights)
        return views, None

    pool_args = (pool_w_b[0], vec(pool_scale[0]), vec(ln_g[0, 0]), vec(ln_b[0, 0]), alpha)
    early_w = (xa_w_q, xa_w_o, hg_w_in, hg_w_o)
    side, early_w_b = cast_beside(early_w, _pool_prompt_grid(x_prompt))
    xp = _pool_prompt(x_prompt, *pool_args, side)
    if side:
        xp, early_w_b = xp[0], tuple(c.reshape(w.shape) for c, w in zip(xp[1:], early_w))
    xa_w_q_b, xa_w_o_b, hg_w_in_b, hg_w_o_b = early_w_b
    prev = jnp.pad(state_pool[0], ((0, 0), (POOL_HALO - n_prev, 0), (0, 0)))
    xs = _pool_sample(x_sample, prev, n_prev, *pool_args)
    ffn_w = (ffn_w_gu, ffn_w_d)
    side, ffn_w_b = cast_beside(ffn_w, (b, t // tt_attn))
    xp, xs, casts = attend(xp, xs, 0, side)
    if side:
        ffn_w_b = tuple(c.reshape(w.shape) for c, w in zip(casts, ffn_w))
    ffn_args = (*ffn_w_b, vec(ln_g[0, 2]), vec(ln_b[0, 2]), alpha)
    moe_w = (moe_w_gu, moe_w_d)
    tm_ffn = min(TILE_FFN_CASTING, b * t)
    side, moe_w_b = cast_beside(moe_w, (b * t // tm_ffn,))
    xp = _ffn_ln(xp.reshape(b * t, d), *ffn_args, tm_ffn, side)
    if side:
        xp, moe_w_b = xp[0], tuple(c.reshape(w.shape) for c, w in zip(xp[1:], moe_w))
    xp = xp.reshape(b, t, d)
    xs = _ffn_ln(xs.reshape(sb * st, d), *ffn_args, TILE_FFN).reshape(sb, st, d)

    hg_args = (hg_w_in_b[0], hg_lb, vec(hg_norm_g[0]), hg_w_o_b[0], vec(ln_g[1, 0]), vec(ln_b[1, 0]), alpha, 1)
    xp, hg_p = _hgrn_tiled(xp, None, *hg_args, 2 * HG_CHUNK, 1, min(t, TILE_HGRN))
    xs, hg_s = _hgrn_tiled(xs, state_hgrn[0], *hg_args, math.gcd(st, HG_CHUNK), min(sb, SEQS_HGRN), st)
    xp, xs, _ = attend(xp, xs, 1)
    yp, ys = _moe_ln([xp.reshape(b * t, d), xs.reshape(sb * st, d)], moe_w_router[0], moe_w_b[0][0],
                     moe_w_b[1][0], vec(ln_g[1, 2]), vec(ln_b[1, 2]), alpha, TILE_EXPERT)

    heads_hd = (depth, b, mem_len, XA_HEADS, d // XA_HEADS)
    pool_p = x_prompt[:, t - n_prev:, :][None]
    pool_s = jnp.concatenate([state_pool[0], x_sample], axis=1)[:, st:, :][None]
    return (yp.reshape(b, t, d), ys.reshape(sb, st, d), pool_p, hg_p[None],
            mk.reshape(heads_hd), mv.reshape(heads_hd), pool_s, hg_s[None])
```

```python
import functools
import math

import jax
import jax.numpy as jnp
from jax import lax
from jax.experimental import pallas as pl
from jax.experimental.pallas import tpu as pltpu

F32 = jnp.float32
BF16 = jnp.bfloat16

POOL_WINDOWS = (2, 4, 8, 16)
POOL_HALO = 16
HG_HEADS = 8
HG_CHUNK = 32
RMS_EPS = 1e-6
XA_HEADS = 4
N_EXPERTS = 8
TOP_K = 2
LN_EPS = 1e-5
LANES = 128
VMEM_LIMIT = 62 * 1024 * 1024

TILE_KV = 512
TILE_POOL = 1024
SEQS_POOL = 16
TILE_ATTN = 1024
SEQS_ATTN = 8
TILE_FFN = 1024
TILE_FFN_CASTING = 256
TILE_EXPERT = 512
TILE_HGRN = 256
SEQS_HGRN = 8
TILE_ROUTE = 1024

_NT = (((1,), (1,)), ((), ()))
_TN = (((0,), (0,)), ((), ()))


def _cparams(*sem):
    return pltpu.CompilerParams(dimension_semantics=sem, vmem_limit_bytes=VMEM_LIMIT)


def _layer_norm(z, g, b):
    mu = jnp.mean(z, axis=-1, keepdims=True)
    zc = z - mu
    var = jnp.mean(zc * zc, axis=-1, keepdims=True)
    return zc * lax.rsqrt(var + LN_EPS) * g + b


def _dot(a, b):
    return jnp.dot(a, b, preferred_element_type=F32)


def _side_cast_plan(arrays, grid):
    steps = math.prod(grid)

    def block_index(*idx):
        lin = 0
        for g, i in zip(grid, idx):
            lin = lin * g + i
        return lin, 0

    specs, shapes = [], []
    for a in arrays:
        rows, cols = a.shape
        if rows % steps or (rows // steps) % 16:
            return None
        specs.append(pl.BlockSpec((rows // steps, cols), block_index))
        shapes.append(jax.ShapeDtypeStruct(a.shape, BF16))
    return specs, shapes


def _side_cast(in_refs, out_refs):
    for src, dst in zip(in_refs, out_refs):
        dst[...] = src[...].astype(dst.dtype)


def _kv_kernel(mem_ref, w_ref, k_ref, v_ref):
    kv = _dot(mem_ref[...].astype(BF16), w_ref[0])
    n = k_ref.shape[-1]
    k_ref[0] = kv[:, :n]
    v_ref[0] = kv[:, n:]


def _mem_kv(mem2d, w_kv):
    rows, d = mem2d.shape
    depth, _, e2 = w_kv.shape
    e = e2 // 2
    tm = min(rows, TILE_KV)
    out = jax.ShapeDtypeStruct((depth, rows, e), F32)
    return pl.pallas_call(
        _kv_kernel,
        grid=(depth, rows // tm),
        in_specs=[pl.BlockSpec((tm, d), lambda l, i: (i, 0)),
                  pl.BlockSpec((1, d, e2), lambda l, i: (l, 0, 0))],
        out_specs=[pl.BlockSpec((1, tm, e), lambda l, i: (l, i, 0)),
                   pl.BlockSpec((1, tm, e), lambda l, i: (l, i, 0))],
        out_shape=[out, out],
        compiler_params=_cparams("parallel", "parallel"),
        name="mem_kv",
    )(mem2d, w_kv)


def _window_sums(a, w):
    s, k = a, 1
    while k < w:
        s = s + pltpu.roll(s, k, 0)
        k *= 2
    return s


def _pool_mix(pooled_minus_tok, w_ref, scale):
    parts = [_dot(pooled_minus_tok[g].astype(BF16), w_ref[g]) for g in range(len(POOL_WINDOWS))]
    return jnp.concatenate(parts, axis=1) * scale


def _pool_prompt_kernel(*refs, alpha, n_side):
    x_ref, halo_ref, w_ref, scale_ref, g_ref, b_ref = refs[:6]
    o_ref = refs[6 + n_side]
    buf_ref = refs[-1]
    _side_cast(refs[6:6 + n_side], refs[7 + n_side:7 + 2 * n_side])
    t = pl.program_id(1)
    tt = x_ref.shape[1]
    gw = w_ref.shape[-1]
    x = x_ref[0]
    buf_ref[0:POOL_HALO, :] = jnp.where(t > 0, halo_ref[0], 0.0)
    buf_ref[POOL_HALO:, :] = x
    pos = t * tt + lax.broadcasted_iota(jnp.int32, (tt, 1), 0)
    diffs = []
    for g, w in enumerate(POOL_WINDOWS):
        sl = slice(g * gw, (g + 1) * gw)
        s = _window_sums(buf_ref[:, sl], w)[POOL_HALO:, :]
        cnt = jnp.minimum(pos + 1, w).astype(F32)
        diffs.append(s / cnt - x[:, sl])
    mixed = _pool_mix(diffs, w_ref, scale_ref[...])
    o_ref[0] = _layer_norm(alpha * x + mixed, g_ref[...], b_ref[...])


def _pool_prompt_grid(x):
    b, t, _ = x.shape
    return b, t // min(t, TILE_POOL)


def _pool_prompt(x, pool_w, scale, ln_g, ln_b, alpha, side=()):
    b, t, d = x.shape
    grid = _pool_prompt_grid(x)
    tt = t // grid[1]
    hb = tt // POOL_HALO
    vec = pl.BlockSpec((1, d), lambda i, j: (0, 0))
    side_specs, side_shapes = _side_cast_plan(side, grid)
    out = pl.pallas_call(
        functools.partial(_pool_prompt_kernel, alpha=alpha, n_side=len(side)),
        grid=grid,
        in_specs=[pl.BlockSpec((1, tt, d), lambda i, j: (i, j, 0)),
                  pl.BlockSpec((1, POOL_HALO, d), lambda i, j: (i, jnp.maximum(j * hb - 1, 0), 0)),
                  pl.BlockSpec(pool_w.shape, lambda i, j: (0, 0, 0)),
                  vec, vec, vec] + side_specs,
        out_specs=[pl.BlockSpec((1, tt, d), lambda i, j: (i, j, 0))] + side_specs,
        out_shape=[jax.ShapeDtypeStruct(x.shape, F32)] + side_shapes,
        scratch_shapes=[pltpu.VMEM((tt + POOL_HALO, d), F32)],
        compiler_params=_cparams("arbitrary", "arbitrary"),
        name="pool_prompt",
    )(x, x, pool_w, scale, ln_g, ln_b, *side)
    return out if side else out[0]


def _pool_sample_kernel(x_ref, prev_ref, w_ref, scale_ref, g_ref, b_ref, o_ref, carry_ref, buf_ref, *, alpha):
    nb, t, d = x_ref.shape
    n_prev = prev_ref.shape[1]
    assert n_prev == POOL_HALO - 1 and t % 8 == 0
    gw = w_ref.shape[-1]
    rows = POOL_HALO + t
    buf_ref[:, POOL_HALO - 8:POOL_HALO, :] = jnp.zeros((nb, 8, d), F32)
    buf_ref[:, 0:n_prev, :] = prev_ref[...]
    head = buf_ref[:, 0:POOL_HALO, :].reshape(nb * POOL_HALO, d)
    buf_ref[:, 0:POOL_HALO, :] = pltpu.roll(head, 1, 0).reshape(nb, POOL_HALO, d)
    buf_ref[:, POOL_HALO:, :] = x_ref[...]
    first = POOL_HALO + t - n_prev
    whole = buf_ref[...].reshape(nb * rows, d)
    carry_ref[...] = pltpu.roll(whole, nb * rows - first, 0).reshape(nb, rows, d)[:, 0:n_prev, :]
    x = x_ref[...].reshape(nb * t, d)
    pos = lax.broadcasted_iota(jnp.int32, (nb, t, 1), 1).reshape(nb * t, 1)
    diffs = []
    for g, w in enumerate(POOL_WINDOWS):
        sl = slice(g * gw, (g + 1) * gw)
        s = _window_sums(buf_ref[:, :, sl].reshape(nb * rows, gw), w)
        s = s.reshape(nb, rows, gw)[:, POOL_HALO:, :].reshape(nb * t, gw)
        cnt = jnp.minimum(pos + 1 + n_prev, w).astype(F32)
        diffs.append(s / cnt - x[:, sl])
    mixed = _pool_mix(diffs, w_ref, scale_ref[...])
    o_ref[...] = _layer_norm(alpha * x + mixed, g_ref[...], b_ref[...]).reshape(nb, t, d)


def _pool_sample(x, state_pool, layer, pool_w, scale, ln_g, ln_b, alpha):
    b, t, d = x.shape
    n_prev = state_pool.shape[2]
    nb = min(b, SEQS_POOL)
    vec = pl.BlockSpec((1, d), lambda i: (0, 0))
    return pl.pallas_call(
        functools.partial(_pool_sample_kernel, alpha=alpha),
        grid=(b // nb,),
        in_specs=[pl.BlockSpec((nb, t, d), lambda i: (i, 0, 0)),
                  pl.BlockSpec((None, nb, n_prev, d), lambda i: (layer, i, 0, 0)),
                  pl.BlockSpec(pool_w.shape, lambda i: (0, 0, 0)),
                  vec, vec, vec],
        out_specs=[pl.BlockSpec((nb, t, d), lambda i: (i, 0, 0)),
                   pl.BlockSpec((nb, n_prev, d), lambda i: (i, 0, 0))],
        out_shape=[jax.ShapeDtypeStruct(x.shape, F32), jax.ShapeDtypeStruct((b, n_prev, d), F32)],
        scratch_shapes=[pltpu.VMEM((nb, POOL_HALO + t, d), F32)],
        compiler_params=_cparams("parallel"),
        name="pool_sample",
    )(x, state_pool, pool_w, scale, ln_g, ln_b)


def _attn_kernel(*refs, alpha, n_side):
    o_ref = refs[7 + n_side]
    _side_cast(refs[7:7 + n_side], refs[8 + n_side:8 + 2 * n_side])
    _attn_body(*refs[:7], o_ref, *refs[-3:], alpha)


def _attn_body(x_ref, k_ref, v_ref, wq_ref, wo_ref, g_ref, b_ref, o_ref, oh_ref, s_ref, p_ref, alpha):
    tt, d = x_ref.shape
    hd = d // XA_HEADS
    scale = hd ** -0.5
    heads = [slice(h * hd, (h + 1) * hd) for h in range(XA_HEADS)]
    n_parts = 4 if tt % 64 == 0 else 1
    parts = [slice(r * (tt // n_parts), (r + 1) * (tt // n_parts)) for r in range(n_parts)]
    xs = [x_ref[rows, :] for rows in parts]
    for rows, x in zip(parts, xs):
        oh_ref[rows, :] = _dot(x.astype(BF16), wq_ref[...]).astype(BF16)
    for rows in parts:
        for h, sl in enumerate(heads):
            s_ref[h, rows, :] = lax.dot_general(oh_ref[rows, sl], k_ref[:, sl].astype(BF16), _NT,
                                                preferred_element_type=F32) * scale
    for rows in parts:
        for h in range(XA_HEADS):
            s = s_ref[h, rows, :]
            p = jnp.exp(s - jnp.max(s, axis=-1, keepdims=True))
            p_ref[h, rows, :] = (p / jnp.sum(p, axis=-1, keepdims=True)).astype(BF16)
    for rows in parts:
        for h, sl in enumerate(heads):
            oh_ref[rows, sl] = _dot(p_ref[h, rows, :], v_ref[:, sl].astype(BF16)).astype(BF16)
    for rows, x in zip(parts, xs):
        a = _dot(oh_ref[rows, :], wo_ref[...])
        o_ref[rows, :] = _layer_norm(alpha * x + a, g_ref[...], b_ref[...])


def _mem_attend(x, k4, v4, layer, wq, wo, ln_g, ln_b, alpha, tt, side=()):
    b, t, d = x.shape
    m = k4.shape[2]
    grid = (b, t // tt)
    vec = pl.BlockSpec((1, d), lambda i, j: (0, 0))
    mat = pl.BlockSpec((d, d), lambda i, j: (0, 0))
    kv = pl.BlockSpec((None, None, m, d), lambda i, j: (layer, i, 0, 0))
    side_specs, side_shapes = _side_cast_plan(side, grid)
    out = pl.pallas_call(
        functools.partial(_attn_kernel, alpha=alpha, n_side=len(side)),
        grid=grid,
        in_specs=[pl.BlockSpec((None, tt, d), lambda i, j: (i, j, 0)), kv, kv, mat, mat, vec, vec] + side_specs,
        out_specs=[pl.BlockSpec((None, tt, d), lambda i, j: (i, j, 0))] + side_specs,
        out_shape=[jax.ShapeDtypeStruct(x.shape, F32)] + side_shapes,
        scratch_shapes=[pltpu.VMEM((tt, d), BF16), pltpu.VMEM((XA_HEADS, tt, m), F32),
                        pltpu.VMEM((XA_HEADS, tt, m), BF16)],
        compiler_params=_cparams("arbitrary", "arbitrary"),
        name=f"mem_attend_l{layer}",
    )(x, k4, v4, wq, wo, ln_g, ln_b, *side)
    return out if side else out[0]


def _attn_cache_kernel(*refs, alpha):
    _attn_cache_body(*refs, alpha)


def _attn_cache_body(x_ref, k_ref, v_ref, wq_ref, wo_ref, g_ref, b_ref, o_ref, q_scr, oh_scr, s_scr, p_scr, alpha):
    nb, t, d = x_ref.shape
    heads = XA_HEADS
    hd = d // heads
    halves = hd // LANES
    assert halves == 2
    rows = k_ref.shape[1]
    qr = heads * t
    scale = hd ** -0.5
    x = x_ref[...].reshape(nb * t, d)
    q = _dot(x.astype(BF16), wq_ref[...])
    for c in range(halves):
        for h in range(heads):
            col = (h * halves + c) * LANES
            q_scr[c, :, h * t:(h + 1) * t, :] = q[:, col:col + LANES].reshape(nb, t, LANES)

    col_id = lax.broadcasted_iota(jnp.int32, (qr, rows), 1)
    row_head = lax.broadcasted_iota(jnp.int32, (qr, rows), 0) // t
    col_half = (col_id // heads) % halves
    col_head = col_id % heads
    keep = jnp.logical_and(col_half == 1, col_head == row_head)

    for b in range(nb):
        kb = k_ref[b].astype(BF16)
        g0 = lax.dot_general(q_scr[0, b].astype(BF16), kb, _NT, preferred_element_type=F32)
        g1 = lax.dot_general(q_scr[1, b].astype(BF16), kb, _NT, preferred_element_type=F32)
        s_scr[b] = jnp.where(col_half == 0, g0, g1)
    for b in range(nb):
        part = s_scr[b]
        s = (part + pltpu.roll(part, heads, 1)) * scale
        s = jnp.where(keep, s, -jnp.inf)
        p = jnp.exp(s - jnp.max(s, axis=-1, keepdims=True))
        p = p / jnp.sum(p, axis=-1, keepdims=True)
        p_scr[0, b] = pltpu.roll(p, rows - heads, 1).astype(BF16)
        p_scr[1, b] = p.astype(BF16)
    for b in range(nb):
        vb = v_ref[b].astype(BF16)
        for c in range(halves):
            oc = _dot(p_scr[c, b], vb)
            for h in range(heads):
                col = (h * halves + c) * LANES
                oh_scr[b * t:(b + 1) * t, col:col + LANES] = oc[h * t:(h + 1) * t, :]
    a = _dot(oh_scr[...].astype(BF16), wo_ref[...])
    o_ref[...] = _layer_norm(alpha * x + a, g_ref[...], b_ref[...]).reshape(nb, t, d)


def _mem_attend_cache(x, k8, v8, layer, wq, wo, ln_g, ln_b, alpha, nb):
    b, t, d = x.shape
    rows = k8.shape[2]
    vec = pl.BlockSpec((1, d), lambda i: (0, 0))
    mat = pl.BlockSpec((d, d), lambda i: (0, 0))
    kv = pl.BlockSpec((None, nb, rows, LANES), lambda i: (layer, i, 0, 0))
    return pl.pallas_call(
        functools.partial(_attn_cache_kernel, alpha=alpha),
        grid=(b // nb,),
        in_specs=[pl.BlockSpec((nb, t, d), lambda i: (i, 0, 0)), kv, kv, mat, mat, vec, vec],
        out_specs=pl.BlockSpec((nb, t, d), lambda i: (i, 0, 0)),
        out_shape=jax.ShapeDtypeStruct(x.shape, F32),
        scratch_shapes=[pltpu.VMEM((2, nb, XA_HEADS * t, LANES), F32), pltpu.VMEM((nb * t, d), F32),
                        pltpu.VMEM((nb, XA_HEADS * t, rows), F32), pltpu.VMEM((2, nb, XA_HEADS * t, rows), BF16)],
        compiler_params=_cparams("parallel"),
        name=f"mem_attend_cache_l{layer}",
    )(x, k8, v8, wq, wo, ln_g, ln_b)


FFN_SUB = 512


def _swiglu(x_ref, wg_ref, wu_ref, wd_ref, h_ref):
    xb = x_ref[...].astype(BF16)
    f = h_ref.shape[1]
    for s in range(f // FFN_SUB):
        sl = slice(s * FFN_SUB, (s + 1) * FFN_SUB)
        gate = _dot(xb, wg_ref[0, :, sl])
        up = _dot(xb, wu_ref[0, :, sl])
        h_ref[:, sl] = (gate * jax.nn.sigmoid(gate) * up).astype(BF16)
    return _dot(h_ref[...], wd_ref[0])


def _ffn_ln_kernel(*refs, alpha, n_side):
    x_ref, wg_ref, wu_ref, wd_ref, g_ref, b_ref = refs[:6]
    o_ref = refs[6 + n_side]
    h_ref = refs[-1]
    y = _swiglu(x_ref, wg_ref, wu_ref, wd_ref, h_ref)
    o_ref[...] = _layer_norm(alpha * x_ref[...] + y, g_ref[...], b_ref[...])
    _side_cast(refs[6:6 + n_side], refs[7 + n_side:7 + 2 * n_side])


def _ffn_ln(x2d, w_gu, w_d, ln_g, ln_b, alpha, tm, side=()):
    n, d = x2d.shape
    tm = min(tm, n)
    f = w_d.shape[1]
    grid = (n // tm,)
    vec = pl.BlockSpec((1, d), lambda i: (0, 0))
    once = pl.Buffered(1)
    side_specs, side_shapes = _side_cast_plan(side, grid)
    out = pl.pallas_call(
        functools.partial(_ffn_ln_kernel, alpha=alpha, n_side=len(side)),
        grid=grid,
        in_specs=[pl.BlockSpec((tm, d), lambda i: (i, 0)),
                  pl.BlockSpec((1, d, f), lambda i: (0, 0, 0), pipeline_mode=once),
                  pl.BlockSpec((1, d, f), lambda i: (0, 0, 1), pipeline_mode=once),
                  pl.BlockSpec((1, f, d), lambda i: (0, 0, 0), pipeline_mode=once),
                  vec, vec] + side_specs,
        out_specs=[pl.BlockSpec((tm, d), lambda i: (i, 0))] + side_specs,
        out_shape=[jax.ShapeDtypeStruct((n, d), F32)] + side_shapes,
        scratch_shapes=[pltpu.VMEM((tm, f), BF16)],
        compiler_params=_cparams("parallel"),
        name="ffn_dense",
    )(x2d, w_gu, w_gu, w_d, ln_g, ln_b, *side)
    return out if side else out[0]


def _expert_kernel(eid_ref, used_ref, x_ref, wg_ref, wu_ref, wd_ref, o_ref, h_ref):
    i = pl.program_id(0)

    @pl.when(used_ref[i] > 0)
    def _():
        o_ref[...] = _swiglu(x_ref, wg_ref, wu_ref, wd_ref, h_ref)

    @pl.when(used_ref[i] == 0)
    def _():
        o_ref[...] = jnp.zeros_like(o_ref)


def _expert_ffn(xs, tile_eid, tile_used, w_gu, w_d, tm):
    r, d = xs.shape
    f = w_d.shape[1]
    grid_spec = pltpu.PrefetchScalarGridSpec(
        num_scalar_prefetch=2,
        grid=(r // tm,),
        in_specs=[pl.BlockSpec((tm, d), lambda i, e, u: (i, 0)),
                  pl.BlockSpec((1, d, f), lambda i, e, u: (e[i], 0, 0)),
                  pl.BlockSpec((1, d, f), lambda i, e, u: (e[i], 0, 1)),
                  pl.BlockSpec((1, f, d), lambda i, e, u: (e[i], 0, 0))],
        out_specs=pl.BlockSpec((tm, d), lambda i, e, u: (i, 0)),
        scratch_shapes=[pltpu.VMEM((tm, f), BF16)],
    )
    return pl.pallas_call(
        _expert_kernel,
        grid_spec=grid_spec,
        out_shape=jax.ShapeDtypeStruct((r, d), F32),
        compiler_params=_cparams("arbitrary"),
        name="ffn_experts",
    )(tile_eid, tile_used, xs, w_gu, w_gu, w_d)


def _split3(x):
    hi = x.astype(BF16).astype(F32)
    r = x - hi
    mid = r.astype(BF16).astype(F32)
    lo = (r - mid).astype(BF16).astype(F32)
    return hi, mid, lo


def _hgrn_tile_kernel(*refs, alpha, layer, chunk, has_state, n_side):
    st_refs = refs[-HG_HEADS:]
    refs = refs[:-HG_HEADS]
    s0_ref = None
    if has_state:
        s0_ref, refs = refs[1], refs[:1] + refs[2:]
    x_ref, w_in_ref, lbp_ref, ng_ref, w_o_ref, g_ref, b_ref = refs[:7]
    o_ref, sfin_ref = refs[7 + n_side:9 + n_side]
    _side_cast(refs[7:7 + n_side], refs[9 + n_side:9 + 2 * n_side])
    (proj_ref, lf_ref, bcum_ref, qrel_ref, krel_ref, kend_ref, qdec_ref, vb_ref, dec_ref,
     u_ref, oacc_ref, og_ref) = refs[9 + 2 * n_side:]
    t = pl.program_id(1)
    nb, tt, d = x_ref.shape
    heads = HG_HEADS
    dv = ng_ref.shape[-1]
    vdim = heads * dv
    kdim = (w_in_ref.shape[1] - 2 * vdim) // 2
    dk = kdim // heads
    m = nb * tt
    c = chunk
    n_chunks = m // c
    chunks_per_seq = tt // c
    mid = max(c // 2 - 1, 0)

    @pl.when(t == 0)
    def _():
        for h in range(heads):
            if has_state:
                for bi in range(nb):
                    st_refs[h][bi] = s0_ref[bi, h]
            else:
                st_refs[h][...] = jnp.zeros_like(st_refs[h])

    x = x_ref[...].reshape(m, d)
    xb = x.astype(BF16)

    lbp = lbp_ref[...]
    e = jnp.exp(lbp - jnp.max(lbp, axis=0, keepdims=True))
    sm = e / jnp.sum(e, axis=0, keepdims=True)
    lb = jnp.sum(sm[1:layer + 1], axis=0, keepdims=True)

    f_cols = slice(kdim, 2 * kdim)
    proj_ref[:, f_cols] = _dot(xb, w_in_ref[:, f_cols])
    for j in range(n_chunks):
        rows = slice(j * c, (j + 1) * c)
        f = lb + (1.0 - lb) * jax.nn.sigmoid(proj_ref[rows, f_cols])
        proj_ref[rows, f_cols] = 1.0 - f
        for p, part in enumerate(_split3(jnp.log(f))):
            lf_ref[p, rows, :] = part.astype(lf_ref.dtype)
    proj_ref[:, 0:kdim] = _dot(xb, w_in_ref[:, 0:kdim])
    proj_ref[:, 2 * kdim:] = _dot(xb, w_in_ref[:, 2 * kdim:])

    ri = lax.broadcasted_iota(jnp.int32, (m, m), 0)
    ci = lax.broadcasted_iota(jnp.int32, (m, m), 1)
    causal = jnp.logical_and(ri // c == ci // c, ri >= ci)
    tri = causal.astype(BF16)
    bcum_ref[...] = (_dot(tri, lf_ref[0].astype(BF16)) + _dot(tri, lf_ref[1].astype(BF16))
                     + _dot(tri, lf_ref[2].astype(BF16)))

    sdt = qrel_ref.dtype
    vb_ref[...] = proj_ref[:, 2 * kdim:2 * kdim + vdim].astype(sdt)
    for j in range(n_chunks):
        rows = slice(j * c, (j + 1) * c)
        bcum = bcum_ref[rows, :]
        bmid = bcum[mid:mid + 1]
        blast = bcum[c - 1:c]
        q_dec = proj_ref[rows, 0:kdim] * jnp.exp(bcum)
        k_rel = proj_ref[rows, f_cols] * jnp.exp(bmid - bcum)
        qdec_ref[rows, :] = q_dec.astype(sdt)
        qrel_ref[rows, :] = (q_dec * jnp.exp(-bmid)).astype(sdt)
        krel_ref[rows, :] = k_rel.astype(sdt)
        kend_ref[rows, :] = (k_rel * jnp.exp(blast - bmid)).astype(sdt)
        dec_ref[j:j + 1, :] = jnp.exp(blast)
    if n_chunks < dec_ref.shape[0]:
        dec_ref[n_chunks:, :] = jnp.zeros((dec_ref.shape[0] - n_chunks, kdim), F32)
    dec_cols = dec_ref[...].T

    for h in range(heads):
        sk = slice(h * dk, (h + 1) * dk)
        sv = slice(h * dv, (h + 1) * dv)
        a = lax.dot_general(qrel_ref[:, sk].astype(BF16), krel_ref[:, sk].astype(BF16), _NT,
                            preferred_element_type=F32)
        a = jnp.where(causal, a, 0.0).astype(BF16)
        oacc_ref[:, sv] = _dot(a, vb_ref[:, sv].astype(BF16))

    for j in range(n_chunks):
        rows = slice(j * c, (j + 1) * c)
        for h in range(heads):
            sk = slice(h * dk, (h + 1) * dk)
            sv = slice(h * dv, (h + 1) * dv)
            u_ref[j, h] = lax.dot_general(kend_ref[rows, sk].astype(BF16), vb_ref[rows, sv].astype(BF16), _TN,
                                          preferred_element_type=F32)

    for j in range(n_chunks):
        rows = slice(j * c, (j + 1) * c)
        bi = j // chunks_per_seq
        for h in range(heads):
            sk = slice(h * dk, (h + 1) * dk)
            sv = slice(h * dv, (h + 1) * dv)
            st = st_refs[h][bi]
            oacc_ref[rows, sv] += _dot(qdec_ref[rows, sk].astype(BF16), st.astype(BF16))
            st_refs[h][bi] = st * dec_cols[sk, j:j + 1] + u_ref[j, h]

    ng = ng_ref[...]
    for h in range(heads):
        sv = slice(h * dv, (h + 1) * dv)
        o = oacc_ref[:, sv]
        o = o * lax.rsqrt(jnp.mean(o * o, axis=-1, keepdims=True) + RMS_EPS) * ng
        gate = proj_ref[:, 2 * kdim + vdim + h * dv:2 * kdim + vdim + (h + 1) * dv]
        og_ref[:, sv] = (o * (gate * jax.nn.sigmoid(gate))).astype(BF16)
    out = _dot(og_ref[...], w_o_ref[...])
    o_ref[...] = _layer_norm(alpha * x + out, g_ref[...], b_ref[...]).reshape(nb, tt, d)

    @pl.when(t == pl.num_programs(1) - 1)
    def _():
        for bi in range(nb):
            for h in range(heads):
                sfin_ref[bi, h] = st_refs[h][bi]


def _hgrn_tiled(x, s0, w_in, lb_param, norm_g, w_o, ln_g, ln_b, alpha, layer, chunk, nb, tt, side=()):
    b, t, d = x.shape
    heads = HG_HEADS
    dv = norm_g.shape[-1]
    vdim = heads * dv
    kdim = (w_in.shape[1] - 2 * vdim) // 2
    dk = kdim // heads
    m = nb * tt
    n_chunks = m // chunk
    has_state = s0 is not None
    const2 = lambda i, j: (0, 0)
    state_spec = pl.BlockSpec((nb, heads, dk, dv), lambda i, j: (i, 0, 0, 0))
    in_specs = [pl.BlockSpec((nb, tt, d), lambda i, j: (i, j, 0))]
    args = [x]
    if has_state:
        in_specs.append(state_spec)
        args.append(s0)
    in_specs += [pl.BlockSpec(w_in.shape, const2), pl.BlockSpec(lb_param.shape, const2),
                 pl.BlockSpec(norm_g.shape, const2), pl.BlockSpec(w_o.shape, const2),
                 pl.BlockSpec((1, d), const2), pl.BlockSpec((1, d), const2)]
    args += [w_in, lb_param, norm_g, w_o, ln_g, ln_b]
    sdt = BF16 if chunk % 16 == 0 else F32
    scratch = [pltpu.VMEM((m, w_in.shape[1]), F32),
               pltpu.VMEM((3, m, kdim), sdt),
               pltpu.VMEM((m, kdim), F32),
               pltpu.VMEM((m, kdim), sdt),
               pltpu.VMEM((m, kdim), sdt),
               pltpu.VMEM((m, kdim), sdt),
               pltpu.VMEM((m, kdim), sdt),
               pltpu.VMEM((m, vdim), sdt),
               pltpu.VMEM((max(n_chunks, 8), kdim), F32),
               pltpu.VMEM((n_chunks, heads, dk, dv), F32),
               pltpu.VMEM((m, vdim), F32),
               pltpu.VMEM((m, vdim), BF16)]
    scratch += [pltpu.VMEM((nb, dk, dv), F32) for _ in range(heads)]
    grid = (b // nb, t // tt)
    side_specs, side_shapes = _side_cast_plan(side, grid)
    out = pl.pallas_call(
        functools.partial(_hgrn_tile_kernel, alpha=alpha, layer=layer, chunk=chunk, has_state=has_state,
                          n_side=len(side)),
        grid=grid,
        in_specs=in_specs + side_specs,
        out_specs=[pl.BlockSpec((nb, tt, d), lambda i, j: (i, j, 0)), state_spec] + side_specs,
        out_shape=[jax.ShapeDtypeStruct(x.shape, F32),
                   jax.ShapeDtypeStruct((b, heads, dk, dv), F32)] + side_shapes,
        scratch_shapes=scratch,
        compiler_params=_cparams("arbitrary", "arbitrary"),
        name="hgrn_state" if has_state else "hgrn_fresh",
    )(*args, *side)
    return out[0], out[1], out[2:]


_I_E0, _I_E1, _I_G0, _I_G1, _I_R0, _I_R1 = range(6)
ROUTE_ROWS = 8


def _router_kernel(x_ref, w_ref, cnt0_ref, info_ref, route_ref, cnt_ref, carry_ref):
    i = pl.program_id(0)
    tm = x_ref.shape[0]

    @pl.when(i == 0)
    def _():
        carry_ref[...] = cnt0_ref[...]

    x = x_ref[...]
    w = w_ref[...]
    x_hi = x.astype(BF16)
    x_lo = (x - x_hi.astype(F32)).astype(BF16)
    w_hi = w.astype(BF16)
    w_lo = (w - w_hi.astype(F32)).astype(BF16)
    hi_both = _dot(x_hi, jnp.concatenate([w_hi, w_lo], axis=1))
    logits = hi_both[:, :LANES] + (hi_both[:, LANES:] + _dot(x_lo, w_hi))
    lane = lax.broadcasted_iota(jnp.int32, (tm, LANES), 1)
    valid = lane < N_EXPERTS
    logits = jnp.where(valid, logits, -jnp.inf)
    p = jnp.exp(logits - jnp.max(logits, axis=-1, keepdims=True))
    p = p / jnp.sum(p, axis=-1, keepdims=True)
    p0 = jnp.where(valid, p, -1.0)
    v0 = jnp.max(p0, axis=-1, keepdims=True)
    e0 = jnp.min(jnp.where(p0 == v0, lane, LANES), axis=-1, keepdims=True)
    p1 = jnp.where(lane == e0, -1.0, p0)
    v1 = jnp.max(p1, axis=-1, keepdims=True)
    e1 = jnp.min(jnp.where(p1 == v1, lane, LANES), axis=-1, keepdims=True)
    den = v0 + v1
    sel = jnp.logical_or(lane == e0, lane == e1)
    ri = lax.broadcasted_iota(jnp.int32, (tm, tm), 0)
    ci = lax.broadcasted_iota(jnp.int32, (tm, tm), 1)
    before = (ri > ci).astype(BF16)
    rank = _dot(before, sel.astype(BF16)) + carry_ref[...]
    r0 = jnp.sum(jnp.where(lane == e0, rank, 0.0), axis=-1, keepdims=True)
    r1 = jnp.sum(jnp.where(lane == e1, rank, 0.0), axis=-1, keepdims=True)
    carry_ref[...] += jnp.sum(sel.astype(F32), axis=0, keepdims=True)
    info = jnp.zeros((tm, LANES), F32)
    for idx, val in ((_I_E0, e0.astype(F32)), (_I_E1, e1.astype(F32)), (_I_G0, v0 / den),
                     (_I_G1, v1 / den), (_I_R0, r0), (_I_R1, r1)):
        info = jnp.where(lane == idx, val, info)
    info_ref[...] = info
    route_ref[...] = info.T[0:route_ref.shape[0], :]
    cnt_ref[...] = carry_ref[...]


def _router(x2d, w_router_padded, counts_in, tm):
    n, d = x2d.shape
    return pl.pallas_call(
        _router_kernel,
        grid=(n // tm,),
        in_specs=[pl.BlockSpec((tm, d), lambda i: (i, 0)),
                  pl.BlockSpec((d, LANES), lambda i: (0, 0)),
                  pl.BlockSpec((1, LANES), lambda i: (0, 0))],
        out_specs=[pl.BlockSpec((tm, LANES), lambda i: (i, 0)),
                   pl.BlockSpec((ROUTE_ROWS, tm), lambda i: (0, i)),
                   pl.BlockSpec((1, LANES), lambda i: (0, 0))],
        out_shape=[jax.ShapeDtypeStruct((n, LANES), F32), jax.ShapeDtypeStruct((ROUTE_ROWS, n), F32),
                   jax.ShapeDtypeStruct((1, LANES), F32)],
        scratch_shapes=[pltpu.VMEM((1, LANES), F32)],
        compiler_params=_cparams("arbitrary"),
        name="moe_router",
    )(x2d, w_router_padded, counts_in)


def _positions_kernel(offs_ref, route_ref, pos_ref):
    route = route_ref[...].astype(jnp.int32)
    expert = route[_I_E0:_I_E1 + 1]
    start = jnp.zeros_like(expert)
    for e in range(N_EXPERTS):
        start = jnp.where(expert == e, offs_ref[e], start)
    pos_ref[...] = start + route[_I_R0:_I_R1 + 1]


def _positions(route, offs, tn):
    n = route.shape[1]
    grid_spec = pltpu.PrefetchScalarGridSpec(
        num_scalar_prefetch=1,
        grid=(n // tn,),
        in_specs=[pl.BlockSpec((ROUTE_ROWS, tn), lambda i, o: (0, i))],
        out_specs=pl.BlockSpec((TOP_K, tn), lambda i, o: (0, i)),
    )
    return pl.pallas_call(
        _positions_kernel,
        grid_spec=grid_spec,
        out_shape=jax.ShapeDtypeStruct((TOP_K, n), jnp.int32),
        compiler_params=_cparams("parallel"),
        name="moe_positions",
    )(offs, route)


def _row_copy(src, src_row, dst, dst_row, sem):
    return pltpu.make_async_copy(src.at[pl.ds(src_row, 1)], dst.at[pl.ds(dst_row, 1)], sem)


def _scatter_rows(pos_refs, x_ref, out_hbm, sem):
    tb = x_ref.shape[0]

    def issue(j, carry):
        for pos_ref in pos_refs:
            _row_copy(x_ref, j, out_hbm, pos_ref[j], sem).start()
        return carry

    lax.fori_loop(0, tb, issue, 0, unroll=4)
    for _ in range(TOP_K):
        pltpu.make_async_copy(x_ref, out_hbm.at[pl.ds(0, tb)], sem).wait()


def _dispatch_kernel(pos0_ref, pos1_ref, fill_ref, xa_ref, xb_ref, out_hbm, zero_ref, sem, zsem, *, steps_a):
    i = pl.program_id(0)
    pos_refs = (pos0_ref, pos1_ref)

    @pl.when(i == 0)
    def _():
        zero_ref[...] = jnp.zeros_like(zero_ref)
        tz = zero_ref.shape[0]

        def fill(e):
            row = pl.multiple_of(jnp.maximum(fill_ref[e], 0), tz)
            return pltpu.make_async_copy(zero_ref, out_hbm.at[pl.ds(row, tz)], zsem)

        for e in range(fill_ref.shape[0]):
            pl.when(fill_ref[e] >= 0)(lambda e=e: fill(e).start())
        for e in range(fill_ref.shape[0]):
            pl.when(fill_ref[e] >= 0)(lambda e=e: fill(e).wait())

    pl.when(i < steps_a)(lambda: _scatter_rows(pos_refs, xa_ref, out_hbm, sem))
    pl.when(i >= steps_a)(lambda: _scatter_rows(pos_refs, xb_ref, out_hbm, sem))


def _dispatch(xa, xb, pos, fill_rows, n_rows, tz):
    d = xa.shape[1]
    tb = min(TILE_ROUTE, xa.shape[0], xb.shape[0])
    steps_a = xa.shape[0] // tb
    steps_b = xb.shape[0] // tb
    pos_spec = pl.BlockSpec((tb,), lambda i: (i,), memory_space=pltpu.SMEM)
    return pl.pallas_call(
        functools.partial(_dispatch_kernel, steps_a=steps_a),
        grid=(steps_a + steps_b,),
        in_specs=[pos_spec, pos_spec,
                  pl.BlockSpec(memory_space=pltpu.SMEM),
                  pl.BlockSpec((tb, d), lambda i: (jnp.minimum(i, steps_a - 1), 0)),
                  pl.BlockSpec((tb, d), lambda i: (jnp.maximum(i - steps_a, 0), 0))],
        out_specs=pl.BlockSpec(memory_space=pl.ANY),
        out_shape=jax.ShapeDtypeStruct((n_rows, d), xa.dtype),
        scratch_shapes=[pltpu.VMEM((tz, d), xa.dtype), pltpu.SemaphoreType.DMA(()),
                        pltpu.SemaphoreType.DMA(())],
        compiler_params=_cparams("arbitrary"),
        name="moe_dispatch",
    )(pos[0], pos[1], fill_rows, xa, xb)


def _combine_kernel(pos0_ref, pos1_ref, pos0n_ref, pos1n_ref, x_ref, info_ref, y_hbm, g_ref, b_ref, o_ref,
                    ybuf, sem, *, alpha):
    i = pl.program_id(0)
    n = pl.num_programs(0)
    tm = x_ref.shape[0]

    def issue(pos_refs, slot):
        def body(j, carry):
            for kk, pos_ref in enumerate(pos_refs):
                _row_copy(y_hbm, pos_ref[j], ybuf.at[slot, kk], j, sem.at[slot]).start()
            return carry
        lax.fori_loop(0, tm, body, 0, unroll=8)

    @pl.when(i == 0)
    def _():
        issue((pos0_ref, pos1_ref), 0)

    @pl.when(i + 1 < n)
    def _():
        issue((pos0n_ref, pos1n_ref), (i + 1) % 2)

    slot = i % 2
    for kk in range(TOP_K):
        pltpu.make_async_copy(y_hbm.at[pl.ds(0, tm)], ybuf.at[slot, kk], sem.at[slot]).wait()
    info = info_ref[...]
    moe = info[:, _I_G0:_I_G0 + 1] * ybuf[slot, 0] + info[:, _I_G1:_I_G1 + 1] * ybuf[slot, 1]
    o_ref[...] = _layer_norm(alpha * x_ref[...] + moe, g_ref[...], b_ref[...])


def _combine_ln(x2d, info, pos, y, ln_g, ln_b, alpha, tm):
    n, d = x2d.shape
    nt = n // tm
    vec = pl.BlockSpec((1, d), lambda i: (0, 0))
    pos_now = pl.BlockSpec((tm,), lambda i: (i,), memory_space=pltpu.SMEM)
    pos_next = pl.BlockSpec((tm,), lambda i: (jnp.minimum(i + 1, nt - 1),), memory_space=pltpu.SMEM)
    return pl.pallas_call(
        functools.partial(_combine_kernel, alpha=alpha),
        grid=(nt,),
        in_specs=[pos_now, pos_now, pos_next, pos_next,
                  pl.BlockSpec((tm, d), lambda i: (i, 0)),
                  pl.BlockSpec((tm, LANES), lambda i: (i, 0)),
                  pl.BlockSpec(memory_space=pl.ANY),
                  vec, vec],
        out_specs=pl.BlockSpec((tm, d), lambda i: (i, 0)),
        out_shape=jax.ShapeDtypeStruct((n, d), F32),
        scratch_shapes=[pltpu.VMEM((2, TOP_K, tm, d), F32), pltpu.SemaphoreType.DMA((2,))],
        compiler_params=_cparams("arbitrary"),
        name="moe_combine",
    )(pos[0], pos[1], pos[0], pos[1], x2d, info, y, ln_g, ln_b)


def _moe_ln(xs2d, w_router, w_gu, w_d, ln_g, ln_b, alpha, tm_expert):
    d = xs2d[0].shape[1]
    n_total = sum(x.shape[0] for x in xs2d)
    w_r = jnp.pad(w_router, ((0, 0), (0, LANES - N_EXPERTS)))
    counts = jnp.zeros((1, LANES), F32)
    infos, routes = [], []
    for x in xs2d:
        info, route, counts = _router(x, w_r, counts, min(x.shape[0], TILE_ROUTE))
        infos.append(info)
        routes.append(route)
    cnt = counts[0, :N_EXPERTS].astype(jnp.int32)
    padded = (cnt + tm_expert - 1) // tm_expert * tm_expert
    ends = jnp.cumsum(padded)
    offs = ends - padded
    n_rows = -(-(TOP_K * n_total) // tm_expert) * tm_expert + N_EXPERTS * tm_expert
    tile_start = jnp.arange(n_rows // tm_expert, dtype=jnp.int32) * tm_expert
    tile_eid = jnp.minimum(jnp.sum(tile_start[None, :] >= ends[:, None], axis=0), N_EXPERTS - 1).astype(jnp.int32)
    tile_used = (tile_start < offs[tile_eid] + cnt[tile_eid]).astype(jnp.int32)
    tail = ends[-1] + jnp.arange(N_EXPERTS, dtype=jnp.int32) * tm_expert
    fill_rows = jnp.concatenate([jnp.where(padded > 0, ends - tm_expert, -1),
                                 jnp.where(tail < n_rows, tail, -1)]).astype(jnp.int32)
    poss = [_positions(route, offs.astype(jnp.int32), route.shape[1]) for route in routes]
    xa, xb = xs2d
    buf = _dispatch(xa, xb, jnp.concatenate(poss, axis=1), fill_rows, n_rows, tm_expert)
    y = _expert_ffn(buf, tile_eid, tile_used, w_gu, w_d, tm_expert)
    return [_combine_ln(x, info, pos, y, ln_g, ln_b, alpha, min(x.shape[0], TILE_ROUTE))
            for x, info, pos in zip(xs2d, infos, poss)]


def kernel(x_prompt, x_sample, mem_prompt, state_pool, state_hgrn, cache_mem_k, cache_mem_v, pool_w, pool_scale, hg_w_in, hg_lb, hg_norm_g, hg_w_o, xa_w_q, xa_w_kv, xa_w_o, ffn_w_gu, ffn_w_d, moe_w_router, moe_w_gu, moe_w_d, ln_g, ln_b):
    b, t, d = x_prompt.shape
    sb, st, _ = x_sample.shape
    depth = ln_g.shape[0]
    mem_len = mem_prompt.shape[1]
    alpha = (2 * depth) ** 0.25
    n_prev = state_pool.shape[2]

    pool_w_b = pool_w.astype(BF16)
    xa_w_kv_b = xa_w_kv.astype(BF16)

    def vec(a):
        return a.reshape(1, -1)

    mk, mv = _mem_kv(mem_prompt.reshape(b * mem_len, d), xa_w_kv_b)
    mk4 = mk.reshape(depth, b, mem_len, d)
    mv4 = mv.reshape(depth, b, mem_len, d)

    def stored_order(c):
        halves = c.shape[4] // LANES
        c = c.reshape(depth, sb, mem_len, XA_HEADS, halves, LANES).transpose(0, 1, 2, 4, 3, 5)
        return c.reshape(depth, sb, mem_len * halves * XA_HEADS, LANES)

    ck8 = stored_order(cache_mem_k)
    cv8 = stored_order(cache_mem_v)

    tt_attn = min(t, TILE_ATTN)
    nb_attn = min(sb, SEQS_ATTN)

    def attend(xp, xs, layer, side=()):
        args = (xa_w_q_b[layer], xa_w_o_b[layer], vec(ln_g[layer, 1]), vec(ln_b[layer, 1]), alpha)
        outp = _mem_attend(xp, mk4, mv4, layer, *args, tt_attn, side)
        outs = _mem_attend_cache(xs, ck8, cv8, layer, *args, nb_attn)
        return (outp[0], outs, outp[1:]) if side else (outp, outs, [])

    def cast_beside(weights, grid):
        views = tuple(w.reshape(-1, w.shape[-1]) for w in weights)
        if _side_cast_plan(views, grid) is None:
            return (), tuple(w.astype(BF16) for w in weights)
        return views, None

    pool_args = (pool_w_b[0], vec(pool_scale[0]), vec(ln_g[0, 0]), vec(ln_b[0, 0]), alpha)
    early_w = (xa_w_q, xa_w_o, hg_w_in, hg_w_o)
    side, early_w_b = cast_beside(early_w, _pool_prompt_grid(x_prompt))
    xp = _pool_prompt(x_prompt, *pool_args, side)
    if side:
        xp, early_w_b = xp[0], tuple(c.reshape(w.shape) for c, w in zip(xp[1:], early_w))
    xa_w_q_b, xa_w_o_b, hg_w_in_b, hg_w_o_b = early_w_b
    xs, pool_s = _pool_sample(x_sample, state_pool, 0, *pool_args)
    ffn_w = (ffn_w_gu, ffn_w_d)
    side, ffn_w_b = cast_beside(ffn_w, (b, t // tt_attn))
    xp, xs, casts = attend(xp, xs, 0, side)
    if side:
        ffn_w_b = tuple(c.reshape(w.shape) for c, w in zip(casts, ffn_w))
    ffn_args = (*ffn_w_b, vec(ln_g[0, 2]), vec(ln_b[0, 2]), alpha)
    xp = _ffn_ln(xp.reshape(b * t, d), *ffn_args, TILE_FFN).reshape(b, t, d)
    xs = _ffn_ln(xs.reshape(sb * st, d), *ffn_args, TILE_FFN).reshape(sb, st, d)

    hg_args = (hg_w_in_b[0], hg_lb, vec(hg_norm_g[0]), hg_w_o_b[0], vec(ln_g[1, 0]), vec(ln_b[1, 0]), alpha, 1)
    moe_w = (moe_w_gu, moe_w_d)
    tt_hg = min(t, TILE_HGRN)
    side, moe_w_b = cast_beside(moe_w, (b, t // tt_hg))
    xp, hg_p, casts = _hgrn_tiled(xp, None, *hg_args, 2 * HG_CHUNK, 1, tt_hg, side)
    if side:
        moe_w_b = tuple(c.reshape(w.shape) for c, w in zip(casts, moe_w))
    xs, hg_s, _ = _hgrn_tiled(xs, state_hgrn[0], *hg_args, math.gcd(st, HG_CHUNK), min(sb, SEQS_HGRN), st)
    xp, xs, _ = attend(xp, xs, 1)
    yp, ys = _moe_ln([xp.reshape(b * t, d), xs.reshape(sb * st, d)], moe_w_router[0], moe_w_b[0][0],
                     moe_w_b[1][0], vec(ln_g[1, 2]), vec(ln_b[1, 2]), alpha, TILE_EXPERT)

    heads_hd = (depth, b, mem_len, XA_HEADS, d // XA_HEADS)
    pool_p = x_prompt[:, t - n_prev:, :][None]
    return (yp.reshape(b, t, d), ys.reshape(sb, st, d), pool_p, hg_p[None],
            mk.reshape(heads_hd), mv.reshape(heads_hd), pool_s[None], hg_s[None])
```

```python
import functools
import math

import jax
import jax.numpy as jnp
from jax import lax
from jax.experimental import pallas as pl
from jax.experimental.pallas import tpu as pltpu

F32 = jnp.float32
BF16 = jnp.bfloat16

POOL_WINDOWS = (2, 4, 8, 16)
POOL_HALO = 16
HG_HEADS = 8
HG_CHUNK = 32
RMS_EPS = 1e-6
XA_HEADS = 4
N_EXPERTS = 8
TOP_K = 2
LN_EPS = 1e-5
LANES = 128
VMEM_LIMIT = 62 * 1024 * 1024

TILE_KV = 512
TILE_POOL = 1024
SEQS_POOL = 16
TILE_ATTN = 1024
SEQS_ATTN = 8
TILE_FFN = 1024
TILE_FFN_CASTING = 256
TILE_EXPERT = 512
TILE_HGRN = 256
SEQS_HGRN = 8
TILE_ROUTE = 1024

_NT = (((1,), (1,)), ((), ()))
_TN = (((0,), (0,)), ((), ()))


def _cparams(*sem):
    return pltpu.CompilerParams(dimension_semantics=sem, vmem_limit_bytes=VMEM_LIMIT)


def _layer_norm(z, g, b):
    mu = jnp.mean(z, axis=-1, keepdims=True)
    zc = z - mu
    var = jnp.mean(zc * zc, axis=-1, keepdims=True)
    return zc * lax.rsqrt(var + LN_EPS) * g + b


def _dot(a, b):
    return jnp.dot(a, b, preferred_element_type=F32)


def _side_cast_plan(arrays, grid):
    steps = math.prod(grid)

    def block_index(*idx):
        lin = 0
        for g, i in zip(grid, idx):
            lin = lin * g + i
        return lin, 0

    specs, shapes = [], []
    for a in arrays:
        rows, cols = a.shape
        if rows % steps or (rows // steps) % 16:
            return None
        specs.append(pl.BlockSpec((rows // steps, cols), block_index))
        shapes.append(jax.ShapeDtypeStruct(a.shape, BF16))
    return specs, shapes


def _side_cast(in_refs, out_refs):
    for src, dst in zip(in_refs, out_refs):
        dst[...] = src[...].astype(dst.dtype)


def _kv_kernel(mem_ref, w_ref, k_ref, v_ref):
    kv = _dot(mem_ref[...].astype(BF16), w_ref[0])
    n = k_ref.shape[-1]
    k_ref[0] = kv[:, :n]
    v_ref[0] = kv[:, n:]


def _mem_kv(mem2d, w_kv):
    rows, d = mem2d.shape
    depth, _, e2 = w_kv.shape
    e = e2 // 2
    tm = min(rows, TILE_KV)
    out = jax.ShapeDtypeStruct((depth, rows, e), F32)
    return pl.pallas_call(
        _kv_kernel,
        grid=(depth, rows // tm),
        in_specs=[pl.BlockSpec((tm, d), lambda l, i: (i, 0)),
                  pl.BlockSpec((1, d, e2), lambda l, i: (l, 0, 0))],
        out_specs=[pl.BlockSpec((1, tm, e), lambda l, i: (l, i, 0)),
                   pl.BlockSpec((1, tm, e), lambda l, i: (l, i, 0))],
        out_shape=[out, out],
        compiler_params=_cparams("parallel", "parallel"),
        name="mem_kv",
    )(mem2d, w_kv)


def _window_sums(a, w):
    s, k = a, 1
    while k < w:
        s = s + pltpu.roll(s, k, 0)
        k *= 2
    return s


def _pool_mix(pooled_minus_tok, w_ref, scale):
    parts = [_dot(pooled_minus_tok[g].astype(BF16), w_ref[g]) for g in range(len(POOL_WINDOWS))]
    return jnp.concatenate(parts, axis=1) * scale


def _pool_prompt_kernel(*refs, alpha, n_side):
    x_ref, halo_ref, w_ref, scale_ref, g_ref, b_ref = refs[:6]
    o_ref = refs[6 + n_side]
    buf_ref = refs[-1]
    _side_cast(refs[6:6 + n_side], refs[7 + n_side:7 + 2 * n_side])
    t = pl.program_id(1)
    tt = x_ref.shape[1]
    gw = w_ref.shape[-1]
    x = x_ref[0]
    buf_ref[0:POOL_HALO, :] = jnp.where(t > 0, halo_ref[0], 0.0)
    buf_ref[POOL_HALO:, :] = x
    pos = t * tt + lax.broadcasted_iota(jnp.int32, (tt, 1), 0)
    diffs = []
    for g, w in enumerate(POOL_WINDOWS):
        sl = slice(g * gw, (g + 1) * gw)
        s = _window_sums(buf_ref[:, sl], w)[POOL_HALO:, :]
        cnt = jnp.minimum(pos + 1, w).astype(F32)
        diffs.append(s / cnt - x[:, sl])
    mixed = _pool_mix(diffs, w_ref, scale_ref[...])
    o_ref[0] = _layer_norm(alpha * x + mixed, g_ref[...], b_ref[...])


def _pool_prompt_grid(x):
    b, t, _ = x.shape
    return b, t // min(t, TILE_POOL)


def _pool_prompt(x, pool_w, scale, ln_g, ln_b, alpha, side=()):
    b, t, d = x.shape
    grid = _pool_prompt_grid(x)
    tt = t // grid[1]
    hb = tt // POOL_HALO
    vec = pl.BlockSpec((1, d), lambda i, j: (0, 0))
    side_specs, side_shapes = _side_cast_plan(side, grid)
    out = pl.pallas_call(
        functools.partial(_pool_prompt_kernel, alpha=alpha, n_side=len(side)),
        grid=grid,
        in_specs=[pl.BlockSpec((1, tt, d), lambda i, j: (i, j, 0)),
                  pl.BlockSpec((1, POOL_HALO, d), lambda i, j: (i, jnp.maximum(j * hb - 1, 0), 0)),
                  pl.BlockSpec(pool_w.shape, lambda i, j: (0, 0, 0)),
                  vec, vec, vec] + side_specs,
        out_specs=[pl.BlockSpec((1, tt, d), lambda i, j: (i, j, 0))] + side_specs,
        out_shape=[jax.ShapeDtypeStruct(x.shape, F32)] + side_shapes,
        scratch_shapes=[pltpu.VMEM((tt + POOL_HALO, d), F32)],
        compiler_params=_cparams("arbitrary", "arbitrary"),
        name="pool_prompt",
    )(x, x, pool_w, scale, ln_g, ln_b, *side)
    return out if side else out[0]


def _pool_sample_kernel(x_ref, prev_ref, w_ref, scale_ref, g_ref, b_ref, o_ref, carry_ref, buf_ref, *, alpha):
    nb, t, d = x_ref.shape
    n_prev = prev_ref.shape[1]
    assert n_prev == POOL_HALO - 1 and t % 8 == 0
    gw = w_ref.shape[-1]
    rows = POOL_HALO + t
    buf_ref[:, POOL_HALO - 8:POOL_HALO, :] = jnp.zeros((nb, 8, d), F32)
    buf_ref[:, 0:n_prev, :] = prev_ref[...]
    head = buf_ref[:, 0:POOL_HALO, :].reshape(nb * POOL_HALO, d)
    buf_ref[:, 0:POOL_HALO, :] = pltpu.roll(head, 1, 0).reshape(nb, POOL_HALO, d)
    buf_ref[:, POOL_HALO:, :] = x_ref[...]
    first = POOL_HALO + t - n_prev
    whole = buf_ref[...].reshape(nb * rows, d)
    carry_ref[...] = pltpu.roll(whole, nb * rows - first, 0).reshape(nb, rows, d)[:, 0:n_prev, :]
    x = x_ref[...].reshape(nb * t, d)
    pos = lax.broadcasted_iota(jnp.int32, (nb, t, 1), 1).reshape(nb * t, 1)
    diffs = []
    for g, w in enumerate(POOL_WINDOWS):
        sl = slice(g * gw, (g + 1) * gw)
        s = _window_sums(buf_ref[:, :, sl].reshape(nb * rows, gw), w)
        s = s.reshape(nb, rows, gw)[:, POOL_HALO:, :].reshape(nb * t, gw)
        cnt = jnp.minimum(pos + 1 + n_prev, w).astype(F32)
        diffs.append(s / cnt - x[:, sl])
    mixed = _pool_mix(diffs, w_ref, scale_ref[...])
    o_ref[...] = _layer_norm(alpha * x + mixed, g_ref[...], b_ref[...]).reshape(nb, t, d)


def _pool_sample(x, state_pool, layer, pool_w, scale, ln_g, ln_b, alpha):
    b, t, d = x.shape
    n_prev = state_pool.shape[2]
    nb = min(b, SEQS_POOL)
    vec = pl.BlockSpec((1, d), lambda i: (0, 0))
    return pl.pallas_call(
        functools.partial(_pool_sample_kernel, alpha=alpha),
        grid=(b // nb,),
        in_specs=[pl.BlockSpec((nb, t, d), lambda i: (i, 0, 0)),
                  pl.BlockSpec((None, nb, n_prev, d), lambda i: (layer, i, 0, 0)),
                  pl.BlockSpec(pool_w.shape, lambda i: (0, 0, 0)),
                  vec, vec, vec],
        out_specs=[pl.BlockSpec((nb, t, d), lambda i: (i, 0, 0)),
                   pl.BlockSpec((nb, n_prev, d), lambda i: (i, 0, 0))],
        out_shape=[jax.ShapeDtypeStruct(x.shape, F32), jax.ShapeDtypeStruct((b, n_prev, d), F32)],
        scratch_shapes=[pltpu.VMEM((nb, POOL_HALO + t, d), F32)],
        compiler_params=_cparams("parallel"),
        name="pool_sample",
    )(x, state_pool, pool_w, scale, ln_g, ln_b)


def _attn_kernel(*refs, alpha, n_side):
    o_ref = refs[7 + n_side]
    _side_cast(refs[7:7 + n_side], refs[8 + n_side:8 + 2 * n_side])
    _attn_body(*refs[:7], o_ref, *refs[-3:], alpha)


def _attn_body(x_ref, k_ref, v_ref, wq_ref, wo_ref, g_ref, b_ref, o_ref, oh_ref, s_ref, p_ref, alpha):
    tt, d = x_ref.shape
    hd = d // XA_HEADS
    scale = hd ** -0.5
    heads = [slice(h * hd, (h + 1) * hd) for h in range(XA_HEADS)]
    n_parts = 4 if tt % 64 == 0 else 1
    parts = [slice(r * (tt // n_parts), (r + 1) * (tt // n_parts)) for r in range(n_parts)]
    xs = [x_ref[rows, :] for rows in parts]
    for rows, x in zip(parts, xs):
        oh_ref[rows, :] = _dot(x.astype(BF16), wq_ref[...]).astype(BF16)
    for rows in parts:
        for h, sl in enumerate(heads):
            s_ref[h, rows, :] = lax.dot_general(oh_ref[rows, sl], k_ref[:, sl].astype(BF16), _NT,
                                                preferred_element_type=F32) * scale
    for rows in parts:
        for h in range(XA_HEADS):
            s = s_ref[h, rows, :]
            p = jnp.exp(s - jnp.max(s, axis=-1, keepdims=True))
            p_ref[h, rows, :] = (p / jnp.sum(p, axis=-1, keepdims=True)).astype(BF16)
    for rows in parts:
        for h, sl in enumerate(heads):
            oh_ref[rows, sl] = _dot(p_ref[h, rows, :], v_ref[:, sl].astype(BF16)).astype(BF16)
    for rows, x in zip(parts, xs):
        a = _dot(oh_ref[rows, :], wo_ref[...])
        o_ref[rows, :] = _layer_norm(alpha * x + a, g_ref[...], b_ref[...])


def _mem_attend(x, k4, v4, layer, wq, wo, ln_g, ln_b, alpha, tt, side=()):
    b, t, d = x.shape
    m = k4.shape[2]
    grid = (b, t // tt)
    vec = pl.BlockSpec((1, d), lambda i, j: (0, 0))
    mat = pl.BlockSpec((d, d), lambda i, j: (0, 0))
    kv = pl.BlockSpec((None, None, m, d), lambda i, j: (layer, i, 0, 0))
    side_specs, side_shapes = _side_cast_plan(side, grid)
    out = pl.pallas_call(
        functools.partial(_attn_kernel, alpha=alpha, n_side=len(side)),
        grid=grid,
        in_specs=[pl.BlockSpec((None, tt, d), lambda i, j: (i, j, 0)), kv, kv, mat, mat, vec, vec] + side_specs,
        out_specs=[pl.BlockSpec((None, tt, d), lambda i, j: (i, j, 0))] + side_specs,
        out_shape=[jax.ShapeDtypeStruct(x.shape, F32)] + side_shapes,
        scratch_shapes=[pltpu.VMEM((tt, d), BF16), pltpu.VMEM((XA_HEADS, tt, m), F32),
                        pltpu.VMEM((XA_HEADS, tt, m), BF16)],
        compiler_params=_cparams("arbitrary", "arbitrary"),
        name=f"mem_attend_l{layer}",
    )(x, k4, v4, wq, wo, ln_g, ln_b, *side)
    return out if side else out[0]


def _attn_cache_kernel(*refs, alpha):
    _attn_cache_body(*refs, alpha)


def _attn_cache_body(x_ref, k_ref, v_ref, wq_ref, wo_ref, g_ref, b_ref, o_ref, q_scr, oh_scr, s_scr, p_scr, alpha):
    nb, t, d = x_ref.shape
    heads = XA_HEADS
    hd = d // heads
    halves = hd // LANES
    assert halves == 2
    rows = k_ref.shape[1]
    qr = heads * t
    scale = hd ** -0.5
    x = x_ref[...].reshape(nb * t, d)
    q = _dot(x.astype(BF16), wq_ref[...])
    for c in range(halves):
        for h in range(heads):
            col = (h * halves + c) * LANES
            q_scr[:, c * qr + h * t:c * qr + (h + 1) * t, :] = q[:, col:col + LANES].reshape(nb, t, LANES)

    col_id = lax.broadcasted_iota(jnp.int32, (qr, rows), 1)
    row_head = lax.broadcasted_iota(jnp.int32, (qr, rows), 0) // t
    col_half = (col_id // heads) % halves
    col_head = col_id % heads
    keep = jnp.logical_and(col_half == 1, col_head == row_head)

    for b in range(nb):
        g = lax.dot_general(q_scr[b].astype(BF16), k_ref[b].astype(BF16), _NT, preferred_element_type=F32)
        s_scr[b] = jnp.where(col_half == 0, g[:qr], g[qr:])
    for b in range(nb):
        part = s_scr[b]
        s = (part + pltpu.roll(part, heads, 1)) * scale
        s = jnp.where(keep, s, -jnp.inf)
        p = jnp.exp(s - jnp.max(s, axis=-1, keepdims=True))
        p = p / jnp.sum(p, axis=-1, keepdims=True)
        p_scr[b, 0:qr, :] = pltpu.roll(p, rows - heads, 1).astype(BF16)
        p_scr[b, qr:, :] = p.astype(BF16)
    for b in range(nb):
        o2 = _dot(p_scr[b], v_ref[b].astype(BF16))
        for c in range(halves):
            for h in range(heads):
                col = (h * halves + c) * LANES
                oh_scr[b * t:(b + 1) * t, col:col + LANES] = o2[c * qr + h * t:c * qr + (h + 1) * t, :]
    a = _dot(oh_scr[...].astype(BF16), wo_ref[...])
    o_ref[...] = _layer_norm(alpha * x + a, g_ref[...], b_ref[...]).reshape(nb, t, d)


def _mem_attend_cache(x, k8, v8, layer, wq, wo, ln_g, ln_b, alpha, nb):
    b, t, d = x.shape
    rows = k8.shape[2]
    vec = pl.BlockSpec((1, d), lambda i: (0, 0))
    mat = pl.BlockSpec((d, d), lambda i: (0, 0))
    kv = pl.BlockSpec((None, nb, rows, LANES), lambda i: (layer, i, 0, 0))
    return pl.pallas_call(
        functools.partial(_attn_cache_kernel, alpha=alpha),
        grid=(b // nb,),
        in_specs=[pl.BlockSpec((nb, t, d), lambda i: (i, 0, 0)), kv, kv, mat, mat, vec, vec],
        out_specs=pl.BlockSpec((nb, t, d), lambda i: (i, 0, 0)),
        out_shape=jax.ShapeDtypeStruct(x.shape, F32),
        scratch_shapes=[pltpu.VMEM((nb, 2 * XA_HEADS * t, LANES), F32), pltpu.VMEM((nb * t, d), F32),
                        pltpu.VMEM((nb, XA_HEADS * t, rows), F32), pltpu.VMEM((nb, 2 * XA_HEADS * t, rows), BF16)],
        compiler_params=_cparams("parallel"),
        name=f"mem_attend_cache_l{layer}",
    )(x, k8, v8, wq, wo, ln_g, ln_b)


FFN_SUB = 512


def _swiglu(x_ref, wg_ref, wu_ref, wd_ref, h_ref):
    xb = x_ref[...].astype(BF16)
    f = h_ref.shape[1]
    for s in range(f // FFN_SUB):
        sl = slice(s * FFN_SUB, (s + 1) * FFN_SUB)
        gate = _dot(xb, wg_ref[0, :, sl])
        up = _dot(xb, wu_ref[0, :, sl])
        h_ref[:, sl] = (gate * jax.nn.sigmoid(gate) * up).astype(BF16)
    return _dot(h_ref[...], wd_ref[0])


def _ffn_ln_kernel(*refs, alpha, n_side):
    x_ref, wg_ref, wu_ref, wd_ref, g_ref, b_ref = refs[:6]
    o_ref = refs[6 + n_side]
    h_ref = refs[-1]
    y = _swiglu(x_ref, wg_ref, wu_ref, wd_ref, h_ref)
    o_ref[...] = _layer_norm(alpha * x_ref[...] + y, g_ref[...], b_ref[...])
    _side_cast(refs[6:6 + n_side], refs[7 + n_side:7 + 2 * n_side])


def _ffn_ln(x2d, w_gu, w_d, ln_g, ln_b, alpha, tm, side=()):
    n, d = x2d.shape
    tm = min(tm, n)
    f = w_d.shape[1]
    grid = (n // tm,)
    vec = pl.BlockSpec((1, d), lambda i: (0, 0))
    once = pl.Buffered(1)
    side_specs, side_shapes = _side_cast_plan(side, grid)
    out = pl.pallas_call(
        functools.partial(_ffn_ln_kernel, alpha=alpha, n_side=len(side)),
        grid=grid,
        in_specs=[pl.BlockSpec((tm, d), lambda i: (i, 0)),
                  pl.BlockSpec((1, d, f), lambda i: (0, 0, 0), pipeline_mode=once),
                  pl.BlockSpec((1, d, f), lambda i: (0, 0, 1), pipeline_mode=once),
                  pl.BlockSpec((1, f, d), lambda i: (0, 0, 0), pipeline_mode=once),
                  vec, vec] + side_specs,
        out_specs=[pl.BlockSpec((tm, d), lambda i: (i, 0))] + side_specs,
        out_shape=[jax.ShapeDtypeStruct((n, d), F32)] + side_shapes,
        scratch_shapes=[pltpu.VMEM((tm, f), BF16)],
        compiler_params=_cparams("parallel"),
        name="ffn_dense",
    )(x2d, w_gu, w_gu, w_d, ln_g, ln_b, *side)
    return out if side else out[0]


def _expert_kernel(eid_ref, used_ref, x_ref, wg_ref, wu_ref, wd_ref, o_ref, h_ref):
    i = pl.program_id(0)

    @pl.when(used_ref[i] > 0)
    def _():
        o_ref[...] = _swiglu(x_ref, wg_ref, wu_ref, wd_ref, h_ref)

    @pl.when(used_ref[i] == 0)
    def _():
        o_ref[...] = jnp.zeros_like(o_ref)


def _expert_ffn(xs, tile_eid, tile_used, w_gu, w_d, tm):
    r, d = xs.shape
    f = w_d.shape[1]
    grid_spec = pltpu.PrefetchScalarGridSpec(
        num_scalar_prefetch=2,
        grid=(r // tm,),
        in_specs=[pl.BlockSpec((tm, d), lambda i, e, u: (i, 0)),
                  pl.BlockSpec((1, d, f), lambda i, e, u: (e[i], 0, 0)),
                  pl.BlockSpec((1, d, f), lambda i, e, u: (e[i], 0, 1)),
                  pl.BlockSpec((1, f, d), lambda i, e, u: (e[i], 0, 0))],
        out_specs=pl.BlockSpec((tm, d), lambda i, e, u: (i, 0)),
        scratch_shapes=[pltpu.VMEM((tm, f), BF16)],
    )
    return pl.pallas_call(
        _expert_kernel,
        grid_spec=grid_spec,
        out_shape=jax.ShapeDtypeStruct((r, d), F32),
        compiler_params=_cparams("arbitrary"),
        name="ffn_experts",
    )(tile_eid, tile_used, xs, w_gu, w_gu, w_d)


def _split3(x):
    hi = x.astype(BF16).astype(F32)
    r = x - hi
    mid = r.astype(BF16).astype(F32)
    lo = (r - mid).astype(BF16).astype(F32)
    return hi, mid, lo


def _hgrn_tile_kernel(*refs, alpha, layer, chunk, has_state):
    st_refs = refs[-HG_HEADS:]
    refs = refs[:-HG_HEADS]
    s0_ref = None
    if has_state:
        s0_ref, refs = refs[1], refs[:1] + refs[2:]
    (x_ref, w_in_ref, lbp_ref, ng_ref, w_o_ref, g_ref, b_ref, o_ref, sfin_ref,
     proj_ref, lf_ref, bcum_ref, qrel_ref, krel_ref, kend_ref, qdec_ref, vb_ref, dec_ref,
     u_ref, oacc_ref, og_ref) = refs
    t = pl.program_id(1)
    nb, tt, d = x_ref.shape
    heads = HG_HEADS
    dv = ng_ref.shape[-1]
    vdim = heads * dv
    kdim = (w_in_ref.shape[1] - 2 * vdim) // 2
    dk = kdim // heads
    m = nb * tt
    c = chunk
    n_chunks = m // c
    chunks_per_seq = tt // c
    mid = max(c // 2 - 1, 0)

    @pl.when(t == 0)
    def _():
        for h in range(heads):
            if has_state:
                for bi in range(nb):
                    st_refs[h][bi] = s0_ref[bi, h]
            else:
                st_refs[h][...] = jnp.zeros_like(st_refs[h])

    x = x_ref[...].reshape(m, d)
    xb = x.astype(BF16)

    lbp = lbp_ref[...]
    e = jnp.exp(lbp - jnp.max(lbp, axis=0, keepdims=True))
    sm = e / jnp.sum(e, axis=0, keepdims=True)
    lb = jnp.sum(sm[1:layer + 1], axis=0, keepdims=True)

    f_cols = slice(kdim, 2 * kdim)
    proj_ref[:, f_cols] = _dot(xb, w_in_ref[:, f_cols])
    for j in range(n_chunks):
        rows = slice(j * c, (j + 1) * c)
        f = lb + (1.0 - lb) * jax.nn.sigmoid(proj_ref[rows, f_cols])
        proj_ref[rows, f_cols] = 1.0 - f
        for p, part in enumerate(_split3(jnp.log(f))):
            lf_ref[p, rows, :] = part.astype(lf_ref.dtype)
    proj_ref[:, 0:kdim] = _dot(xb, w_in_ref[:, 0:kdim])
    proj_ref[:, 2 * kdim:] = _dot(xb, w_in_ref[:, 2 * kdim:])

    ri = lax.broadcasted_iota(jnp.int32, (m, m), 0)
    ci = lax.broadcasted_iota(jnp.int32, (m, m), 1)
    causal = jnp.logical_and(ri // c == ci // c, ri >= ci)
    tri = causal.astype(BF16)
    bcum_ref[...] = (_dot(tri, lf_ref[0].astype(BF16)) + _dot(tri, lf_ref[1].astype(BF16))
                     + _dot(tri, lf_ref[2].astype(BF16)))

    sdt = qrel_ref.dtype
    vb_ref[...] = proj_ref[:, 2 * kdim:2 * kdim + vdim].astype(sdt)
    for j in range(n_chunks):
        rows = slice(j * c, (j + 1) * c)
        bcum = bcum_ref[rows, :]
        bmid = bcum[mid:mid + 1]
        blast = bcum[c - 1:c]
        q_dec = proj_ref[rows, 0:kdim] * jnp.exp(bcum)
        k_rel = proj_ref[rows, f_cols] * jnp.exp(bmid - bcum)
        qdec_ref[rows, :] = q_dec.astype(sdt)
        qrel_ref[rows, :] = (q_dec * jnp.exp(-bmid)).astype(sdt)
        krel_ref[rows, :] = k_rel.astype(sdt)
        kend_ref[rows, :] = (k_rel * jnp.exp(blast - bmid)).astype(sdt)
        dec_ref[j:j + 1, :] = jnp.exp(blast)
    if n_chunks < dec_ref.shape[0]:
        dec_ref[n_chunks:, :] = jnp.zeros((dec_ref.shape[0] - n_chunks, kdim), F32)
    dec_cols = dec_ref[...].T

    for h in range(heads):
        sk = slice(h * dk, (h + 1) * dk)
        sv = slice(h * dv, (h + 1) * dv)
        a = lax.dot_general(qrel_ref[:, sk].astype(BF16), krel_ref[:, sk].astype(BF16), _NT,
                            preferred_element_type=F32)
        a = jnp.where(causal, a, 0.0).astype(BF16)
        oacc_ref[:, sv] = _dot(a, vb_ref[:, sv].astype(BF16))

    for j in range(n_chunks):
        rows = slice(j * c, (j + 1) * c)
        for h in range(heads):
            sk = slice(h * dk, (h + 1) * dk)
            sv = slice(h * dv, (h + 1) * dv)
            u_ref[j, h] = lax.dot_general(kend_ref[rows, sk].astype(BF16), vb_ref[rows, sv].astype(BF16), _TN,
                                          preferred_element_type=F32)

    for j in range(n_chunks):
        rows = slice(j * c, (j + 1) * c)
        bi = j // chunks_per_seq
        for h in range(heads):
            sk = slice(h * dk, (h + 1) * dk)
            sv = slice(h * dv, (h + 1) * dv)
            st = st_refs[h][bi]
            oacc_ref[rows, sv] += _dot(qdec_ref[rows, sk].astype(BF16), st.astype(BF16))
            st_refs[h][bi] = st * dec_cols[sk, j:j + 1] + u_ref[j, h]

    ng = ng_ref[...]
    for h in range(heads):
        sv = slice(h * dv, (h + 1) * dv)
        o = oacc_ref[:, sv]
        o = o * lax.rsqrt(jnp.mean(o * o, axis=-1, keepdims=True) + RMS_EPS) * ng
        gate = proj_ref[:, 2 * kdim + vdim + h * dv:2 * kdim + vdim + (h + 1) * dv]
        og_ref[:, sv] = (o * (gate * jax.nn.sigmoid(gate))).astype(BF16)
    out = _dot(og_ref[...], w_o_ref[...])
    o_ref[...] = _layer_norm(alpha * x + out, g_ref[...], b_ref[...]).reshape(nb, tt, d)

    @pl.when(t == pl.num_programs(1) - 1)
    def _():
        for bi in range(nb):
            for h in range(heads):
                sfin_ref[bi, h] = st_refs[h][bi]


def _hgrn_tiled(x, s0, w_in, lb_param, norm_g, w_o, ln_g, ln_b, alpha, layer, chunk, nb, tt):
    b, t, d = x.shape
    heads = HG_HEADS
    dv = norm_g.shape[-1]
    vdim = heads * dv
    kdim = (w_in.shape[1] - 2 * vdim) // 2
    dk = kdim // heads
    m = nb * tt
    n_chunks = m // chunk
    has_state = s0 is not None
    const2 = lambda i, j: (0, 0)
    state_spec = pl.BlockSpec((nb, heads, dk, dv), lambda i, j: (i, 0, 0, 0))
    in_specs = [pl.BlockSpec((nb, tt, d), lambda i, j: (i, j, 0))]
    args = [x]
    if has_state:
        in_specs.append(state_spec)
        args.append(s0)
    in_specs += [pl.BlockSpec(w_in.shape, const2), pl.BlockSpec(lb_param.shape, const2),
                 pl.BlockSpec(norm_g.shape, const2), pl.BlockSpec(w_o.shape, const2),
                 pl.BlockSpec((1, d), const2), pl.BlockSpec((1, d), const2)]
    args += [w_in, lb_param, norm_g, w_o, ln_g, ln_b]
    sdt = BF16 if chunk % 16 == 0 else F32
    scratch = [pltpu.VMEM((m, w_in.shape[1]), F32),
               pltpu.VMEM((3, m, kdim), sdt),
               pltpu.VMEM((m, kdim), F32),
               pltpu.VMEM((m, kdim), sdt),
               pltpu.VMEM((m, kdim), sdt),
               pltpu.VMEM((m, kdim), sdt),
               pltpu.VMEM((m, kdim), sdt),
               pltpu.VMEM((m, vdim), sdt),
               pltpu.VMEM((max(n_chunks, 8), kdim), F32),
               pltpu.VMEM((n_chunks, heads, dk, dv), F32),
               pltpu.VMEM((m, vdim), F32),
               pltpu.VMEM((m, vdim), BF16)]
    scratch += [pltpu.VMEM((nb, dk, dv), F32) for _ in range(heads)]
    return pl.pallas_call(
        functools.partial(_hgrn_tile_kernel, alpha=alpha, layer=layer, chunk=chunk, has_state=has_state),
        grid=(b // nb, t // tt),
        in_specs=in_specs,
        out_specs=[pl.BlockSpec((nb, tt, d), lambda i, j: (i, j, 0)), state_spec],
        out_shape=[jax.ShapeDtypeStruct(x.shape, F32),
                   jax.ShapeDtypeStruct((b, heads, dk, dv), F32)],
        scratch_shapes=scratch,
        compiler_params=_cparams("parallel", "arbitrary"),
        name="hgrn_state" if has_state else "hgrn_fresh",
    )(*args)


_I_E0, _I_E1, _I_G0, _I_G1, _I_R0, _I_R1 = range(6)
ROUTE_ROWS = 8


def _router_kernel(x_ref, w_ref, cnt0_ref, info_ref, route_ref, cnt_ref, carry_ref):
    i = pl.program_id(0)
    tm = x_ref.shape[0]

    @pl.when(i == 0)
    def _():
        carry_ref[...] = cnt0_ref[...]

    x = x_ref[...]
    w = w_ref[...]
    x_hi = x.astype(BF16)
    x_lo = (x - x_hi.astype(F32)).astype(BF16)
    w_hi = w.astype(BF16)
    w_lo = (w - w_hi.astype(F32)).astype(BF16)
    hi_both = _dot(x_hi, jnp.concatenate([w_hi, w_lo], axis=1))
    logits = hi_both[:, :LANES] + (hi_both[:, LANES:] + _dot(x_lo, w_hi))
    lane = lax.broadcasted_iota(jnp.int32, (tm, LANES), 1)
    valid = lane < N_EXPERTS
    logits = jnp.where(valid, logits, -jnp.inf)
    p = jnp.exp(logits - jnp.max(logits, axis=-1, keepdims=True))
    p = p / jnp.sum(p, axis=-1, keepdims=True)
    p0 = jnp.where(valid, p, -1.0)
    v0 = jnp.max(p0, axis=-1, keepdims=True)
    e0 = jnp.min(jnp.where(p0 == v0, lane, LANES), axis=-1, keepdims=True)
    p1 = jnp.where(lane == e0, -1.0, p0)
    v1 = jnp.max(p1, axis=-1, keepdims=True)
    e1 = jnp.min(jnp.where(p1 == v1, lane, LANES), axis=-1, keepdims=True)
    den = v0 + v1
    sel = jnp.logical_or(lane == e0, lane == e1)
    ri = lax.broadcasted_iota(jnp.int32, (tm, tm), 0)
    ci = lax.broadcasted_iota(jnp.int32, (tm, tm), 1)
    before = (ri > ci).astype(BF16)
    rank = _dot(before, sel.astype(BF16)) + carry_ref[...]
    r0 = jnp.sum(jnp.where(lane == e0, rank, 0.0), axis=-1, keepdims=True)
    r1 = jnp.sum(jnp.where(lane == e1, rank, 0.0), axis=-1, keepdims=True)
    carry_ref[...] += jnp.sum(sel.astype(F32), axis=0, keepdims=True)
    info = jnp.zeros((tm, LANES), F32)
    for idx, val in ((_I_E0, e0.astype(F32)), (_I_E1, e1.astype(F32)), (_I_G0, v0 / den),
                     (_I_G1, v1 / den), (_I_R0, r0), (_I_R1, r1)):
        info = jnp.where(lane == idx, val, info)
    info_ref[...] = info
    route_ref[...] = info.T[0:route_ref.shape[0], :]
    cnt_ref[...] = carry_ref[...]


def _router(x2d, w_router_padded, counts_in, tm):
    n, d = x2d.shape
    return pl.pallas_call(
        _router_kernel,
        grid=(n // tm,),
        in_specs=[pl.BlockSpec((tm, d), lambda i: (i, 0)),
                  pl.BlockSpec((d, LANES), lambda i: (0, 0)),
                  pl.BlockSpec((1, LANES), lambda i: (0, 0))],
        out_specs=[pl.BlockSpec((tm, LANES), lambda i: (i, 0)),
                   pl.BlockSpec((ROUTE_ROWS, tm), lambda i: (0, i)),
                   pl.BlockSpec((1, LANES), lambda i: (0, 0))],
        out_shape=[jax.ShapeDtypeStruct((n, LANES), F32), jax.ShapeDtypeStruct((ROUTE_ROWS, n), F32),
                   jax.ShapeDtypeStruct((1, LANES), F32)],
        scratch_shapes=[pltpu.VMEM((1, LANES), F32)],
        compiler_params=_cparams("arbitrary"),
        name="moe_router",
    )(x2d, w_router_padded, counts_in)


def _positions_kernel(offs_ref, route_ref, pos_ref):
    route = route_ref[...].astype(jnp.int32)
    expert = route[_I_E0:_I_E1 + 1]
    start = jnp.zeros_like(expert)
    for e in range(N_EXPERTS):
        start = jnp.where(expert == e, offs_ref[e], start)
    pos_ref[...] = start + route[_I_R0:_I_R1 + 1]


def _positions(route, offs, tn):
    n = route.shape[1]
    grid_spec = pltpu.PrefetchScalarGridSpec(
        num_scalar_prefetch=1,
        grid=(n // tn,),
        in_specs=[pl.BlockSpec((ROUTE_ROWS, tn), lambda i, o: (0, i))],
        out_specs=pl.BlockSpec((TOP_K, tn), lambda i, o: (0, i)),
    )
    return pl.pallas_call(
        _positions_kernel,
        grid_spec=grid_spec,
        out_shape=jax.ShapeDtypeStruct((TOP_K, n), jnp.int32),
        compiler_params=_cparams("parallel"),
        name="moe_positions",
    )(offs, route)


def _row_copy(src, src_row, dst, dst_row, sem):
    return pltpu.make_async_copy(src.at[pl.ds(src_row, 1)], dst.at[pl.ds(dst_row, 1)], sem)


def _scatter_rows(pos_refs, x_ref, out_hbm, sem):
    tb = x_ref.shape[0]

    def issue(j, carry):
        for pos_ref in pos_refs:
            _row_copy(x_ref, j, out_hbm, pos_ref[j], sem).start()
        return carry

    lax.fori_loop(0, tb, issue, 0, unroll=4)
    for _ in range(TOP_K):
        pltpu.make_async_copy(x_ref, out_hbm.at[pl.ds(0, tb)], sem).wait()


def _dispatch_kernel(pos0_ref, pos1_ref, fill_ref, xa_ref, xb_ref, out_hbm, zero_ref, sem, zsem, *, steps_a):
    i = pl.program_id(0)
    pos_refs = (pos0_ref, pos1_ref)

    @pl.when(i == 0)
    def _():
        zero_ref[...] = jnp.zeros_like(zero_ref)
        tz = zero_ref.shape[0]

        def fill(e):
            row = pl.multiple_of(jnp.maximum(fill_ref[e], 0), tz)
            return pltpu.make_async_copy(zero_ref, out_hbm.at[pl.ds(row, tz)], zsem)

        for e in range(fill_ref.shape[0]):
            pl.when(fill_ref[e] >= 0)(lambda e=e: fill(e).start())
        for e in range(fill_ref.shape[0]):
            pl.when(fill_ref[e] >= 0)(lambda e=e: fill(e).wait())

    pl.when(i < steps_a)(lambda: _scatter_rows(pos_refs, xa_ref, out_hbm, sem))
    pl.when(i >= steps_a)(lambda: _scatter_rows(pos_refs, xb_ref, out_hbm, sem))


def _dispatch(xa, xb, pos, fill_rows, n_rows, tz):
    d = xa.shape[1]
    tb = min(TILE_ROUTE, xa.shape[0], xb.shape[0])
    steps_a = xa.shape[0] // tb
    steps_b = xb.shape[0] // tb
    pos_spec = pl.BlockSpec((tb,), lambda i: (i,), memory_space=pltpu.SMEM)
    return pl.pallas_call(
        functools.partial(_dispatch_kernel, steps_a=steps_a),
        grid=(steps_a + steps_b,),
        in_specs=[pos_spec, pos_spec,
                  pl.BlockSpec(memory_space=pltpu.SMEM),
                  pl.BlockSpec((tb, d), lambda i: (jnp.minimum(i, steps_a - 1), 0)),
                  pl.BlockSpec((tb, d), lambda i: (jnp.maximum(i - steps_a, 0), 0))],
        out_specs=pl.BlockSpec(memory_space=pl.ANY),
        out_shape=jax.ShapeDtypeStruct((n_rows, d), xa.dtype),
        scratch_shapes=[pltpu.VMEM((tz, d), xa.dtype), pltpu.SemaphoreType.DMA(()),
                        pltpu.SemaphoreType.DMA(())],
        compiler_params=_cparams("arbitrary"),
        name="moe_dispatch",
    )(pos[0], pos[1], fill_rows, xa, xb)


def _combine_kernel(pos0_ref, pos1_ref, pos0n_ref, pos1n_ref, x_ref, info_ref, y_hbm, g_ref, b_ref, o_ref,
                    ybuf, sem, *, alpha):
    i = pl.program_id(0)
    n = pl.num_programs(0)
    tm = x_ref.shape[0]

    def issue(pos_refs, slot):
        def body(j, carry):
            for kk, pos_ref in enumerate(pos_refs):
                _row_copy(y_hbm, pos_ref[j], ybuf.at[slot, kk], j, sem.at[slot]).start()
            return carry
        lax.fori_loop(0, tm, body, 0, unroll=8)

    @pl.when(i == 0)
    def _():
        issue((pos0_ref, pos1_ref), 0)

    @pl.when(i + 1 < n)
    def _():
        issue((pos0n_ref, pos1n_ref), (i + 1) % 2)

    slot = i % 2
    for kk in range(TOP_K):
        pltpu.make_async_copy(y_hbm.at[pl.ds(0, tm)], ybuf.at[slot, kk], sem.at[slot]).wait()
    info = info_ref[...]
    moe = info[:, _I_G0:_I_G0 + 1] * ybuf[slot, 0] + info[:, _I_G1:_I_G1 + 1] * ybuf[slot, 1]
    o_ref[...] = _layer_norm(alpha * x_ref[...] + moe, g_ref[...], b_ref[...])


def _combine_ln(x2d, info, pos, y, ln_g, ln_b, alpha, tm):
    n, d = x2d.shape
    nt = n // tm
    vec = pl.BlockSpec((1, d), lambda i: (0, 0))
    pos_now = pl.BlockSpec((tm,), lambda i: (i,), memory_space=pltpu.SMEM)
    pos_next = pl.BlockSpec((tm,), lambda i: (jnp.minimum(i + 1, nt - 1),), memory_space=pltpu.SMEM)
    return pl.pallas_call(
        functools.partial(_combine_kernel, alpha=alpha),
        grid=(nt,),
        in_specs=[pos_now, pos_now, pos_next, pos_next,
                  pl.BlockSpec((tm, d), lambda i: (i, 0)),
                  pl.BlockSpec((tm, LANES), lambda i: (i, 0)),
                  pl.BlockSpec(memory_space=pl.ANY),
                  vec, vec],
        out_specs=pl.BlockSpec((tm, d), lambda i: (i, 0)),
        out_shape=jax.ShapeDtypeStruct((n, d), F32),
        scratch_shapes=[pltpu.VMEM((2, TOP_K, tm, d), F32), pltpu.SemaphoreType.DMA((2,))],
        compiler_params=_cparams("arbitrary"),
        name="moe_combine",
    )(pos[0], pos[1], pos[0], pos[1], x2d, info, y, ln_g, ln_b)


def _moe_ln(xs2d, w_router, w_gu, w_d, ln_g, ln_b, alpha, tm_expert):
    d = xs2d[0].shape[1]
    n_total = sum(x.shape[0] for x in xs2d)
    w_r = jnp.pad(w_router, ((0, 0), (0, LANES - N_EXPERTS)))
    counts = jnp.zeros((1, LANES), F32)
    infos, routes = [], []
    for x in xs2d:
        info, route, counts = _router(x, w_r, counts, min(x.shape[0], TILE_ROUTE))
        infos.append(info)
        routes.append(route)
    cnt = counts[0, :N_EXPERTS].astype(jnp.int32)
    padded = (cnt + tm_expert - 1) // tm_expert * tm_expert
    ends = jnp.cumsum(padded)
    offs = ends - padded
    n_rows = -(-(TOP_K * n_total) // tm_expert) * tm_expert + N_EXPERTS * tm_expert
    tile_start = jnp.arange(n_rows // tm_expert, dtype=jnp.int32) * tm_expert
    tile_eid = jnp.minimum(jnp.sum(tile_start[None, :] >= ends[:, None], axis=0), N_EXPERTS - 1).astype(jnp.int32)
    tile_used = (tile_start < offs[tile_eid] + cnt[tile_eid]).astype(jnp.int32)
    tail = ends[-1] + jnp.arange(N_EXPERTS, dtype=jnp.int32) * tm_expert
    fill_rows = jnp.concatenate([jnp.where(padded > 0, ends - tm_expert, -1),
                                 jnp.where(tail < n_rows, tail, -1)]).astype(jnp.int32)
    poss = [_positions(route, offs.astype(jnp.int32), route.shape[1]) for route in routes]
    xa, xb = xs2d
    buf = _dispatch(xa, xb, jnp.concatenate(poss, axis=1), fill_rows, n_rows, tm_expert)
    y = _expert_ffn(buf, tile_eid, tile_used, w_gu, w_d, tm_expert)
    return [_combine_ln(x, info, pos, y, ln_g, ln_b, alpha, min(x.shape[0], TILE_ROUTE))
            for x, info, pos in zip(xs2d, infos, poss)]


def kernel(x_prompt, x_sample, mem_prompt, state_pool, state_hgrn, cache_mem_k, cache_mem_v, pool_w, pool_scale, hg_w_in, hg_lb, hg_norm_g, hg_w_o, xa_w_q, xa_w_kv, xa_w_o, ffn_w_gu, ffn_w_d, moe_w_router, moe_w_gu, moe_w_d, ln_g, ln_b):
    b, t, d = x_prompt.shape
    sb, st, _ = x_sample.shape
    depth = ln_g.shape[0]
    mem_len = mem_prompt.shape[1]
    alpha = (2 * depth) ** 0.25
    n_prev = state_pool.shape[2]

    pool_w_b = pool_w.astype(BF16)
    xa_w_kv_b = xa_w_kv.astype(BF16)

    def vec(a):
        return a.reshape(1, -1)

    mk, mv = _mem_kv(mem_prompt.reshape(b * mem_len, d), xa_w_kv_b)
    mk4 = mk.reshape(depth, b, mem_len, d)
    mv4 = mv.reshape(depth, b, mem_len, d)

    def stored_order(c):
        halves = c.shape[4] // LANES
        c = c.reshape(depth, sb, mem_len, XA_HEADS, halves, LANES).transpose(0, 1, 2, 4, 3, 5)
        return c.reshape(depth, sb, mem_len * halves * XA_HEADS, LANES)

    ck8 = stored_order(cache_mem_k)
    cv8 = stored_order(cache_mem_v)

    tt_attn = min(t, TILE_ATTN)
    nb_attn = min(sb, SEQS_ATTN)

    def attend(xp, xs, layer, side=()):
        args = (xa_w_q_b[layer], xa_w_o_b[layer], vec(ln_g[layer, 1]), vec(ln_b[layer, 1]), alpha)
        outp = _mem_attend(xp, mk4, mv4, layer, *args, tt_attn, side)
        outs = _mem_attend_cache(xs, ck8, cv8, layer, *args, nb_attn)
        return (outp[0], outs, outp[1:]) if side else (outp, outs, [])

    def cast_beside(weights, grid):
        views = tuple(w.reshape(-1, w.shape[-1]) for w in weights)
        if _side_cast_plan(views, grid) is None:
            return (), tuple(w.astype(BF16) for w in weights)
        return views, None

    pool_args = (pool_w_b[0], vec(pool_scale[0]), vec(ln_g[0, 0]), vec(ln_b[0, 0]), alpha)
    early_w = (xa_w_q, xa_w_o, hg_w_in, hg_w_o)
    side, early_w_b = cast_beside(early_w, _pool_prompt_grid(x_prompt))
    xp = _pool_prompt(x_prompt, *pool_args, side)
    if side:
        xp, early_w_b = xp[0], tuple(c.reshape(w.shape) for c, w in zip(xp[1:], early_w))
    xa_w_q_b, xa_w_o_b, hg_w_in_b, hg_w_o_b = early_w_b
    xs, pool_s = _pool_sample(x_sample, state_pool, 0, *pool_args)
    ffn_w = (ffn_w_gu, ffn_w_d)
    side, ffn_w_b = cast_beside(ffn_w, (b, t // tt_attn))
    xp, xs, casts = attend(xp, xs, 0, side)
    if side:
        ffn_w_b = tuple(c.reshape(w.shape) for c, w in zip(casts, ffn_w))
    ffn_args = (*ffn_w_b, vec(ln_g[0, 2]), vec(ln_b[0, 2]), alpha)
    moe_w = (moe_w_gu, moe_w_d)
    tm_ffn = min(TILE_FFN_CASTING, b * t)
    side, moe_w_b = cast_beside(moe_w, (b * t // tm_ffn,))
    xp = _ffn_ln(xp.reshape(b * t, d), *ffn_args, tm_ffn, side)
    if side:
        xp, moe_w_b = xp[0], tuple(c.reshape(w.shape) for c, w in zip(xp[1:], moe_w))
    xp = xp.reshape(b, t, d)
    xs = _ffn_ln(xs.reshape(sb * st, d), *ffn_args, TILE_FFN).reshape(sb, st, d)

    hg_args = (hg_w_in_b[0], hg_lb, vec(hg_norm_g[0]), hg_w_o_b[0], vec(ln_g[1, 0]), vec(ln_b[1, 0]), alpha, 1)
    xp, hg_p = _hgrn_tiled(xp, None, *hg_args, 2 * HG_CHUNK, 1, min(t, TILE_HGRN))
    xs, hg_s = _hgrn_tiled(xs, state_hgrn[0], *hg_args, math.gcd(st, HG_CHUNK), min(sb, SEQS_HGRN), st)
    xp, xs, _ = attend(xp, xs, 1)
    yp, ys = _moe_ln([xp.reshape(b * t, d), xs.reshape(sb * st, d)], moe_w_router[0], moe_w_b[0][0],
                     moe_w_b[1][0], vec(ln_g[1, 2]), vec(ln_b[1, 2]), alpha, TILE_EXPERT)

    heads_hd = (depth, b, mem_len, XA_HEADS, d // XA_HEADS)
    pool_p = x_prompt[:, t - n_prev:, :][None]
    return (yp.reshape(b, t, d), ys.reshape(sb, st, d), pool_p, hg_p[None],
            mk.reshape(heads_hd), mv.reshape(heads_hd), pool_s[None], hg_s[None])
```

```python
import functools
import math

import jax
import jax.numpy as jnp
from jax import lax
from jax.experimental import pallas as pl
from jax.experimental.pallas import tpu as pltpu

F32 = jnp.float32
BF16 = jnp.bfloat16

POOL_WINDOWS = (2, 4, 8, 16)
POOL_HALO = 16
HG_HEADS = 8
HG_CHUNK = 32
RMS_EPS = 1e-6
XA_HEADS = 4
N_EXPERTS = 8
TOP_K = 2
LN_EPS = 1e-5
LANES = 128
VMEM_LIMIT = 62 * 1024 * 1024

TILE_KV = 512
TILE_POOL = 1024
SEQS_POOL = 16
TILE_ATTN = 1024
SEQS_ATTN = 8
TILE_FFN = 1024
TILE_FFN_CASTING = 256
TILE_EXPERT = 512
TILE_HGRN = 256
SEQS_HGRN = 8
TILE_ROUTE = 1024

_NT = (((1,), (1,)), ((), ()))
_TN = (((0,), (0,)), ((), ()))


def _cparams(*sem):
    return pltpu.CompilerParams(dimension_semantics=sem, vmem_limit_bytes=VMEM_LIMIT)


def _layer_norm(z, g, b):
    mu = jnp.mean(z, axis=-1, keepdims=True)
    zc = z - mu
    var = jnp.mean(zc * zc, axis=-1, keepdims=True)
    return zc * lax.rsqrt(var + LN_EPS) * g + b


def _dot(a, b):
    return jnp.dot(a, b, preferred_element_type=F32)


def _side_cast_plan(arrays, grid):
    steps = math.prod(grid)

    def block_index(*idx):
        lin = 0
        for g, i in zip(grid, idx):
            lin = lin * g + i
        return lin, 0

    specs, shapes = [], []
    for a in arrays:
        rows, cols = a.shape
        if rows % steps or (rows // steps) % 16:
            return None
        specs.append(pl.BlockSpec((rows // steps, cols), block_index))
        shapes.append(jax.ShapeDtypeStruct(a.shape, BF16))
    return specs, shapes


def _side_cast(in_refs, out_refs):
    for src, dst in zip(in_refs, out_refs):
        dst[...] = src[...].astype(dst.dtype)


def _kv_kernel(mem_ref, w_ref, k_ref, v_ref):
    kv = _dot(mem_ref[...].astype(BF16), w_ref[0])
    n = k_ref.shape[-1]
    k_ref[0] = kv[:, :n]
    v_ref[0] = kv[:, n:]


def _mem_kv(mem2d, w_kv):
    rows, d = mem2d.shape
    depth, _, e2 = w_kv.shape
    e = e2 // 2
    tm = min(rows, TILE_KV)
    out = jax.ShapeDtypeStruct((depth, rows, e), F32)
    return pl.pallas_call(
        _kv_kernel,
        grid=(depth, rows // tm),
        in_specs=[pl.BlockSpec((tm, d), lambda l, i: (i, 0)),
                  pl.BlockSpec((1, d, e2), lambda l, i: (l, 0, 0))],
        out_specs=[pl.BlockSpec((1, tm, e), lambda l, i: (l, i, 0)),
                   pl.BlockSpec((1, tm, e), lambda l, i: (l, i, 0))],
        out_shape=[out, out],
        compiler_params=_cparams("parallel", "parallel"),
        name="mem_kv",
    )(mem2d, w_kv)


def _window_sums(a, w):
    s, k = a, 1
    while k < w:
        s = s + pltpu.roll(s, k, 0)
        k *= 2
    return s


def _pool_mix(pooled_minus_tok, w_ref, scale):
    parts = [_dot(pooled_minus_tok[g].astype(BF16), w_ref[g]) for g in range(len(POOL_WINDOWS))]
    return jnp.concatenate(parts, axis=1) * scale


def _pool_prompt_kernel(*refs, alpha, n_side):
    x_ref, halo_ref, w_ref, scale_ref, g_ref, b_ref = refs[:6]
    o_ref = refs[6 + n_side]
    buf_ref = refs[-1]
    _side_cast(refs[6:6 + n_side], refs[7 + n_side:7 + 2 * n_side])
    t = pl.program_id(1)
    tt = x_ref.shape[1]
    gw = w_ref.shape[-1]
    x = x_ref[0]
    buf_ref[0:POOL_HALO, :] = jnp.where(t > 0, halo_ref[0], 0.0)
    buf_ref[POOL_HALO:, :] = x
    pos = t * tt + lax.broadcasted_iota(jnp.int32, (tt, 1), 0)
    diffs = []
    for g, w in enumerate(POOL_WINDOWS):
        sl = slice(g * gw, (g + 1) * gw)
        s = _window_sums(buf_ref[:, sl], w)[POOL_HALO:, :]
        cnt = jnp.minimum(pos + 1, w).astype(F32)
        diffs.append(s / cnt - x[:, sl])
    mixed = _pool_mix(diffs, w_ref, scale_ref[...])
    o_ref[0] = _layer_norm(alpha * x + mixed, g_ref[...], b_ref[...])


def _pool_prompt_grid(x):
    b, t, _ = x.shape
    return b, t // min(t, TILE_POOL)


def _pool_prompt(x, pool_w, scale, ln_g, ln_b, alpha, side=()):
    b, t, d = x.shape
    grid = _pool_prompt_grid(x)
    tt = t // grid[1]
    hb = tt // POOL_HALO
    vec = pl.BlockSpec((1, d), lambda i, j: (0, 0))
    side_specs, side_shapes = _side_cast_plan(side, grid)
    out = pl.pallas_call(
        functools.partial(_pool_prompt_kernel, alpha=alpha, n_side=len(side)),
        grid=grid,
        in_specs=[pl.BlockSpec((1, tt, d), lambda i, j: (i, j, 0)),
                  pl.BlockSpec((1, POOL_HALO, d), lambda i, j: (i, jnp.maximum(j * hb - 1, 0), 0)),
                  pl.BlockSpec(pool_w.shape, lambda i, j: (0, 0, 0)),
                  vec, vec, vec] + side_specs,
        out_specs=[pl.BlockSpec((1, tt, d), lambda i, j: (i, j, 0))] + side_specs,
        out_shape=[jax.ShapeDtypeStruct(x.shape, F32)] + side_shapes,
        scratch_shapes=[pltpu.VMEM((tt + POOL_HALO, d), F32)],
        compiler_params=_cparams("arbitrary", "arbitrary"),
        name="pool_prompt",
    )(x, x, pool_w, scale, ln_g, ln_b, *side)
    return out if side else out[0]


def _pool_sample_kernel(x_ref, prev_ref, w_ref, scale_ref, g_ref, b_ref, o_ref, carry_ref, buf_ref, *, alpha):
    nb, t, d = x_ref.shape
    n_prev = prev_ref.shape[1]
    assert n_prev == POOL_HALO - 1 and t % 8 == 0
    gw = w_ref.shape[-1]
    rows = POOL_HALO + t
    buf_ref[:, POOL_HALO - 8:POOL_HALO, :] = jnp.zeros((nb, 8, d), F32)
    buf_ref[:, 0:n_prev, :] = prev_ref[...]
    head = buf_ref[:, 0:POOL_HALO, :].reshape(nb * POOL_HALO, d)
    buf_ref[:, 0:POOL_HALO, :] = pltpu.roll(head, 1, 0).reshape(nb, POOL_HALO, d)
    buf_ref[:, POOL_HALO:, :] = x_ref[...]
    first = POOL_HALO + t - n_prev
    whole = buf_ref[...].reshape(nb * rows, d)
    carry_ref[...] = pltpu.roll(whole, nb * rows - first, 0).reshape(nb, rows, d)[:, 0:n_prev, :]
    x = x_ref[...].reshape(nb * t, d)
    pos = lax.broadcasted_iota(jnp.int32, (nb, t, 1), 1).reshape(nb * t, 1)
    diffs = []
    for g, w in enumerate(POOL_WINDOWS):
        sl = slice(g * gw, (g + 1) * gw)
        s = _window_sums(buf_ref[:, :, sl].reshape(nb * rows, gw), w)
        s = s.reshape(nb, rows, gw)[:, POOL_HALO:, :].reshape(nb * t, gw)
        cnt = jnp.minimum(pos + 1 + n_prev, w).astype(F32)
        diffs.append(s / cnt - x[:, sl])
    mixed = _pool_mix(diffs, w_ref, scale_ref[...])
    o_ref[...] = _layer_norm(alpha * x + mixed, g_ref[...], b_ref[...]).reshape(nb, t, d)


def _pool_sample(x, state_pool, layer, pool_w, scale, ln_g, ln_b, alpha):
    b, t, d = x.shape
    n_prev = state_pool.shape[2]
    nb = min(b, SEQS_POOL)
    vec = pl.BlockSpec((1, d), lambda i: (0, 0))
    return pl.pallas_call(
        functools.partial(_pool_sample_kernel, alpha=alpha),
        grid=(b // nb,),
        in_specs=[pl.BlockSpec((nb, t, d), lambda i: (i, 0, 0)),
                  pl.BlockSpec((None, nb, n_prev, d), lambda i: (layer, i, 0, 0)),
                  pl.BlockSpec(pool_w.shape, lambda i: (0, 0, 0)),
                  vec, vec, vec],
        out_specs=[pl.BlockSpec((nb, t, d), lambda i: (i, 0, 0)),
                   pl.BlockSpec((nb, n_prev, d), lambda i: (i, 0, 0))],
        out_shape=[jax.ShapeDtypeStruct(x.shape, F32), jax.ShapeDtypeStruct((b, n_prev, d), F32)],
        scratch_shapes=[pltpu.VMEM((nb, POOL_HALO + t, d), F32)],
        compiler_params=_cparams("parallel"),
        name="pool_sample",
    )(x, state_pool, pool_w, scale, ln_g, ln_b)


def _attn_kernel(*refs, alpha, n_side):
    o_ref = refs[7 + n_side]
    _side_cast(refs[7:7 + n_side], refs[8 + n_side:8 + 2 * n_side])
    _attn_body(*refs[:7], o_ref, *refs[-3:], alpha)


def _attn_body(x_ref, k_ref, v_ref, wq_ref, wo_ref, g_ref, b_ref, o_ref, oh_ref, s_ref, p_ref, alpha):
    tt, d = x_ref.shape
    hd = d // XA_HEADS
    scale = hd ** -0.5
    heads = [slice(h * hd, (h + 1) * hd) for h in range(XA_HEADS)]
    n_parts = 4 if tt % 64 == 0 else 1
    parts = [slice(r * (tt // n_parts), (r + 1) * (tt // n_parts)) for r in range(n_parts)]
    xs = [x_ref[rows, :] for rows in parts]
    for rows, x in zip(parts, xs):
        oh_ref[rows, :] = _dot(x.astype(BF16), wq_ref[...]).astype(BF16)
    for rows in parts:
        for h, sl in enumerate(heads):
            s_ref[h, rows, :] = lax.dot_general(oh_ref[rows, sl], k_ref[:, sl].astype(BF16), _NT,
                                                preferred_element_type=F32) * scale
    for rows in parts:
        for h in range(XA_HEADS):
            s = s_ref[h, rows, :]
            p = jnp.exp(s - jnp.max(s, axis=-1, keepdims=True))
            p_ref[h, rows, :] = (p / jnp.sum(p, axis=-1, keepdims=True)).astype(BF16)
    for rows in parts:
        for h, sl in enumerate(heads):
            oh_ref[rows, sl] = _dot(p_ref[h, rows, :], v_ref[:, sl].astype(BF16)).astype(BF16)
    for rows, x in zip(parts, xs):
        a = _dot(oh_ref[rows, :], wo_ref[...])
        o_ref[rows, :] = _layer_norm(alpha * x + a, g_ref[...], b_ref[...])


def _mem_attend(x, k4, v4, layer, wq, wo, ln_g, ln_b, alpha, tt, side=()):
    b, t, d = x.shape
    m = k4.shape[2]
    grid = (b, t // tt)
    vec = pl.BlockSpec((1, d), lambda i, j: (0, 0))
    mat = pl.BlockSpec((d, d), lambda i, j: (0, 0))
    kv = pl.BlockSpec((None, None, m, d), lambda i, j: (layer, i, 0, 0))
    side_specs, side_shapes = _side_cast_plan(side, grid)
    out = pl.pallas_call(
        functools.partial(_attn_kernel, alpha=alpha, n_side=len(side)),
        grid=grid,
        in_specs=[pl.BlockSpec((None, tt, d), lambda i, j: (i, j, 0)), kv, kv, mat, mat, vec, vec] + side_specs,
        out_specs=[pl.BlockSpec((None, tt, d), lambda i, j: (i, j, 0))] + side_specs,
        out_shape=[jax.ShapeDtypeStruct(x.shape, F32)] + side_shapes,
        scratch_shapes=[pltpu.VMEM((tt, d), BF16), pltpu.VMEM((XA_HEADS, tt, m), F32),
                        pltpu.VMEM((XA_HEADS, tt, m), BF16)],
        compiler_params=_cparams("arbitrary", "arbitrary"),
        name=f"mem_attend_l{layer}",
    )(x, k4, v4, wq, wo, ln_g, ln_b, *side)
    return out if side else out[0]


def _attn_cache_kernel(*refs, alpha):
    _attn_cache_body(*refs, alpha)


def _attn_cache_body(x_ref, k_ref, v_ref, wq_ref, wo_ref, g_ref, b_ref, o_ref, q_scr, oh_scr, s_scr, p_scr, alpha):
    nb, t, d = x_ref.shape
    heads = XA_HEADS
    hd = d // heads
    halves = hd // LANES
    assert halves == 2
    rows = k_ref.shape[1]
    qr = heads * t
    scale = hd ** -0.5
    x = x_ref[...].reshape(nb * t, d)
    q = _dot(x.astype(BF16), wq_ref[...])
    for c in range(halves):
        for h in range(heads):
            col = (h * halves + c) * LANES
            q_scr[:, c * qr + h * t:c * qr + (h + 1) * t, :] = q[:, col:col + LANES].reshape(nb, t, LANES)

    col_id = lax.broadcasted_iota(jnp.int32, (qr, rows), 1)
    row_head = lax.broadcasted_iota(jnp.int32, (qr, rows), 0) // t
    col_half = (col_id // heads) % halves
    col_head = col_id % heads
    keep = jnp.logical_and(col_half == 1, col_head == row_head)

    for b in range(nb):
        g = lax.dot_general(q_scr[b].astype(BF16), k_ref[b].astype(BF16), _NT, preferred_element_type=F32)
        s_scr[b] = jnp.where(col_half == 0, g[:qr], g[qr:])
    for b in range(nb):
        part = s_scr[b]
        s = (part + pltpu.roll(part, heads, 1)) * scale
        s = jnp.where(keep, s, -jnp.inf)
        p = jnp.exp(s - jnp.max(s, axis=-1, keepdims=True))
        p = p / jnp.sum(p, axis=-1, keepdims=True)
        p_scr[b, 0:qr, :] = pltpu.roll(p, rows - heads, 1).astype(BF16)
        p_scr[b, qr:, :] = p.astype(BF16)
    for b in range(nb):
        o2 = _dot(p_scr[b], v_ref[b].astype(BF16))
        for c in range(halves):
            for h in range(heads):
                col = (h * halves + c) * LANES
                oh_scr[b * t:(b + 1) * t, col:col + LANES] = o2[c * qr + h * t:c * qr + (h + 1) * t, :]
    a = _dot(oh_scr[...].astype(BF16), wo_ref[...])
    o_ref[...] = _layer_norm(alpha * x + a, g_ref[...], b_ref[...]).reshape(nb, t, d)


def _mem_attend_cache(x, k8, v8, layer, wq, wo, ln_g, ln_b, alpha, nb):
    b, t, d = x.shape
    rows = k8.shape[2]
    vec = pl.BlockSpec((1, d), lambda i: (0, 0))
    mat = pl.BlockSpec((d, d), lambda i: (0, 0))
    kv = pl.BlockSpec((None, nb, rows, LANES), lambda i: (layer, i, 0, 0))
    return pl.pallas_call(
        functools.partial(_attn_cache_kernel, alpha=alpha),
        grid=(b // nb,),
        in_specs=[pl.BlockSpec((nb, t, d), lambda i: (i, 0, 0)), kv, kv, mat, mat, vec, vec],
        out_specs=pl.BlockSpec((nb, t, d), lambda i: (i, 0, 0)),
        out_shape=jax.ShapeDtypeStruct(x.shape, F32),
        scratch_shapes=[pltpu.VMEM((nb, 2 * XA_HEADS * t, LANES), F32), pltpu.VMEM((nb * t, d), F32),
                        pltpu.VMEM((nb, XA_HEADS * t, rows), F32), pltpu.VMEM((nb, 2 * XA_HEADS * t, rows), BF16)],
        compiler_params=_cparams("parallel"),
        name=f"mem_attend_cache_l{layer}",
    )(x, k8, v8, wq, wo, ln_g, ln_b)


FFN_SUB = 512


def _swiglu(x_ref, wg_ref, wu_ref, wd_ref, h_ref):
    xb = x_ref[...].astype(BF16)
    f = h_ref.shape[1]
    for s in range(f // FFN_SUB):
        sl = slice(s * FFN_SUB, (s + 1) * FFN_SUB)
        gate = _dot(xb, wg_ref[0, :, sl])
        up = _dot(xb, wu_ref[0, :, sl])
        h_ref[:, sl] = (gate * jax.nn.sigmoid(gate) * up).astype(BF16)
    return _dot(h_ref[...], wd_ref[0])


def _ffn_ln_kernel(*refs, alpha, n_side):
    x_ref, wg_ref, wu_ref, wd_ref, g_ref, b_ref = refs[:6]
    o_ref = refs[6 + n_side]
    h_ref = refs[-1]
    y = _swiglu(x_ref, wg_ref, wu_ref, wd_ref, h_ref)
    o_ref[...] = _layer_norm(alpha * x_ref[...] + y, g_ref[...], b_ref[...])
    _side_cast(refs[6:6 + n_side], refs[7 + n_side:7 + 2 * n_side])


def _ffn_ln(x2d, w_gu, w_d, ln_g, ln_b, alpha, tm, side=()):
    n, d = x2d.shape
    tm = min(tm, n)
    f = w_d.shape[1]
    grid = (n // tm,)
    vec = pl.BlockSpec((1, d), lambda i: (0, 0))
    once = pl.Buffered(1)
    side_specs, side_shapes = _side_cast_plan(side, grid)
    out = pl.pallas_call(
        functools.partial(_ffn_ln_kernel, alpha=alpha, n_side=len(side)),
        grid=grid,
        in_specs=[pl.BlockSpec((tm, d), lambda i: (i, 0)),
                  pl.BlockSpec((1, d, f), lambda i: (0, 0, 0), pipeline_mode=once),
                  pl.BlockSpec((1, d, f), lambda i: (0, 0, 1), pipeline_mode=once),
                  pl.BlockSpec((1, f, d), lambda i: (0, 0, 0), pipeline_mode=once),
                  vec, vec] + side_specs,
        out_specs=[pl.BlockSpec((tm, d), lambda i: (i, 0))] + side_specs,
        out_shape=[jax.ShapeDtypeStruct((n, d), F32)] + side_shapes,
        scratch_shapes=[pltpu.VMEM((tm, f), BF16)],
        compiler_params=_cparams("parallel"),
        name="ffn_dense",
    )(x2d, w_gu, w_gu, w_d, ln_g, ln_b, *side)
    return out if side else out[0]


def _expert_kernel(eid_ref, used_ref, x_ref, wg_ref, wu_ref, wd_ref, o_ref, h_ref):
    i = pl.program_id(0)

    @pl.when(used_ref[i] > 0)
    def _():
        o_ref[...] = _swiglu(x_ref, wg_ref, wu_ref, wd_ref, h_ref)

    @pl.when(used_ref[i] == 0)
    def _():
        o_ref[...] = jnp.zeros_like(o_ref)


def _expert_ffn(xs, tile_eid, tile_used, w_gu, w_d, tm):
    r, d = xs.shape
    f = w_d.shape[1]
    grid_spec = pltpu.PrefetchScalarGridSpec(
        num_scalar_prefetch=2,
        grid=(r // tm,),
        in_specs=[pl.BlockSpec((tm, d), lambda i, e, u: (i, 0)),
                  pl.BlockSpec((1, d, f), lambda i, e, u: (e[i], 0, 0)),
                  pl.BlockSpec((1, d, f), lambda i, e, u: (e[i], 0, 1)),
                  pl.BlockSpec((1, f, d), lambda i, e, u: (e[i], 0, 0))],
        out_specs=pl.BlockSpec((tm, d), lambda i, e, u: (i, 0)),
        scratch_shapes=[pltpu.VMEM((tm, f), BF16)],
    )
    return pl.pallas_call(
        _expert_kernel,
        grid_spec=grid_spec,
        out_shape=jax.ShapeDtypeStruct((r, d), F32),
        compiler_params=_cparams("arbitrary"),
        name="ffn_experts",
    )(tile_eid, tile_used, xs, w_gu, w_gu, w_d)


def _split3(x):
    hi = x.astype(BF16).astype(F32)
    r = x - hi
    mid = r.astype(BF16).astype(F32)
    lo = (r - mid).astype(BF16).astype(F32)
    return hi, mid, lo


def _hgrn_tile_kernel(*refs, alpha, layer, chunk, has_state):
    st_refs = refs[-HG_HEADS:]
    refs = refs[:-HG_HEADS]
    s0_ref = None
    if has_state:
        s0_ref, refs = refs[1], refs[:1] + refs[2:]
    (x_ref, w_in_ref, lbp_ref, ng_ref, w_o_ref, g_ref, b_ref, o_ref, sfin_ref,
     proj_ref, lf_ref, bcum_ref, qrel_ref, krel_ref, kend_ref, qdec_ref, vb_ref, dec_ref,
     u_ref, oacc_ref, og_ref) = refs
    t = pl.program_id(1)
    nb, tt, d = x_ref.shape
    heads = HG_HEADS
    dv = ng_ref.shape[-1]
    vdim = heads * dv
    kdim = (w_in_ref.shape[1] - 2 * vdim) // 2
    dk = kdim // heads
    m = nb * tt
    c = chunk
    n_chunks = m // c
    chunks_per_seq = tt // c
    mid = max(c // 2 - 1, 0)

    @pl.when(t == 0)
    def _():
        for h in range(heads):
            if has_state:
                for bi in range(nb):
                    st_refs[h][bi] = s0_ref[bi, h]
            else:
                st_refs[h][...] = jnp.zeros_like(st_refs[h])

    x = x_ref[...].reshape(m, d)
    xb = x.astype(BF16)

    lbp = lbp_ref[...]
    e = jnp.exp(lbp - jnp.max(lbp, axis=0, keepdims=True))
    sm = e / jnp.sum(e, axis=0, keepdims=True)
    lb = jnp.sum(sm[1:layer + 1], axis=0, keepdims=True)

    f_cols = slice(kdim, 2 * kdim)
    proj_ref[:, f_cols] = _dot(xb, w_in_ref[:, f_cols])
    for j in range(n_chunks):
        rows = slice(j * c, (j + 1) * c)
        f = lb + (1.0 - lb) * jax.nn.sigmoid(proj_ref[rows, f_cols])
        proj_ref[rows, f_cols] = 1.0 - f
        for p, part in enumerate(_split3(jnp.log(f))):
            lf_ref[p, rows, :] = part.astype(lf_ref.dtype)
    proj_ref[:, 0:kdim] = _dot(xb, w_in_ref[:, 0:kdim])
    proj_ref[:, 2 * kdim:] = _dot(xb, w_in_ref[:, 2 * kdim:])

    ri = lax.broadcasted_iota(jnp.int32, (m, m), 0)
    ci = lax.broadcasted_iota(jnp.int32, (m, m), 1)
    causal = jnp.logical_and(ri // c == ci // c, ri >= ci)
    tri = causal.astype(BF16)
    bcum_ref[...] = (_dot(tri, lf_ref[0].astype(BF16)) + _dot(tri, lf_ref[1].astype(BF16))
                     + _dot(tri, lf_ref[2].astype(BF16)))

    sdt = qrel_ref.dtype
    vb_ref[...] = proj_ref[:, 2 * kdim:2 * kdim + vdim].astype(sdt)
    for j in range(n_chunks):
        rows = slice(j * c, (j + 1) * c)
        bcum = bcum_ref[rows, :]
        bmid = bcum[mid:mid + 1]
        blast = bcum[c - 1:c]
        q_dec = proj_ref[rows, 0:kdim] * jnp.exp(bcum)
        k_rel = proj_ref[rows, f_cols] * jnp.exp(bmid - bcum)
        qdec_ref[rows, :] = q_dec.astype(sdt)
        qrel_ref[rows, :] = (q_dec * jnp.exp(-bmid)).astype(sdt)
        krel_ref[rows, :] = k_rel.astype(sdt)
        kend_ref[rows, :] = (k_rel * jnp.exp(blast - bmid)).astype(sdt)
        dec_ref[j:j + 1, :] = jnp.exp(blast)
    if n_chunks < dec_ref.shape[0]:
        dec_ref[n_chunks:, :] = jnp.zeros((dec_ref.shape[0] - n_chunks, kdim), F32)
    dec_cols = dec_ref[...].T

    for h in range(heads):
        sk = slice(h * dk, (h + 1) * dk)
        sv = slice(h * dv, (h + 1) * dv)
        a = lax.dot_general(qrel_ref[:, sk].astype(BF16), krel_ref[:, sk].astype(BF16), _NT,
                            preferred_element_type=F32)
        a = jnp.where(causal, a, 0.0).astype(BF16)
        oacc_ref[:, sv] = _dot(a, vb_ref[:, sv].astype(BF16))

    for j in range(n_chunks):
        rows = slice(j * c, (j + 1) * c)
        for h in range(heads):
            sk = slice(h * dk, (h + 1) * dk)
            sv = slice(h * dv, (h + 1) * dv)
            u_ref[j, h] = lax.dot_general(kend_ref[rows, sk].astype(BF16), vb_ref[rows, sv].astype(BF16), _TN,
                                          preferred_element_type=F32)

    for j in range(n_chunks):
        rows = slice(j * c, (j + 1) * c)
        bi = j // chunks_per_seq
        for h in range(heads):
            sk = slice(h * dk, (h + 1) * dk)
            sv = slice(h * dv, (h + 1) * dv)
            st = st_refs[h][bi]
            oacc_ref[rows, sv] += _dot(qdec_ref[rows, sk].astype(BF16), st.astype(BF16))
            st_refs[h][bi] = st * dec_cols[sk, j:j + 1] + u_ref[j, h]

    ng = ng_ref[...]
    for h in range(heads):
        sv = slice(h * dv, (h + 1) * dv)
        o = oacc_ref[:, sv]
        o = o * lax.rsqrt(jnp.mean(o * o, axis=-1, keepdims=True) + RMS_EPS) * ng
        gate = proj_ref[:, 2 * kdim + vdim + h * dv:2 * kdim + vdim + (h + 1) * dv]
        og_ref[:, sv] = (o * (gate * jax.nn.sigmoid(gate))).astype(BF16)
    out = _dot(og_ref[...], w_o_ref[...])
    o_ref[...] = _layer_norm(alpha * x + out, g_ref[...], b_ref[...]).reshape(nb, tt, d)

    @pl.when(t == pl.num_programs(1) - 1)
    def _():
        for bi in range(nb):
            for h in range(heads):
                sfin_ref[bi, h] = st_refs[h][bi]


def _hgrn_tiled(x, s0, w_in, lb_param, norm_g, w_o, ln_g, ln_b, alpha, layer, chunk, nb, tt):
    b, t, d = x.shape
    heads = HG_HEADS
    dv = norm_g.shape[-1]
    vdim = heads * dv
    kdim = (w_in.shape[1] - 2 * vdim) // 2
    dk = kdim // heads
    m = nb * tt
    n_chunks = m // chunk
    has_state = s0 is not None
    const2 = lambda i, j: (0, 0)
    state_spec = pl.BlockSpec((nb, heads, dk, dv), lambda i, j: (i, 0, 0, 0))
    in_specs = [pl.BlockSpec((nb, tt, d), lambda i, j: (i, j, 0))]
    args = [x]
    if has_state:
        in_specs.append(state_spec)
        args.append(s0)
    in_specs += [pl.BlockSpec(w_in.shape, const2), pl.BlockSpec(lb_param.shape, const2),
                 pl.BlockSpec(norm_g.shape, const2), pl.BlockSpec(w_o.shape, const2),
                 pl.BlockSpec((1, d), const2), pl.BlockSpec((1, d), const2)]
    args += [w_in, lb_param, norm_g, w_o, ln_g, ln_b]
    sdt = BF16 if chunk % 16 == 0 else F32
    scratch = [pltpu.VMEM((m, w_in.shape[1]), F32),
               pltpu.VMEM((3, m, kdim), sdt),
               pltpu.VMEM((m, kdim), F32),
               pltpu.VMEM((m, kdim), sdt),
               pltpu.VMEM((m, kdim), sdt),
               pltpu.VMEM((m, kdim), sdt),
               pltpu.VMEM((m, kdim), sdt),
               pltpu.VMEM((m, vdim), sdt),
               pltpu.VMEM((max(n_chunks, 8), kdim), F32),
               pltpu.VMEM((n_chunks, heads, dk, dv), F32),
               pltpu.VMEM((m, vdim), F32),
               pltpu.VMEM((m, vdim), BF16)]
    scratch += [pltpu.VMEM((nb, dk, dv), F32) for _ in range(heads)]
    return pl.pallas_call(
        functools.partial(_hgrn_tile_kernel, alpha=alpha, layer=layer, chunk=chunk, has_state=has_state),
        grid=(b // nb, t // tt),
        in_specs=in_specs,
        out_specs=[pl.BlockSpec((nb, tt, d), lambda i, j: (i, j, 0)), state_spec],
        out_shape=[jax.ShapeDtypeStruct(x.shape, F32),
                   jax.ShapeDtypeStruct((b, heads, dk, dv), F32)],
        scratch_shapes=scratch,
        compiler_params=_cparams("parallel", "arbitrary"),
        name="hgrn_state" if has_state else "hgrn_fresh",
    )(*args)


_I_E0, _I_E1, _I_G0, _I_G1, _I_R0, _I_R1 = range(6)
ROUTE_ROWS = 8


def _router_kernel(x_ref, w_ref, cnt0_ref, info_ref, route_ref, cnt_ref, carry_ref):
    i = pl.program_id(0)
    tm = x_ref.shape[0]

    @pl.when(i == 0)
    def _():
        carry_ref[...] = cnt0_ref[...]

    x = x_ref[...]
    w = w_ref[...]
    x_hi = x.astype(BF16)
    x_lo = (x - x_hi.astype(F32)).astype(BF16)
    w_hi = w.astype(BF16)
    w_lo = (w - w_hi.astype(F32)).astype(BF16)
    hi_both = _dot(x_hi, jnp.concatenate([w_hi, w_lo], axis=1))
    logits = hi_both[:, :LANES] + (hi_both[:, LANES:] + _dot(x_lo, w_hi))
    lane = lax.broadcasted_iota(jnp.int32, (tm, LANES), 1)
    valid = lane < N_EXPERTS
    logits = jnp.where(valid, logits, -jnp.inf)
    p = jnp.exp(logits - jnp.max(logits, axis=-1, keepdims=True))
    p = p / jnp.sum(p, axis=-1, keepdims=True)
    p0 = jnp.where(valid, p, -1.0)
    v0 = jnp.max(p0, axis=-1, keepdims=True)
    e0 = jnp.min(jnp.where(p0 == v0, lane, LANES), axis=-1, keepdims=True)
    p1 = jnp.where(lane == e0, -1.0, p0)
    v1 = jnp.max(p1, axis=-1, keepdims=True)
    e1 = jnp.min(jnp.where(p1 == v1, lane, LANES), axis=-1, keepdims=True)
    den = v0 + v1
    sel = jnp.logical_or(lane == e0, lane == e1)
    ri = lax.broadcasted_iota(jnp.int32, (tm, tm), 0)
    ci = lax.broadcasted_iota(jnp.int32, (tm, tm), 1)
    before = (ri > ci).astype(BF16)
    rank = _dot(before, sel.astype(BF16)) + carry_ref[...]
    r0 = jnp.sum(jnp.where(lane == e0, rank, 0.0), axis=-1, keepdims=True)
    r1 = jnp.sum(jnp.where(lane == e1, rank, 0.0), axis=-1, keepdims=True)
    carry_ref[...] += jnp.sum(sel.astype(F32), axis=0, keepdims=True)
    info = jnp.zeros((tm, LANES), F32)
    for idx, val in ((_I_E0, e0.astype(F32)), (_I_E1, e1.astype(F32)), (_I_G0, v0 / den),
                     (_I_G1, v1 / den), (_I_R0, r0), (_I_R1, r1)):
        info = jnp.where(lane == idx, val, info)
    info_ref[...] = info
    route_ref[...] = info.T[0:route_ref.shape[0], :]
    cnt_ref[...] = carry_ref[...]


def _router(x2d, w_router_padded, counts_in, tm):
    n, d = x2d.shape
    return pl.pallas_call(
        _router_kernel,
        grid=(n // tm,),
        in_specs=[pl.BlockSpec((tm, d), lambda i: (i, 0)),
                  pl.BlockSpec((d, LANES), lambda i: (0, 0)),
                  pl.BlockSpec((1, LANES), lambda i: (0, 0))],
        out_specs=[pl.BlockSpec((tm, LANES), lambda i: (i, 0)),
                   pl.BlockSpec((ROUTE_ROWS, tm), lambda i: (0, i)),
                   pl.BlockSpec((1, LANES), lambda i: (0, 0))],
        out_shape=[jax.ShapeDtypeStruct((n, LANES), F32), jax.ShapeDtypeStruct((ROUTE_ROWS, n), F32),
                   jax.ShapeDtypeStruct((1, LANES), F32)],
        scratch_shapes=[pltpu.VMEM((1, LANES), F32)],
        compiler_params=_cparams("arbitrary"),
        name="moe_router",
    )(x2d, w_router_padded, counts_in)


def _positions_kernel(offs_ref, route_ref, pos_ref):
    route = route_ref[...].astype(jnp.int32)
    expert = route[_I_E0:_I_E1 + 1]
    start = jnp.zeros_like(expert)
    for e in range(N_EXPERTS):
        start = jnp.where(expert == e, offs_ref[e], start)
    pos_ref[...] = start + route[_I_R0:_I_R1 + 1]


def _positions(route, offs, tn):
    n = route.shape[1]
    grid_spec = pltpu.PrefetchScalarGridSpec(
        num_scalar_prefetch=1,
        grid=(n // tn,),
        in_specs=[pl.BlockSpec((ROUTE_ROWS, tn), lambda i, o: (0, i))],
        out_specs=pl.BlockSpec((TOP_K, tn), lambda i, o: (0, i)),
    )
    return pl.pallas_call(
        _positions_kernel,
        grid_spec=grid_spec,
        out_shape=jax.ShapeDtypeStruct((TOP_K, n), jnp.int32),
        compiler_params=_cparams("parallel"),
        name="moe_positions",
    )(offs, route)


def _row_copy(src, src_row, dst, dst_row, sem):
    return pltpu.make_async_copy(src.at[pl.ds(src_row, 1)], dst.at[pl.ds(dst_row, 1)], sem)


def _scatter_rows(pos_refs, x_ref, out_hbm, sem):
    tb = x_ref.shape[0]

    def issue(j, carry):
        for kk, pos_ref in enumerate(pos_refs):
            _row_copy(x_ref, j, out_hbm, pos_ref[j], sem).start(priority=kk % 2)
        return carry

    lax.fori_loop(0, tb, issue, 0, unroll=4)
    for _ in range(TOP_K):
        pltpu.make_async_copy(x_ref, out_hbm.at[pl.ds(0, tb)], sem).wait()


def _dispatch_kernel(pos0_ref, pos1_ref, fill_ref, xa_ref, xb_ref, out_hbm, zero_ref, sem, zsem, *, steps_a):
    i = pl.program_id(0)
    pos_refs = (pos0_ref, pos1_ref)

    @pl.when(i == 0)
    def _():
        zero_ref[...] = jnp.zeros_like(zero_ref)
        tz = zero_ref.shape[0]

        def fill(e):
            row = pl.multiple_of(jnp.maximum(fill_ref[e], 0), tz)
            return pltpu.make_async_copy(zero_ref, out_hbm.at[pl.ds(row, tz)], zsem)

        for e in range(fill_ref.shape[0]):
            pl.when(fill_ref[e] >= 0)(lambda e=e: fill(e).start())
        for e in range(fill_ref.shape[0]):
            pl.when(fill_ref[e] >= 0)(lambda e=e: fill(e).wait())

    pl.when(i < steps_a)(lambda: _scatter_rows(pos_refs, xa_ref, out_hbm, sem))
    pl.when(i >= steps_a)(lambda: _scatter_rows(pos_refs, xb_ref, out_hbm, sem))


def _dispatch(xa, xb, pos, fill_rows, n_rows, tz):
    d = xa.shape[1]
    tb = min(TILE_ROUTE, xa.shape[0], xb.shape[0])
    steps_a = xa.shape[0] // tb
    steps_b = xb.shape[0] // tb
    pos_spec = pl.BlockSpec((tb,), lambda i: (i,), memory_space=pltpu.SMEM)
    return pl.pallas_call(
        functools.partial(_dispatch_kernel, steps_a=steps_a),
        grid=(steps_a + steps_b,),
        in_specs=[pos_spec, pos_spec,
                  pl.BlockSpec(memory_space=pltpu.SMEM),
                  pl.BlockSpec((tb, d), lambda i: (jnp.minimum(i, steps_a - 1), 0)),
                  pl.BlockSpec((tb, d), lambda i: (jnp.maximum(i - steps_a, 0), 0))],
        out_specs=pl.BlockSpec(memory_space=pl.ANY),
        out_shape=jax.ShapeDtypeStruct((n_rows, d), xa.dtype),
        scratch_shapes=[pltpu.VMEM((tz, d), xa.dtype), pltpu.SemaphoreType.DMA(()),
                        pltpu.SemaphoreType.DMA(())],
        compiler_params=_cparams("arbitrary"),
        name="moe_dispatch",
    )(pos[0], pos[1], fill_rows, xa, xb)


def _combine_kernel(pos0_ref, pos1_ref, pos0n_ref, pos1n_ref, x_ref, info_ref, y_hbm, g_ref, b_ref, o_ref,
                    ybuf, sem, *, alpha):
    i = pl.program_id(0)
    n = pl.num_programs(0)
    tm = x_ref.shape[0]

    def issue(pos_refs, slot):
        def body(j, carry):
            for kk, pos_ref in enumerate(pos_refs):
                _row_copy(y_hbm, pos_ref[j], ybuf.at[slot, kk], j, sem.at[slot]).start(priority=kk % 2)
            return carry
        lax.fori_loop(0, tm, body, 0, unroll=8)

    @pl.when(i == 0)
    def _():
        issue((pos0_ref, pos1_ref), 0)

    @pl.when(i + 1 < n)
    def _():
        issue((pos0n_ref, pos1n_ref), (i + 1) % 2)

    slot = i % 2
    for kk in range(TOP_K):
        pltpu.make_async_copy(y_hbm.at[pl.ds(0, tm)], ybuf.at[slot, kk], sem.at[slot]).wait()
    info = info_ref[...]
    moe = info[:, _I_G0:_I_G0 + 1] * ybuf[slot, 0] + info[:, _I_G1:_I_G1 + 1] * ybuf[slot, 1]
    o_ref[...] = _layer_norm(alpha * x_ref[...] + moe, g_ref[...], b_ref[...])


def _combine_ln(x2d, info, pos, y, ln_g, ln_b, alpha, tm):
    n, d = x2d.shape
    nt = n // tm
    vec = pl.BlockSpec((1, d), lambda i: (0, 0))
    pos_now = pl.BlockSpec((tm,), lambda i: (i,), memory_space=pltpu.SMEM)
    pos_next = pl.BlockSpec((tm,), lambda i: (jnp.minimum(i + 1, nt - 1),), memory_space=pltpu.SMEM)
    return pl.pallas_call(
        functools.partial(_combine_kernel, alpha=alpha),
        grid=(nt,),
        in_specs=[pos_now, pos_now, pos_next, pos_next,
                  pl.BlockSpec((tm, d), lambda i: (i, 0)),
                  pl.BlockSpec((tm, LANES), lambda i: (i, 0)),
                  pl.BlockSpec(memory_space=pl.ANY),
                  vec, vec],
        out_specs=pl.BlockSpec((tm, d), lambda i: (i, 0)),
        out_shape=jax.ShapeDtypeStruct((n, d), F32),
        scratch_shapes=[pltpu.VMEM((2, TOP_K, tm, d), F32), pltpu.SemaphoreType.DMA((2,))],
        compiler_params=_cparams("arbitrary"),
        name="moe_combine",
    )(pos[0], pos[1], pos[0], pos[1], x2d, info, y, ln_g, ln_b)


def _moe_ln(xs2d, w_router, w_gu, w_d, ln_g, ln_b, alpha, tm_expert):
    d = xs2d[0].shape[1]
    n_total = sum(x.shape[0] for x in xs2d)
    w_r = jnp.pad(w_router, ((0, 0), (0, LANES - N_EXPERTS)))
    counts = jnp.zeros((1, LANES), F32)
    infos, routes = [], []
    for x in xs2d:
        info, route, counts = _router(x, w_r, counts, min(x.shape[0], TILE_ROUTE))
        infos.append(info)
        routes.append(route)
    cnt = counts[0, :N_EXPERTS].astype(jnp.int32)
    padded = (cnt + tm_expert - 1) // tm_expert * tm_expert
    ends = jnp.cumsum(padded)
    offs = ends - padded
    n_rows = -(-(TOP_K * n_total) // tm_expert) * tm_expert + N_EXPERTS * tm_expert
    tile_start = jnp.arange(n_rows // tm_expert, dtype=jnp.int32) * tm_expert
    tile_eid = jnp.minimum(jnp.sum(tile_start[None, :] >= ends[:, None], axis=0), N_EXPERTS - 1).astype(jnp.int32)
    tile_used = (tile_start < offs[tile_eid] + cnt[tile_eid]).astype(jnp.int32)
    tail = ends[-1] + jnp.arange(N_EXPERTS, dtype=jnp.int32) * tm_expert
    fill_rows = jnp.concatenate([jnp.where(padded > 0, ends - tm_expert, -1),
                                 jnp.where(tail < n_rows, tail, -1)]).astype(jnp.int32)
    poss = [_positions(route, offs.astype(jnp.int32), route.shape[1]) for route in routes]
    xa, xb = xs2d
    buf = _dispatch(xa, xb, jnp.concatenate(poss, axis=1), fill_rows, n_rows, tm_expert)
    y = _expert_ffn(buf, tile_eid, tile_used, w_gu, w_d, tm_expert)
    return [_combine_ln(x, info, pos, y, ln_g, ln_b, alpha, min(x.shape[0], TILE_ROUTE))
            for x, info, pos in zip(xs2d, infos, poss)]


def kernel(x_prompt, x_sample, mem_prompt, state_pool, state_hgrn, cache_mem_k, cache_mem_v, pool_w, pool_scale, hg_w_in, hg_lb, hg_norm_g, hg_w_o, xa_w_q, xa_w_kv, xa_w_o, ffn_w_gu, ffn_w_d, moe_w_router, moe_w_gu, moe_w_d, ln_g, ln_b):
    b, t, d = x_prompt.shape
    sb, st, _ = x_sample.shape
    depth = ln_g.shape[0]
    mem_len = mem_prompt.shape[1]
    alpha = (2 * depth) ** 0.25
    n_prev = state_pool.shape[2]

    pool_w_b = pool_w.astype(BF16)
    xa_w_kv_b = xa_w_kv.astype(BF16)

    def vec(a):
        return a.reshape(1, -1)

    mk, mv = _mem_kv(mem_prompt.reshape(b * mem_len, d), xa_w_kv_b)
    mk4 = mk.reshape(depth, b, mem_len, d)
    mv4 = mv.reshape(depth, b, mem_len, d)

    def stored_order(c):
        halves = c.shape[4] // LANES
        c = c.reshape(depth, sb, mem_len, XA_HEADS, halves, LANES).transpose(0, 1, 2, 4, 3, 5)
        return c.reshape(depth, sb, mem_len * halves * XA_HEADS, LANES)

    ck8 = stored_order(cache_mem_k)
    cv8 = stored_order(cache_mem_v)

    tt_attn = min(t, TILE_ATTN)
    nb_attn = min(sb, SEQS_ATTN)

    def attend(xp, xs, layer, side=()):
        args = (xa_w_q_b[layer], xa_w_o_b[layer], vec(ln_g[layer, 1]), vec(ln_b[layer, 1]), alpha)
        outp = _mem_attend(xp, mk4, mv4, layer, *args, tt_attn, side)
        outs = _mem_attend_cache(xs, ck8, cv8, layer, *args, nb_attn)
        return (outp[0], outs, outp[1:]) if side else (outp, outs, [])

    def cast_beside(weights, grid):
        views = tuple(w.reshape(-1, w.shape[-1]) for w in weights)
        if _side_cast_plan(views, grid) is None:
            return (), tuple(w.astype(BF16) for w in weights)
        return views, None

    pool_args = (pool_w_b[0], vec(pool_scale[0]), vec(ln_g[0, 0]), vec(ln_b[0, 0]), alpha)
    early_w = (xa_w_q, xa_w_o, hg_w_in, hg_w_o)
    side, early_w_b = cast_beside(early_w, _pool_prompt_grid(x_prompt))
    xp = _pool_prompt(x_prompt, *pool_args, side)
    if side:
        xp, early_w_b = xp[0], tuple(c.reshape(w.shape) for c, w in zip(xp[1:], early_w))
    xa_w_q_b, xa_w_o_b, hg_w_in_b, hg_w_o_b = early_w_b
    xs, pool_s = _pool_sample(x_sample, state_pool, 0, *pool_args)
    ffn_w = (ffn_w_gu, ffn_w_d)
    side, ffn_w_b = cast_beside(ffn_w, (b, t // tt_attn))
    xp, xs, casts = attend(xp, xs, 0, side)
    if side:
        ffn_w_b = tuple(c.reshape(w.shape) for c, w in zip(casts, ffn_w))
    ffn_args = (*ffn_w_b, vec(ln_g[0, 2]), vec(ln_b[0, 2]), alpha)
    moe_w = (moe_w_gu, moe_w_d)
    tm_ffn = min(TILE_FFN_CASTING, b * t)
    side, moe_w_b = cast_beside(moe_w, (b * t // tm_ffn,))
    xp = _ffn_ln(xp.reshape(b * t, d), *ffn_args, tm_ffn, side)
    if side:
        xp, moe_w_b = xp[0], tuple(c.reshape(w.shape) for c, w in zip(xp[1:], moe_w))
    xp = xp.reshape(b, t, d)
    xs = _ffn_ln(xs.reshape(sb * st, d), *ffn_args, TILE_FFN).reshape(sb, st, d)

    hg_args = (hg_w_in_b[0], hg_lb, vec(hg_norm_g[0]), hg_w_o_b[0], vec(ln_g[1, 0]), vec(ln_b[1, 0]), alpha, 1)
    xp, hg_p = _hgrn_tiled(xp, None, *hg_args, 2 * HG_CHUNK, 1, min(t, TILE_HGRN))
    xs, hg_s = _hgrn_tiled(xs, state_hgrn[0], *hg_args, math.gcd(st, HG_CHUNK), min(sb, SEQS_HGRN), st)
    xp, xs, _ = attend(xp, xs, 1)
    yp, ys = _moe_ln([xp.reshape(b * t, d), xs.reshape(sb * st, d)], moe_w_router[0], moe_w_b[0][0],
                     moe_w_b[1][0], vec(ln_g[1, 2]), vec(ln_b[1, 2]), alpha, TILE_EXPERT)

    heads_hd = (depth, b, mem_len, XA_HEADS, d // XA_HEADS)
    pool_p = x_prompt[:, t - n_prev:, :][None]
    return (yp.reshape(b, t, d), ys.reshape(sb, st, d), pool_p, hg_p[None],
            mk.reshape(heads_hd), mv.reshape(heads_hd), pool_s[None], hg_s[None])
```
